```python
import math
import jax, jax.numpy as jnp
from jax import lax
import numpy as np

D_MODEL = 2048
BATCH = 16
SEQ = 2048
DEPTH = 2

CTX_LEN = 256
GRID_W = 64
N_MOD = 6
EPS = 1e-6
POS_THETA = 10000.0
D_A = D_MODEL // 2
G_A = 8
DG_A = D_A // G_A
GMLP_CHUNK = 128
D_B = D_MODEL // 2
H_B = 8
DK_B = D_B // H_B
GLA_CHUNK = 32
S5_IN = 16
S5_STATE = 64
D_C = 3 * D_MODEL // 8
G_C = D_C // S5_IN
IN_COLS = 2 * D_A + 5 * D_B + D_C + 3 * D_MODEL
PEER_HEADS = 8
PEER_NKEYS = 128
PEER_EXPERTS = PEER_NKEYS * PEER_NKEYS
PEER_TOPK = 16
PEER_DQ = 256
PEER_BLOCK = 128

kernel_name = "hybrid_gmlp_hgrn2_s5_peer_dit"


def rms_norm(x):
    xf = x.astype(jnp.float32)
    return (xf * lax.rsqrt(jnp.mean(xf * xf, axis=-1, keepdims=True) + EPS)).astype(x.dtype)


def layer_norm(x, g, b):
    xf = x.astype(jnp.float32)
    xc = xf - jnp.mean(xf, axis=-1, keepdims=True)
    var = jnp.mean(xc * xc, axis=-1, keepdims=True)
    return (xc * lax.rsqrt(var + EPS) * g.astype(jnp.float32) + b.astype(jnp.float32)).astype(x.dtype)


def pos_embed_2d(n, d):
    n_rows = n // GRID_W
    quarter = d // 4
    omega = 1.0 / (POS_THETA ** (jnp.arange(quarter, dtype=jnp.float32) / quarter))
    def axis_code(length):
        ang = jnp.arange(length, dtype=jnp.float32)[:, None] * omega[None]
        return jnp.concatenate([jnp.sin(ang), jnp.cos(ang)], axis=-1)
    er = jnp.broadcast_to(axis_code(n_rows)[:, None], (n_rows, GRID_W, d // 2))
    ec = jnp.broadcast_to(axis_code(GRID_W)[None], (n_rows, GRID_W, d // 2))
    return jnp.concatenate([er, ec], axis=-1).reshape(n, d)


def split_in_proj(p):
    sizes = (D_A, D_A, D_B, D_B, D_B, D_B, D_B, D_C, 3 * D_MODEL)
    idx = np.cumsum(sizes)[:-1].tolist()
    return jnp.split(p, idx, axis=-1)


def gmlp_branch(a_u, a_v, ln_g, ln_b, w_s, b_s):
    bsz, n, _ = a_v.shape
    u = jax.nn.gelu(a_u, approximate=False)
    v = layer_norm(jax.nn.gelu(a_v, approximate=False), ln_g, ln_b)
    v = v.reshape(bsz, n // GMLP_CHUNK, GMLP_CHUNK, G_A, DG_A)
    v = jnp.einsum('gts,bcsgd->bctgd', w_s.astype(v.dtype), v) + b_s.T[None, None, :, :, None].astype(v.dtype)
    return u * v.reshape(bsz, n, D_A)


def gla_chunk_scan(q, k, v, log_f, s0):
    bsz, n, h, _ = q.shape
    dv = v.shape[-1]
    nc = n // GLA_CHUNK
    def to_chunks(a):
        return a.reshape(bsz, nc, GLA_CHUNK, h, a.shape[-1]).transpose(1, 0, 3, 2, 4)
    mask = jnp.tril(jnp.ones((GLA_CHUNK, GLA_CHUNK), dtype=bool))[:, :, None]
    def step(s, inp):
        qc, kc, vc, lfc = inp
        b = jnp.cumsum(lfc, axis=2)
        b_last = b[:, :, -1:, :]
        o_inter = jnp.einsum('bhtk,bhkv->bhtv', qc * jnp.exp(b), s)
        diff = b[:, :, :, None, :] - b[:, :, None, :, :]
        decay = jnp.where(mask, jnp.exp(jnp.where(mask, diff, 0.0)), 0.0)
        att = jnp.einsum('bhtk,bhsk,bhtsk->bhts', qc, kc, decay)
        o = o_inter + jnp.einsum('bhts,bhsv->bhtv', att, vc)
        s_new = jnp.exp(b_last[:, :, 0, :])[..., None] * s + jnp.einsum('bhsk,bhsv->bhkv', kc * jnp.exp(b_last - b), vc)
        return s_new, o
    s_fin, o = lax.scan(step, s0, (to_chunks(q), to_chunks(k), to_chunks(v), to_chunks(log_f)))
    return o.transpose(1, 0, 3, 2, 4).reshape(bsz, n, h, dv), s_fin


def hgrn_bidir(q, f_fwd, f_bwd, i, lb, s0_f, s0_b):
    rev = lambda a: jnp.flip(a, axis=1)
    log_f_f = jnp.log(lb[0] + (1.0 - lb[0]) * jax.nn.sigmoid(f_fwd))
    log_f_b = rev(jnp.log(lb[1] + (1.0 - lb[1]) * jax.nn.sigmoid(f_bwd)))
    o_f, s_f = gla_chunk_scan(q, -jnp.expm1(log_f_f), i, log_f_f, s0_f)
    o_b, s_b = gla_chunk_scan(rev(q), -jnp.expm1(log_f_b), rev(i), log_f_b, s0_b)
    return o_f + rev(o_b), s_f, s_b


def head_rms_norm(o, g):
    bsz, n = o.shape[:2]
    o = o * lax.rsqrt(jnp.mean(o * o, axis=-1, keepdims=True) + EPS) * g.astype(jnp.float32).reshape(H_B, DK_B)
    return o.reshape(bsz, n, D_B)


def _linear_combine(left, right):
    a_l, b_l = left
    a_r, b_r = right
    return a_r * a_l, a_r * b_l + b_r


def s5_direction(u, lam_re, lam_im, log_step, b_re, b_im, c_re, c_im, x0):
    f32 = jnp.float32
    n = u.shape[1]
    lam = lax.complex(lam_re.astype(f32), lam_im.astype(f32))
    lam_dt = lam * jnp.exp(log_step.astype(f32))[:, None]
    lam_bar = jnp.exp(lam_dt)
    b_bar = ((lam_bar - 1.0) / lam)[:, :, None] * lax.complex(b_re.astype(f32), b_im.astype(f32))
    bu = jnp.einsum('bngd,gpd->bngp', u.astype(jnp.complex64), b_bar)
    a = jnp.broadcast_to(lam_bar, (1, n) + lam_bar.shape)
    _, xs = lax.associative_scan(_linear_combine, (a, bu), axis=1)
    steps = jnp.arange(1, n + 1, dtype=f32)[:, None, None]
    xs = xs + jnp.exp(lam_dt[None] * steps)[None] * x0[:, None]
    y = jnp.einsum('bngp,gdp->bngd', xs, lax.complex(c_re.astype(f32), c_im.astype(f32))).real
    return y, xs[:, -1]


def s5_bidir(u, lam_re, lam_im, log_step, b_re, b_im, c_re, c_im, x0_f, x0_b):
    y_f, x_f = s5_direction(u, lam_re[0], lam_im[0], log_step[0], b_re[0], b_im[0], c_re[0], c_im[0], x0_f)
    y_b, x_b = s5_direction(jnp.flip(u, axis=1), lam_re[1], lam_im[1], log_step[1], b_re[1], b_im[1], c_re[1], c_im[1], x0_b)
    return y_f + jnp.flip(y_b, axis=1), x_f, x_b


def s5_glu(y, u, d_skip, w_glu):
    bsz, n = y.shape[:2]
    y = y.reshape(bsz, n, D_C) + d_skip.astype(jnp.float32) * u.reshape(bsz, n, D_C)
    z = jax.nn.gelu(y, approximate=False) @ w_glu.astype(jnp.float32)
    za, zg = jnp.split(z, 2, axis=-1)
    return za * jax.nn.sigmoid(zg)


def token_mix(h, wi, ln_g, ln_b, w_s, b_s, lb, hn_g, lam_re, lam_im, log_step, b_re, b_im, c_re, c_im,
              d_skip, w_glu, wa, wb, wc, wo, hgrn_s0, s5_x0, need_out):
    f32 = jnp.float32
    bsz, n, _ = h.shape
    a_u, a_v, b_q, b_ff, b_fb, b_i, b_g, c_u, gates = split_in_proj(h @ wi)
    heads = lambda t: t.astype(f32).reshape(bsz, n, H_B, DK_B)
    o_b, hs_f, hs_b = hgrn_bidir(jax.nn.silu(heads(b_q)), heads(b_ff), heads(b_fb), heads(b_i), lb, hgrn_s0[0], hgrn_s0[1])
    cu = c_u.astype(f32).reshape(bsz, n, G_C, S5_IN)
    y_s5, xs_f, xs_b = s5_bidir(cu, lam_re, lam_im, log_step, b_re, b_im, c_re, c_im, s5_x0[0], s5_x0[1])
    if not need_out:
        return None, (hs_f, hs_b), (xs_f, xs_b)
    ya = gmlp_branch(a_u, a_v, ln_g, ln_b, w_s, b_s)
    yb = (head_rms_norm(o_b, hn_g) * jax.nn.silu(b_g.astype(f32))).astype(h.dtype)
    yc = s5_glu(y_s5, cu, d_skip, w_glu).astype(h.dtype)
    ga, gb, gc = jnp.split(gates, 3, axis=-1)
    merged = jax.nn.sigmoid(ga) * (ya @ wa) + jax.nn.sigmoid(gb) * (yb @ wb) + jax.nn.sigmoid(gc) * (yc @ wc)
    return merged @ wo, (hs_f, hs_b), (xs_f, xs_b)


def peer(h, w_q, sub_keys, u_tab, v_tab):
    f32 = jnp.float32
    bsz, n, d = h.shape
    t = h.reshape(bsz * n, d)
    q = (t @ w_q).astype(f32).reshape(-1, PEER_HEADS, 2, PEER_DQ // 2)
    s = jnp.einsum('thpe,hpke->thpk', q, sub_keys.astype(f32))
    sv, si = lax.top_k(s, PEER_TOPK)
    cand = (sv[:, :, 0, :, None] + sv[:, :, 1, None, :]).reshape(-1, PEER_HEADS, PEER_TOPK * PEER_TOPK)
    fv, fi = lax.top_k(cand, PEER_TOPK)
    i1 = jnp.take_along_axis(si[:, :, 0], fi // PEER_TOPK, axis=-1)
    i2 = jnp.take_along_axis(si[:, :, 1], fi % PEER_TOPK, axis=-1)
    idx = (i1 * PEER_NKEYS + i2).reshape(-1, PEER_HEADS * PEER_TOPK)
    g = jax.nn.softmax(fv, axis=-1).reshape(-1, PEER_HEADS * PEER_TOPK)
    nb = t.shape[0] // PEER_BLOCK
    def expert_block(args):
        tb, ib, gb = args
        act = jnp.einsum('td,tkd->tk', tb, u_tab[ib]).astype(f32)
        w = (jax.nn.gelu(act, approximate=False) * gb).astype(v_tab.dtype)
        return jnp.einsum('tk,tkd->td', w, v_tab[ib])
    out = lax.map(expert_block, (t.reshape(nb, PEER_BLOCK, d), idx.reshape(nb, PEER_BLOCK, -1), g.reshape(nb, PEER_BLOCK, -1)))
    return out.reshape(bsz, n, d).astype(h.dtype)


def setup_inputs(seed: int = 0) -> dict:
    key = jax.random.key(seed)
    ks = jax.random.split(key, 32)
    f32 = jnp.float32
    nrm = lambda k, shape, s: jax.random.normal(k, shape, f32) * s
    x = nrm(ks[0], (BATCH, SEQ, D_MODEL), 1.0)
    c = nrm(ks[1], (BATCH, D_MODEL), 1.0)
    ctx = nrm(ks[2], (BATCH, CTX_LEN, D_MODEL), 1.0)
    c_ctx = nrm(ks[3], (D_MODEL,), 1.0)
    w_mod = nrm(ks[4], (DEPTH, D_MODEL, N_MOD * D_MODEL), 0.5 * D_MODEL ** -0.5)
    b_mod = nrm(ks[5], (DEPTH, N_MOD * D_MODEL), 0.01)
    w_in = nrm(ks[6], (DEPTH, D_MODEL, IN_COLS), D_MODEL ** -0.5)
    gmlp_ln_g = 1.0 + nrm(ks[7], (DEPTH, D_A), 0.02)
    gmlp_ln_b = nrm(ks[8], (DEPTH, D_A), 0.02)
    gmlp_w_s = nrm(ks[9], (DEPTH, G_A, GMLP_CHUNK, GMLP_CHUNK), 0.5 * GMLP_CHUNK ** -0.5)
    gmlp_b_s = 1.0 + nrm(ks[10], (DEPTH, G_A, GMLP_CHUNK), 0.01)
    hgrn_lb_logits = nrm(ks[11], (DEPTH, 2, D_B), 0.1)
    hgrn_norm_g = 1.0 + nrm(ks[12], (DEPTH, D_B), 0.02)
    n_idx = jnp.arange(S5_STATE, dtype=f32)
    s5_lam_re = -0.5 + nrm(ks[13], (DEPTH, 2, G_C, S5_STATE), 0.01)
    s5_lam_im = math.pi * n_idx + nrm(ks[14], (DEPTH, 2, G_C, S5_STATE), 0.01)
    s5_log_step = jax.random.uniform(ks[15], (DEPTH, 2, G_C), f32, math.log(1e-3), math.log(1e-1))
    s5_b_re = nrm(ks[16], (DEPTH, 2, G_C, S5_STATE, S5_IN), (2 * S5_IN) ** -0.5)
    s5_b_im = nrm(ks[17], (DEPTH, 2, G_C, S5_STATE, S5_IN), (2 * S5_IN) ** -0.5)
    s5_c_re = nrm(ks[18], (DEPTH, 2, G_C, S5_IN, S5_STATE), (2 * S5_STATE) ** -0.5)
    s5_c_im = nrm(ks[19], (DEPTH, 2, G_C, S5_IN, S5_STATE), (2 * S5_STATE) ** -0.5)
    s5_d = nrm(ks[20], (DEPTH, D_C), 1.0)
    s5_w_glu = nrm(ks[21], (DEPTH, D_C, 2 * D_C), D_C ** -0.5)
    w_branch_a = nrm(ks[22], (DEPTH, D_A, D_MODEL), D_A ** -0.5)
    w_branch_b = nrm(ks[23], (DEPTH, D_B, D_MODEL), D_B ** -0.5)
    w_branch_c = nrm(ks[24], (DEPTH, D_C, D_MODEL), D_C ** -0.5)
    w_out = nrm(ks[25], (DEPTH, D_MODEL, D_MODEL), D_MODEL ** -0.5)
    peer_w_q = nrm(ks[26], (DEPTH, D_MODEL, PEER_HEADS * PEER_DQ), D_MODEL ** -0.5)
    peer_sub_keys = nrm(ks[27], (DEPTH, PEER_HEADS, 2, PEER_NKEYS, PEER_DQ // 2), (PEER_DQ // 2) ** -0.5)
    peer_u = nrm(ks[28], (DEPTH, PEER_EXPERTS, D_MODEL), D_MODEL ** -0.5)
    peer_v = nrm(ks[29], (DEPTH, PEER_EXPERTS, D_MODEL), 1.0)
    final_norm_g = 1.0 + nrm(ks[30], (D_MODEL,), 0.02)
    return {"x": x, "c": c, "ctx": ctx, "c_ctx": c_ctx, "w_mod": w_mod, "b_mod": b_mod, "w_in": w_in,
            "gmlp_ln_g": gmlp_ln_g, "gmlp_ln_b": gmlp_ln_b, "gmlp_w_s": gmlp_w_s, "gmlp_b_s": gmlp_b_s,
            "hgrn_lb_logits": hgrn_lb_logits, "hgrn_norm_g": hgrn_norm_g,
            "s5_lam_re": s5_lam_re, "s5_lam_im": s5_lam_im, "s5_log_step": s5_log_step,
            "s5_b_re": s5_b_re, "s5_b_im": s5_b_im, "s5_c_re": s5_c_re, "s5_c_im": s5_c_im,
            "s5_d": s5_d, "s5_w_glu": s5_w_glu,
            "w_branch_a": w_branch_a, "w_branch_b": w_branch_b, "w_branch_c": w_branch_c, "w_out": w_out,
            "peer_w_q": peer_w_q, "peer_sub_keys": peer_sub_keys, "peer_u": peer_u, "peer_v": peer_v,
            "final_norm_g": final_norm_g}


def reference(x, c, ctx, c_ctx, w_mod, b_mod, w_in, gmlp_ln_g, gmlp_ln_b, gmlp_w_s, gmlp_b_s,
              hgrn_lb_logits, hgrn_norm_g, s5_lam_re, s5_lam_im, s5_log_step, s5_b_re, s5_b_im, s5_c_re, s5_c_im,
              s5_d, s5_w_glu, w_branch_a, w_branch_b, w_branch_c, w_out, peer_w_q, peer_sub_keys, peer_u, peer_v,
              final_norm_g):
    bsz, n, d = x.shape
    x = x + pos_embed_2d(n, d).astype(x.dtype)[None]
    xc = ctx
    gamma = jax.nn.softmax(hgrn_lb_logits.astype(jnp.float32), axis=0)
    lower_bounds = jnp.cumsum(gamma, axis=0) - gamma[0:1]
    zero_h = jnp.zeros((bsz, H_B, DK_B, DK_B), jnp.float32)
    zero_s = jnp.zeros((bsz, G_C, S5_STATE), jnp.complex64)
    for l in range(DEPTH):
        ctx_out = l < DEPTH - 1
        mod_x = jax.nn.silu(c) @ w_mod[l] + b_mod[l]
        mod_c = jax.nn.silu(c_ctx) @ w_mod[l] + b_mod[l]
        sh1, sc1, g1, sh2, sc2, g2 = jnp.split(mod_x[:, None, :], N_MOD, axis=-1)
        csh1, csc1, cg1, csh2, csc2, cg2 = jnp.split(mod_c, N_MOD, axis=-1)
        lb = lower_bounds[l].reshape(2, H_B, DK_B)
        lp = (w_in[l], gmlp_ln_g[l], gmlp_ln_b[l], gmlp_w_s[l], gmlp_b_s[l], lb, hgrn_norm_g[l],
              s5_lam_re[l], s5_lam_im[l], s5_log_step[l], s5_b_re[l], s5_b_im[l], s5_c_re[l], s5_c_im[l],
              s5_d[l], s5_w_glu[l], w_branch_a[l], w_branch_b[l], w_branch_c[l], w_out[l])
        y_ctx, hgrn_state, s5_state = token_mix(rms_norm(xc) * (1.0 + csc1) + csh1, *lp,
                                                (zero_h, zero_h), (zero_s, zero_s), ctx_out)
        y_lat, _, _ = token_mix(rms_norm(x) * (1.0 + sc1) + sh1, *lp, hgrn_state, s5_state, True)
        x = x + g1 * y_lat
        x = x + g2 * peer(rms_norm(x) * (1.0 + sc2) + sh2, peer_w_q[l], peer_sub_keys[l], peer_u[l], peer_v[l])
        if ctx_out:
            xc = xc + cg1 * y_ctx
            xc = xc + cg2 * peer(rms_norm(xc) * (1.0 + csc2) + csh2, peer_w_q[l], peer_sub_keys[l], peer_u[l], peer_v[l])
    return rms_norm(x) * final_norm_g.astype(x.dtype)
```

```python
import functools
import math

import jax
import jax.numpy as jnp
from jax import lax
from jax.experimental import pallas as pl
from jax.experimental.pallas import tpu as pltpu

F32 = jnp.float32
BF16 = jnp.bfloat16

EPS = 1e-6
GRID_W = 64
POS_THETA = 10000.0
N_MOD = 6
G_A = 8
GMLP_CHUNK = 128
H_B = 8
HGRN_CHUNK = 32
S5_IN = 16
S5_STATE = 64
S5_SLAB_G = 8
PEER_HEADS = 8
PEER_NKEYS = 128
PEER_TOPK = 16

TM = 256
LANE = 128
VMEM_LIMIT = 56 * 1024 * 1024


def _cp(sem, vmem=VMEM_LIMIT):
    return pltpu.CompilerParams(dimension_semantics=sem, vmem_limit_bytes=vmem)


def _const_spec(shape):
    nd = len(shape)
    return pl.BlockSpec(shape, lambda *_: (0,) * nd, pipeline_mode=pl.Buffered(1))


def _gelu(x):
    return x * (lax.erf(x * (1.0 / math.sqrt(2.0))) + 1.0) * 0.5


def _sigmoid(x):
    return jax.nn.sigmoid(x)


def _dot(a, b):
    return jnp.dot(a, b, preferred_element_type=F32)


def _dot_nt(a, b):
    return lax.dot_general(a, b, (((1,), (1,)), ((), ())), preferred_element_type=F32)


def _dot_tn(a, b):
    return lax.dot_general(a, b, (((0,), (0,)), ((), ())), preferred_element_type=F32)


def _embed_kernel(ctx_ref, x_ref, pos_ref, o_ref, *, nctx):
    j = pl.program_id(1)

    @pl.when(j < nctx)
    def _():
        o_ref[...] = ctx_ref[...]

    @pl.when(j >= nctx)
    def _():
        o_ref[...] = x_ref[...] + pos_ref[...]


def _embed(x, ctx, pos):
    bsz, n, d = x.shape
    nctx = ctx.shape[1] // TM
    nblk = nctx + n // TM
    return pl.pallas_call(
        functools.partial(_embed_kernel, nctx=nctx),
        out_shape=jax.ShapeDtypeStruct((bsz, nblk * TM, d), x.dtype),
        grid=(bsz, nblk),
        in_specs=[
            pl.BlockSpec((None, TM, d), lambda b, j: (b, jnp.minimum(j, nctx - 1), 0)),
            pl.BlockSpec((None, TM, d), lambda b, j: (b, jnp.maximum(j - nctx, 0), 0)),
            pl.BlockSpec((TM, d), lambda b, j: (jnp.maximum(j - nctx, 0), 0)),
        ],
        out_specs=pl.BlockSpec((None, TM, d), lambda b, j: (b, j, 0)),
        compiler_params=_cp(("parallel", "arbitrary")),
        name="embed",
    )(ctx, x, pos)


def _mod_kernel(c_ref, w_ref, b_ref, o_ref):
    c = c_ref[...]
    s = c * _sigmoid(c)
    o_ref[...] = _dot(s.astype(BF16), w_ref[...].astype(BF16)) + b_ref[...]


def _modulation(cc, w_mod, b_mod):
    depth, d, nd = w_mod.shape
    rows = cc.shape[0]
    tn = 1024
    return pl.pallas_call(
        _mod_kernel,
        out_shape=jax.ShapeDtypeStruct((depth, rows, nd), F32),
        grid=(depth, nd // tn),
        in_specs=[
            pl.BlockSpec((rows, d), lambda l, n: (0, 0)),
            pl.BlockSpec((None, d, tn), lambda l, n: (l, 0, n)),
            pl.BlockSpec((None, 1, tn), lambda l, n: (l, 0, n)),
        ],
        out_specs=pl.BlockSpec((None, rows, tn), lambda l, n: (l, 0, n)),
        compiler_params=_cp(("parallel", "arbitrary")),
        name="modulation",
    )(cc, w_mod, b_mod.reshape(depth, 1, nd))


def _inproj_kernel(x_ref, mod_ref, w_ref, o_ref, h_scr, *, tm, ctx_rows):
    j = pl.program_id(1)

    @pl.when(pl.program_id(2) == 0)
    def _():
        x = x_ref[...]
        r = lax.rsqrt(jnp.mean(x * x, axis=-1, keepdims=True) + EPS)
        row = j * tm + lax.broadcasted_iota(jnp.int32, (tm, 1), 0)
        is_ctx = row < ctx_rows
        sh = jnp.where(is_ctx, mod_ref[0, 0:1, :], mod_ref[1, 0:1, :])
        sc = jnp.where(is_ctx, mod_ref[0, 1:2, :], mod_ref[1, 1:2, :])
        h_scr[...] = (x * r * (1.0 + sc) + sh).astype(BF16)

    o_ref[...] = _dot(h_scr[...], w_ref[...])


def _in_proj(xall, mod, w_pad, ctx_rows):
    bsz, l, d = xall.shape
    ncol = w_pad.shape[1]
    tm = 768 if l % 768 == 0 else TM
    tn = 1024
    return pl.pallas_call(
        functools.partial(_inproj_kernel, tm=tm, ctx_rows=ctx_rows),
        out_shape=jax.ShapeDtypeStruct((bsz, l, ncol), F32),
        grid=(bsz, l // tm, ncol // tn),
        in_specs=[
            pl.BlockSpec((None, tm, d), lambda b, j, n: (b, j, 0)),
            pl.BlockSpec((None, 2, N_MOD, d), lambda b, j, n: (b, 0, 0, 0)),
            pl.BlockSpec((d, tn), lambda b, j, n: (0, n)),
        ],
        out_specs=pl.BlockSpec((None, tm, tn), lambda b, j, n: (b, j, n)),
        scratch_shapes=[pltpu.VMEM((tm, d), BF16)],
        compiler_params=_cp(("parallel", "parallel", "arbitrary")),
        name="in_proj",
    )(xall, mod, w_pad)


def _gmlp_kernel(u_ref, v_ref, lng_ref, lnb_ref, ws_ref, bias_ref, o_ref):
    v = _gelu(v_ref[...])
    mu = jnp.mean(v, axis=-1, keepdims=True)
    vc = v - mu
    var = jnp.mean(vc * vc, axis=-1, keepdims=True)
    v = (vc * lax.rsqrt(var + EPS) * lng_ref[...] + lnb_ref[...]).astype(BF16)
    for c in range(TM // GMLP_CHUNK):
        rows = slice(c * GMLP_CHUNK, (c + 1) * GMLP_CHUNK)
        for g in range(G_A):
            cols = slice(g * LANE, (g + 1) * LANE)
            mixed = _dot(ws_ref[g], v[rows, cols]) + bias_ref[:, cols]
            o_ref[rows, cols] = _gelu(u_ref[rows, cols]) * mixed


def _gmlp(p, ln_g, ln_b, w_s, bias):
    bsz, l, _ = p.shape
    da = ln_g.shape[-1]
    return pl.pallas_call(
        _gmlp_kernel,
        out_shape=jax.ShapeDtypeStruct((bsz, l, da), F32),
        grid=(bsz, l // TM),
        in_specs=[
            pl.BlockSpec((None, TM, da), lambda b, j: (b, j, 0)),
            pl.BlockSpec((None, TM, da), lambda b, j: (b, j, 1)),
            _const_spec((1, da)),
            _const_spec((1, da)),
            _const_spec((G_A, GMLP_CHUNK, GMLP_CHUNK)),
            _const_spec((GMLP_CHUNK, da)),
        ],
        out_specs=pl.BlockSpec((None, TM, da), lambda b, j: (b, j, 0)),
        compiler_params=_cp(("parallel", "parallel")),
        name="gmlp",
    )(p, p, ln_g, ln_b, w_s, bias)


def _scan_block(p, rev, nctx, nblk):
    if not rev:
        return p
    return jnp.where(p < nctx, nctx - 1 - p, nblk - 1 - (p - nctx))


def _hgrn_kernel(q_ref, f_ref, i_ref, lb_ref, o_ref, st_ref, *, rev):
    c_sz = HGRN_CHUNK
    nchunk = TM // c_sz

    @pl.when(pl.program_id(1) == 0)
    def _():
        st_ref[...] = jnp.zeros_like(st_ref)

    rowi = lax.broadcasted_iota(jnp.int32, (c_sz, LANE), 0)

    def head_body(hh, carry):
        col0 = pl.multiple_of(hh * LANE, LANE)
        cols = pl.ds(col0, LANE)
        lbh = lb_ref[:, cols]

        def chunk_body(ci, st):
            c = (nchunk - 1 - ci) if rev else ci
            rows = pl.ds(pl.multiple_of(c * c_sz, c_sz), c_sz)
            qr = q_ref[rows, cols]
            q = qr * _sigmoid(qr)
            fg = lbh + (1.0 - lbh) * _sigmoid(f_ref[rows, cols])
            k = 1.0 - fg
            v = i_ref[rows, cols]
            b = jnp.log(fg)
            for s in (1, 2, 4, 8, 16):
                if rev:
                    b = b + jnp.where(rowi < c_sz - s, pltpu.roll(b, c_sz - s, 0), 0.0)
                else:
                    b = b + jnp.where(rowi >= s, pltpu.roll(b, s, 0), 0.0)
            b_last = b[0:1, :] if rev else b[c_sz - 1:c_sz, :]
            o = _dot_nt((q * jnp.exp(b)).astype(BF16), st.astype(BF16))
            for s in range(c_sz):
                valid = (rowi <= s) if rev else (rowi >= s)
                dec = jnp.exp(jnp.where(valid, b - b[s:s + 1, :], 0.0))
                pr = jnp.where(valid, q * (k[s:s + 1, :] * dec), 0.0)
                o = o + jnp.sum(pr, axis=-1, keepdims=True) * v[s:s + 1, :]
            o_ref[rows, cols] = o
            kk = k * jnp.exp(b_last - b)
            return st * jnp.exp(b_last) + _dot_tn(v.astype(BF16), kk.astype(BF16))

        st_ref[hh] = lax.fori_loop(0, nchunk, chunk_body, st_ref[hh])
        return carry

    lax.fori_loop(0, H_B, head_body, 0)


def _hgrn(p, lb, rev, nctx):
    bsz, l, _ = p.shape
    db = lb.shape[-1]
    nblk = l // TM
    blk = functools.partial(_scan_block, rev=rev, nctx=nctx, nblk=nblk)
    fcol = 4 if rev else 3
    return pl.pallas_call(
        functools.partial(_hgrn_kernel, rev=rev),
        out_shape=jax.ShapeDtypeStruct((bsz, l, db), F32),
        grid=(bsz, nblk),
        in_specs=[
            pl.BlockSpec((None, TM, db), lambda b, p_: (b, blk(p_), 2)),
            pl.BlockSpec((None, TM, db), lambda b, p_: (b, blk(p_), fcol)),
            pl.BlockSpec((None, TM, db), lambda b, p_: (b, blk(p_), 5)),
            _const_spec((1, db)),
        ],
        out_specs=pl.BlockSpec((None, TM, db), lambda b, p_: (b, blk(p_), 0)),
        scratch_shapes=[pltpu.VMEM((H_B, LANE, LANE), F32)],
        compiler_params=_cp(("parallel", "arbitrary")),
        name="hgrn_bwd" if rev else "hgrn_fwd",
    )(p, p, p, lb)


def _s5_kernel(u_ref, bb_ref, lam_ref, cc_ref, o_ref, bu_scr, st_ref, *, rev, tb, bsz, nslab):
    half = S5_SLAB_G * S5_STATE

    @pl.when(pl.program_id(0) == 0)
    def _():
        st_ref[...] = jnp.zeros_like(st_ref)

    for j in range(nslab):
        uj = u_ref[:, :, j * LANE:(j + 1) * LANE].reshape(tb * bsz, LANE).astype(BF16)
        bu_scr[...] = _dot(uj, bb_ref[j])
        lre = jnp.broadcast_to(lam_ref[j, 0:1, :], (bsz, half))
        lim = jnp.broadcast_to(lam_ref[j, 1:2, :], (bsz, half))

        def step(i, carry):
            xre, xim = carry
            t = (tb - 1 - i) if rev else i
            rows = pl.ds(pl.multiple_of(t * bsz, bsz), bsz)
            nre = lre * xre - lim * xim + bu_scr[rows, 0:half]
            nim = lre * xim + lim * xre + bu_scr[rows, half:2 * half]
            bu_scr[rows, 0:half] = nre
            bu_scr[rows, half:2 * half] = nim
            return nre, nim

        xre, xim = lax.fori_loop(0, tb, step, (st_ref[j, :, 0:half], st_ref[j, :, half:2 * half]), unroll=4)
        st_ref[j, :, 0:half] = xre
        st_ref[j, :, half:2 * half] = xim
        y = _dot(bu_scr[...].astype(BF16), cc_ref[j])
        o_ref[:, :, j * LANE:(j + 1) * LANE] = y.reshape(tb, bsz, LANE)


def _s5(u_tm, bbig, lam, cbig, rev, ctx_len):
    l, bsz, dc = u_tm.shape
    nslab = dc // LANE
    tb = 128
    nblk, nctx = l // tb, ctx_len // tb
    blk = functools.partial(_scan_block, rev=rev, nctx=nctx, nblk=nblk)
    half = S5_SLAB_G * S5_STATE
    return pl.pallas_call(
        functools.partial(_s5_kernel, rev=rev, tb=tb, bsz=bsz, nslab=nslab),
        out_shape=jax.ShapeDtypeStruct((l, bsz, dc), F32),
        grid=(nblk,),
        in_specs=[
            pl.BlockSpec((tb, bsz, dc), lambda p_: (blk(p_), 0, 0)),
            _const_spec((nslab, LANE, 2 * half)),
            _const_spec((nslab, 2, half)),
            _const_spec((nslab, 2 * half, LANE)),
        ],
        out_specs=pl.BlockSpec((tb, bsz, dc), lambda p_: (blk(p_), 0, 0)),
        scratch_shapes=[pltpu.VMEM((tb * bsz, 2 * half), F32), pltpu.VMEM((nslab, bsz, 2 * half), F32)],
        compiler_params=_cp(("arbitrary",)),
        name="s5_bwd" if rev else "s5_fwd",
    )(u_tm, bbig, lam, cbig)


def _s5_params(lam_re, lam_im, log_step, b_re, b_im, c_re, c_im):
    g_c, p_s = lam_re.shape
    nslab = g_c // S5_SLAB_G
    lam = lax.complex(lam_re.astype(F32), lam_im.astype(F32))
    lam_dt = lam * jnp.exp(log_step.astype(F32))[:, None]
    lam_bar = jnp.exp(lam_dt)
    b_bar = ((lam_bar - 1.0) / lam)[:, :, None] * lax.complex(b_re.astype(F32), b_im.astype(F32))
    eye = jnp.eye(S5_SLAB_G, dtype=F32)

    def in_map(m):
        m = m.reshape(nslab, S5_SLAB_G, p_s, S5_IN)
        return jnp.einsum('jgpd,gh->jgdhp', m, eye).reshape(nslab, S5_SLAB_G * S5_IN, S5_SLAB_G * p_s)

    def out_map(m):
        m = m.reshape(nslab, S5_SLAB_G, S5_IN, p_s)
        return jnp.einsum('jgdp,gh->jgphd', m, eye).reshape(nslab, S5_SLAB_G * p_s, S5_SLAB_G * S5_IN)

    bbig = jnp.concatenate([in_map(jnp.real(b_bar)), in_map(jnp.imag(b_bar))], axis=-1).astype(BF16)
    cbig = jnp.concatenate([out_map(c_re.astype(F32)), out_map(-c_im.astype(F32))], axis=1).astype(BF16)
    lam_s = jnp.stack([jnp.real(lam_bar).reshape(nslab, -1), jnp.imag(lam_bar).reshape(nslab, -1)], axis=1)
    return bbig, lam_s, cbig


def _merge_a_kernel(ya_ref, of_ref, ob_ref, bg_ref, yf_ref, yb_ref, cu_ref, ga_ref, gb_ref, gc_ref,
                    hng_ref, dsk_ref, wglu_ref, wa_ref, wb_ref, wc_ref, o_ref):
    dc = dsk_ref.shape[-1]
    o = of_ref[...] + ob_ref[...]
    parts = []
    for hh in range(H_B):
        oh = o[:, hh * LANE:(hh + 1) * LANE]
        parts.append(oh * lax.rsqrt(jnp.mean(oh * oh, axis=-1, keepdims=True) + EPS))
    bg = bg_ref[...]
    yb = jnp.concatenate(parts, axis=-1) * hng_ref[...] * (bg * _sigmoid(bg))
    y = yf_ref[...] + yb_ref[...] + dsk_ref[...] * cu_ref[:, 0:dc]
    z = _dot(_gelu(y).astype(BF16), wglu_ref[...])
    yc = z[:, 0:dc] * _sigmoid(z[:, dc:2 * dc])
    merged = (_sigmoid(ga_ref[...]) * _dot(ya_ref[...].astype(BF16), wa_ref[...])
              + _sigmoid(gb_ref[...]) * _dot(yb.astype(BF16), wb_ref[...])
              + _sigmoid(gc_ref[...]) * _dot(yc.astype(BF16), wc_ref[...]))
    o_ref[...] = merged.astype(BF16)


def _merge_a(ya, o_f, o_b, p, y_f, y_b, hn_g, d_skip, w_glu, wa, wb, wc, off):
    bsz, l, da = ya.shape
    dc = y_f.shape[-1]
    d = wa.shape[-1]
    tm = 128
    o2 = off * (TM // tm)
    nblk = l // tm - o2
    tok = lambda b, j: (b, j + o2, 0)
    pcol = lambda c: (lambda b, j: (b, j + o2, c))
    return pl.pallas_call(
        _merge_a_kernel,
        out_shape=jax.ShapeDtypeStruct((bsz, nblk * tm, d), BF16),
        grid=(bsz, nblk),
        in_specs=[
            pl.BlockSpec((None, tm, da), tok),
            pl.BlockSpec((None, tm, da), tok),
            pl.BlockSpec((None, tm, da), tok),
            pl.BlockSpec((None, tm, da), pcol(6)),
            pl.BlockSpec((None, tm, dc), tok),
            pl.BlockSpec((None, tm, dc), tok),
            pl.BlockSpec((None, tm, da), pcol(7)),
            pl.BlockSpec((None, tm, d), pcol(4)),
            pl.BlockSpec((None, tm, d), pcol(5)),
            pl.BlockSpec((None, tm, d), pcol(6)),
            _const_spec((1, da)),
            _const_spec((1, dc)),
            _const_spec(w_glu.shape),
            _const_spec(wa.shape),
            _const_spec(wb.shape),
            _const_spec(wc.shape),
        ],
        out_specs=pl.BlockSpec((None, tm, d), lambda b, j: (b, j, 0)),
        compiler_params=_cp(("parallel", "parallel")),
        name="merge_a",
    )(ya, o_f, o_b, p, y_f, y_b, p, p, p, p, hn_g, d_skip, w_glu, wa, wb, wc)


def _merge_b_kernel(m_ref, x_ref, mod_ref, wo_ref, wq_ref, x1_ref, h2_ref, q_ref):
    x1 = x_ref[...] + mod_ref[2:3, :] * _dot(m_ref[...], wo_ref[...])
    x1_ref[...] = x1
    r = lax.rsqrt(jnp.mean(x1 * x1, axis=-1, keepdims=True) + EPS)
    h2 = x1 * r * (1.0 + mod_ref[4:5, :]) + mod_ref[3:4, :]
    h2_ref[...] = h2
    q_ref[...] = _dot(h2.astype(BF16), wq_ref[...])


def _merge_b(merged, xall, mod, wo, wq, off, nctx):
    bsz, lo, d = merged.shape
    nblk = lo // TM
    dq = wq.shape[-1]
    out = lambda b, j: (b, j, 0)
    return pl.pallas_call(
        _merge_b_kernel,
        out_shape=(jax.ShapeDtypeStruct((bsz, lo, d), F32), jax.ShapeDtypeStruct((bsz, lo, d), F32),
                   jax.ShapeDtypeStruct((bsz, lo, dq), F32)),
        grid=(bsz, nblk),
        in_specs=[
            pl.BlockSpec((None, TM, d), out),
            pl.BlockSpec((None, TM, d), lambda b, j: (b, j + off, 0)),
            pl.BlockSpec((None, None, N_MOD, d), lambda b, j: (b, (j + off >= nctx).astype(jnp.int32), 0, 0)),
            _const_spec(wo.shape),
            _const_spec(wq.shape),
        ],
        out_specs=(pl.BlockSpec((None, TM, d), out), pl.BlockSpec((None, TM, d), out),
                   pl.BlockSpec((None, TM, dq), out)),
        compiler_params=_cp(("parallel", "parallel")),
        name="merge_b",
    )(merged, xall, mod, wo, wq)


def _topk_rows(s, k, n):
    it = lax.broadcasted_iota(jnp.int32, s.shape, 0).astype(F32)
    slot = lax.broadcasted_iota(jnp.int32, (k, s.shape[1]), 0)
    vals = jnp.zeros((k, s.shape[1]), F32)
    ids = jnp.zeros((k, s.shape[1]), F32)
    for r in range(k):
        m = jnp.max(s, axis=0, keepdims=True)
        am = jnp.min(jnp.where(s == m, it, float(n)), axis=0, keepdims=True)
        vals = jnp.where(slot == r, m, vals)
        ids = jnp.where(slot == r, am, ids)
        s = jnp.where(it == am, -jnp.inf, s)
    return vals, ids


def _route_kernel(q_ref, keys_ref, idx_ref, g_ref):
    k = PEER_TOPK
    for h in range(PEER_HEADS):
        sv, si = [], []
        for half in range(2):
            c0 = (h * 2 + half) * PEER_NKEYS
            qh = q_ref[:, c0:c0 + PEER_NKEYS].astype(BF16)
            s = _dot_nt(keys_ref[h, half], qh)
            v_, i_ = _topk_rows(s, k, PEER_NKEYS)
            sv.append(v_)
            si.append(i_)
        cand = jnp.concatenate([sv[0][a:a + 1, :] + sv[1] for a in range(k)], axis=0)
        fv, fi = _topk_rows(cand, k, k * k)
        fa = jnp.floor(fi * (1.0 / k))
        fb = fi - fa * k
        i1 = jnp.zeros_like(fi)
        i2 = jnp.zeros_like(fi)
        for a in range(k):
            i1 = jnp.where(fa == a, si[0][a:a + 1, :], i1)
            i2 = jnp.where(fb == a, si[1][a:a + 1, :], i2)
        e = jnp.exp(fv - fv[0:1, :])
        g_ref[h * k:(h + 1) * k, :] = e / jnp.sum(e, axis=0, keepdims=True)
        idx_ref[h * k:(h + 1) * k, :] = (i1 * PEER_NKEYS + i2).astype(jnp.int32)


def _route(q, keys):
    t, dq = q.shape
    tm = LANE
    nsel = PEER_HEADS * PEER_TOPK
    return pl.pallas_call(
        _route_kernel,
        out_shape=(jax.ShapeDtypeStruct((nsel, t), jnp.int32), jax.ShapeDtypeStruct((nsel, t), F32)),
        grid=(t // tm,),
        in_specs=[pl.BlockSpec((tm, dq), lambda i: (i, 0)), _const_spec(keys.shape)],
        out_specs=(pl.BlockSpec((nsel, tm), lambda i: (0, i)), pl.BlockSpec((nsel, tm), lambda i: (0, i))),
        compiler_params=_cp(("parallel",)),
        name="peer_route",
    )(q, keys)


PEER_TB = 32


def _experts_kernel(idx_ref, g_ref, h_ref, x_ref, mod_ref, tab_ref, o_ref, buf, sem, *, d):
    nsel = PEER_HEADS * PEER_TOPK

    def row_copy(t, j, slot):
        return pltpu.make_async_copy(tab_ref.at[idx_ref[t, j]], buf.at[slot, j], sem.at[slot])

    def issue(t, slot):
        def body(j, c):
            row_copy(t, j, slot).start()
            return c
        lax.fori_loop(0, nsel, body, 0, unroll=8)

    def wait(slot):
        pltpu.make_async_copy(tab_ref.at[pl.ds(0, nsel)], buf.at[slot], sem.at[slot]).wait()

    issue(0, 0)
    eye = lax.broadcasted_iota(jnp.int32, (nsel, nsel), 0) == lax.broadcasted_iota(jnp.int32, (nsel, nsel), 1)
    g2 = mod_ref[5:6, :]

    def tok(t, c):
        slot = lax.rem(t, 2)

        @pl.when(t + 1 < PEER_TB)
        def _():
            issue(t + 1, 1 - slot)

        wait(slot)
        row = pl.ds(t, 1)
        act = jnp.sum(buf[slot, :, 0:d] * h_ref[row, :], axis=-1, keepdims=True)
        gcol = jnp.sum(jnp.where(eye, g_ref[row, :], 0.0), axis=-1, keepdims=True)
        w = _gelu(act) * gcol
        out = jnp.sum(w * buf[slot, :, d:2 * d], axis=0, keepdims=True)
        o_ref[row, :] = x_ref[row, :] + g2 * out
        return c

    lax.fori_loop(0, PEER_TB, tok, 0)


def _experts(idx, g, h2, x1, mod, tab, off, nctx):
    bsz, lo, d = h2.shape
    nsel = idx.shape[-1]
    tb = PEER_TB
    per = TM // tb
    tokb = lambda b, j: (b, j, 0)
    return pl.pallas_call(
        functools.partial(_experts_kernel, d=d),
        out_shape=jax.ShapeDtypeStruct((bsz, lo, d), F32),
        grid=(bsz, lo // tb),
        in_specs=[
            pl.BlockSpec((None, tb, nsel), tokb, memory_space=pltpu.SMEM),
            pl.BlockSpec((None, tb, nsel), tokb),
            pl.BlockSpec((None, tb, d), tokb),
            pl.BlockSpec((None, tb, d), tokb),
            pl.BlockSpec((None, None, N_MOD, d), lambda b, j: (b, (j // per + off >= nctx).astype(jnp.int32), 0, 0)),
            pl.BlockSpec(memory_space=pl.ANY),
        ],
        out_specs=pl.BlockSpec((None, tb, d), tokb),
        scratch_shapes=[pltpu.VMEM((2, nsel, 2 * d), F32), pltpu.SemaphoreType.DMA((2,))],
        compiler_params=_cp(("parallel", "arbitrary")),
        name="peer_experts",
    )(idx, g, h2, x1, mod, tab)


def _final_kernel(x_ref, g_ref, o_ref):
    x = x_ref[...]
    o_ref[...] = x * lax.rsqrt(jnp.mean(x * x, axis=-1, keepdims=True) + EPS) * g_ref[...]


def _final_norm(x, g):
    bsz, n, d = x.shape
    return pl.pallas_call(
        _final_kernel,
        out_shape=jax.ShapeDtypeStruct((bsz, n, d), F32),
        grid=(bsz, n // TM),
        in_specs=[pl.BlockSpec((None, TM, d), lambda b, j: (b, j, 0)), _const_spec((1, d))],
        out_specs=pl.BlockSpec((None, TM, d), lambda b, j: (b, j, 0)),
        compiler_params=_cp(("parallel", "parallel")),
        name="final_norm",
    )(x, g.reshape(1, d))


def _pos_embed_2d(n, d):
    n_rows = n // GRID_W
    quarter = d // 4
    omega = 1.0 / (POS_THETA ** (jnp.arange(quarter, dtype=F32) / quarter))

    def axis_code(length):
        ang = jnp.arange(length, dtype=F32)[:, None] * omega[None]
        return jnp.concatenate([jnp.sin(ang), jnp.cos(ang)], axis=-1)

    er = jnp.broadcast_to(axis_code(n_rows)[:, None], (n_rows, GRID_W, d // 2))
    ec = jnp.broadcast_to(axis_code(GRID_W)[None], (n_rows, GRID_W, d // 2))
    return jnp.concatenate([er, ec], axis=-1).reshape(n, d)


def kernel(x, c, ctx, c_ctx, w_mod, b_mod, w_in, gmlp_ln_g, gmlp_ln_b, gmlp_w_s, gmlp_b_s, hgrn_lb_logits, hgrn_norm_g, s5_lam_re, s5_lam_im, s5_log_step, s5_b_re, s5_b_im, s5_c_re, s5_c_im, s5_d, s5_w_glu, w_branch_a, w_branch_b, w_branch_c, w_out, peer_w_q, peer_sub_keys, peer_u, peer_v, final_norm_g):
    bsz, n, d = x.shape
    ctx_len = ctx.shape[1]
    depth = w_mod.shape[0]
    nctx = ctx_len // TM
    da = gmlp_ln_g.shape[-1]
    db = hgrn_norm_g.shape[-1]
    dc = s5_d.shape[-1]
    assert n % TM == 0 and ctx_len % TM == 0 and bsz % 8 == 0

    xall = _embed(x, ctx, _pos_embed_2d(n, d).astype(x.dtype))

    rows = -(-(bsz + 1) // 8) * 8
    cc = jnp.zeros((rows, d), F32).at[0].set(c_ctx).at[1:bsz + 1].set(c)
    mod_all = _modulation(cc, w_mod, b_mod)

    gamma = jax.nn.softmax(hgrn_lb_logits.astype(F32), axis=0)
    lower_bounds = jnp.cumsum(gamma, axis=0) - gamma[0:1]

    for l in range(depth):
        last = l == depth - 1
        off = nctx if last else 0
        mod_c = jnp.broadcast_to(mod_all[l, 0][None], (bsz, N_MOD * d))
        mod = jnp.stack([mod_c, mod_all[l, 1:bsz + 1]], axis=1).reshape(bsz, 2, N_MOD, d)

        cu0 = 2 * da + 5 * db
        w_pad = jnp.concatenate([w_in[l][:, :cu0 + dc], jnp.zeros((d, da - dc), F32), w_in[l][:, cu0 + dc:]],
                                axis=1).astype(BF16)
        p = _in_proj(xall, mod, w_pad, ctx_len)

        bias = jnp.repeat(gmlp_b_s[l].T, da // G_A, axis=1)
        ya = _gmlp(p, gmlp_ln_g[l].reshape(1, da), gmlp_ln_b[l].reshape(1, da), gmlp_w_s[l].astype(BF16), bias)

        o_f = _hgrn(p, lower_bounds[l, 0].reshape(1, db), False, nctx)
        o_b = _hgrn(p, lower_bounds[l, 1].reshape(1, db), True, nctx)

        u_tm = jnp.transpose(p[:, :, cu0:cu0 + dc], (1, 0, 2))
        ys = []
        for r in range(2):
            prm = _s5_params(s5_lam_re[l, r], s5_lam_im[l, r], s5_log_step[l, r], s5_b_re[l, r], s5_b_im[l, r],
                             s5_c_re[l, r], s5_c_im[l, r])
            ys.append(jnp.transpose(_s5(u_tm, *prm, rev=bool(r), ctx_len=ctx_len), (1, 0, 2)))

        merged = _merge_a(ya, o_f, o_b, p, ys[0], ys[1], hgrn_norm_g[l].reshape(1, db), s5_d[l].reshape(1, dc),
                          s5_w_glu[l].astype(BF16), w_branch_a[l].astype(BF16), w_branch_b[l].astype(BF16),
                          w_branch_c[l].astype(BF16), off)
        x1, h2, q = _merge_b(merged, xall, mod, w_out[l].astype(BF16), peer_w_q[l].astype(BF16), off, nctx)

        lo = x1.shape[1]
        idx_t, g_t = _route(q.reshape(bsz * lo, -1), peer_sub_keys[l].astype(BF16))
        idx = idx_t.T.reshape(bsz, lo, -1)
        g = g_t.T.reshape(bsz, lo, -1)
        tab = jnp.concatenate([peer_u[l], peer_v[l]], axis=1)
        xall = _experts(idx, g, h2, x1, mod, tab, off, nctx)

    return _final_norm(xall, final_norm_g.astype(F32))
```

```python
import functools
import math

import jax
import jax.numpy as jnp
from jax import lax
from jax.experimental import pallas as pl
from jax.experimental.pallas import tpu as pltpu

F32 = jnp.float32
BF16 = jnp.bfloat16

EPS = 1e-6
GRID_W = 64
POS_THETA = 10000.0
N_MOD = 6
G_A = 8
GMLP_CHUNK = 128
H_B = 8
HGRN_CHUNK = 32
S5_IN = 16
S5_STATE = 64
S5_SLAB_G = 8
PEER_HEADS = 8
PEER_NKEYS = 128
PEER_TOPK = 16

TM = 256
LANE = 128
SUB = 8
VMEM_LIMIT = 56 * 1024 * 1024


def _cp(sem, vmem=VMEM_LIMIT):
    return pltpu.CompilerParams(dimension_semantics=sem, vmem_limit_bytes=vmem)


def _const_spec(shape):
    nd = len(shape)
    return pl.BlockSpec(shape, lambda *_: (0,) * nd, pipeline_mode=pl.Buffered(1))


def _gelu(x):
    return x * (lax.erf(x * (1.0 / math.sqrt(2.0))) + 1.0) * 0.5


def _sigmoid(x):
    return jax.nn.sigmoid(x)


def _dot(a, b):
    return jnp.dot(a, b, preferred_element_type=F32)


def _dot_nt(a, b):
    return lax.dot_general(a, b, (((1,), (1,)), ((), ())), preferred_element_type=F32)


def _dot_tn(a, b):
    return lax.dot_general(a, b, (((0,), (0,)), ((), ())), preferred_element_type=F32)


def _embed_kernel(ctx_ref, x_ref, pos_ref, o_ref, *, nctx):
    j = pl.program_id(1)

    @pl.when(j < nctx)
    def _():
        o_ref[...] = ctx_ref[...]

    @pl.when(j >= nctx)
    def _():
        o_ref[...] = x_ref[...] + pos_ref[...]


def _embed(x, ctx, pos):
    bsz, n, d = x.shape
    nctx = ctx.shape[1] // TM
    nblk = nctx + n // TM
    return pl.pallas_call(
        functools.partial(_embed_kernel, nctx=nctx),
        out_shape=jax.ShapeDtypeStruct((bsz, nblk * TM, d), x.dtype),
        grid=(bsz, nblk),
        in_specs=[
            pl.BlockSpec((None, TM, d), lambda b, j: (b, jnp.minimum(j, nctx - 1), 0)),
            pl.BlockSpec((None, TM, d), lambda b, j: (b, jnp.maximum(j - nctx, 0), 0)),
            pl.BlockSpec((TM, d), lambda b, j: (jnp.maximum(j - nctx, 0), 0)),
        ],
        out_specs=pl.BlockSpec((None, TM, d), lambda b, j: (b, j, 0)),
        compiler_params=_cp(("parallel", "arbitrary")),
        name="embed",
    )(ctx, x, pos)


def _mod_kernel(c_ref, w_ref, b_ref, o_ref):
    c = c_ref[...]
    s = c * _sigmoid(c)
    o_ref[...] = _dot(s.astype(BF16), w_ref[...].astype(BF16)) + b_ref[...]


def _modulation(cc, w_mod, b_mod):
    depth, d, nd = w_mod.shape
    rows = cc.shape[0]
    tn = 1024
    return pl.pallas_call(
        _mod_kernel,
        out_shape=jax.ShapeDtypeStruct((depth, rows, nd), F32),
        grid=(depth, nd // tn),
        in_specs=[
            pl.BlockSpec((rows, d), lambda l, n: (0, 0)),
            pl.BlockSpec((None, d, tn), lambda l, n: (l, 0, n)),
            pl.BlockSpec((None, 1, tn), lambda l, n: (l, 0, n)),
        ],
        out_specs=pl.BlockSpec((None, rows, tn), lambda l, n: (l, 0, n)),
        compiler_params=_cp(("parallel", "arbitrary")),
        name="modulation",
    )(cc, w_mod, b_mod.reshape(depth, 1, nd))


def _inproj_kernel(x_ref, mod_ref, w_ref, o_ref, h_scr, *, tm, ctx_rows):
    j = pl.program_id(1)

    @pl.when(pl.program_id(2) == 0)
    def _():
        x = x_ref[...]
        r = lax.rsqrt(jnp.mean(x * x, axis=-1, keepdims=True) + EPS)
        row = j * tm + lax.broadcasted_iota(jnp.int32, (tm, 1), 0)
        is_ctx = row < ctx_rows
        sh = jnp.where(is_ctx, mod_ref[0, 0:1, :], mod_ref[1, 0:1, :])
        sc = jnp.where(is_ctx, mod_ref[0, 1:2, :], mod_ref[1, 1:2, :])
        h_scr[...] = (x * r * (1.0 + sc) + sh).astype(BF16)

    o_ref[...] = _dot(h_scr[...], w_ref[...])


def _in_proj(xall, mod, w_pad, ctx_rows):
    bsz, l, d = xall.shape
    ncol = w_pad.shape[1]
    tm = 768 if l % 768 == 0 else TM
    tn = 1024
    return pl.pallas_call(
        functools.partial(_inproj_kernel, tm=tm, ctx_rows=ctx_rows),
        out_shape=jax.ShapeDtypeStruct((bsz, l, ncol), F32),
        grid=(bsz, l // tm, ncol // tn),
        in_specs=[
            pl.BlockSpec((None, tm, d), lambda b, j, n: (b, j, 0)),
            pl.BlockSpec((None, 2, N_MOD, d), lambda b, j, n: (b, 0, 0, 0)),
            pl.BlockSpec((d, tn), lambda b, j, n: (0, n)),
        ],
        out_specs=pl.BlockSpec((None, tm, tn), lambda b, j, n: (b, j, n)),
        scratch_shapes=[pltpu.VMEM((tm, d), BF16)],
        compiler_params=_cp(("parallel", "parallel", "arbitrary")),
        name="in_proj",
    )(xall, mod, w_pad)


def _gmlp_kernel(u_ref, v_ref, lng_ref, lnb_ref, ws_ref, bias_ref, o_ref):
    v = _gelu(v_ref[...])
    mu = jnp.mean(v, axis=-1, keepdims=True)
    vc = v - mu
    var = jnp.mean(vc * vc, axis=-1, keepdims=True)
    v = (vc * lax.rsqrt(var + EPS) * lng_ref[...] + lnb_ref[...]).astype(BF16)
    for c in range(TM // GMLP_CHUNK):
        rows = slice(c * GMLP_CHUNK, (c + 1) * GMLP_CHUNK)
        for g in range(G_A):
            cols = slice(g * LANE, (g + 1) * LANE)
            mixed = _dot(ws_ref[g], v[rows, cols]) + bias_ref[:, cols]
            o_ref[rows, cols] = _gelu(u_ref[rows, cols]) * mixed


def _gmlp(p, ln_g, ln_b, w_s, bias):
    bsz, l, _ = p.shape
    da = ln_g.shape[-1]
    return pl.pallas_call(
        _gmlp_kernel,
        out_shape=jax.ShapeDtypeStruct((bsz, l, da), F32),
        grid=(bsz, l // TM),
        in_specs=[
            pl.BlockSpec((None, TM, da), lambda b, j: (b, j, 0)),
            pl.BlockSpec((None, TM, da), lambda b, j: (b, j, 1)),
            _const_spec((1, da)),
            _const_spec((1, da)),
            _const_spec((G_A, GMLP_CHUNK, GMLP_CHUNK)),
            _const_spec((GMLP_CHUNK, da)),
        ],
        out_specs=pl.BlockSpec((None, TM, da), lambda b, j: (b, j, 0)),
        compiler_params=_cp(("parallel", "parallel")),
        name="gmlp",
    )(p, p, ln_g, ln_b, w_s, bias)


def _scan_block(p, rev, nctx, nblk):
    if not rev:
        return p
    return jnp.where(p < nctx, nctx - 1 - p, nblk - 1 - (p - nctx))


def _hgrn_kernel(q_ref, f_ref, i_ref, lb_ref, o_ref, st_ref, *, rev):
    c_sz = HGRN_CHUNK
    nchunk = TM // c_sz

    @pl.when(pl.program_id(1) == 0)
    def _():
        st_ref[...] = jnp.zeros_like(st_ref)

    nsb = c_sz // SUB
    rowi = lax.broadcasted_iota(jnp.int32, (c_sz, LANE), 0)
    row8 = lax.broadcasted_iota(jnp.int32, (SUB, LANE), 0)
    rowa = lax.broadcasted_iota(jnp.int32, (c_sz, c_sz), 0)

    def head_body(hh, carry):
        col0 = pl.multiple_of(hh * LANE, LANE)
        cols = pl.ds(col0, LANE)
        lbh = lb_ref[:, cols]

        def chunk_body(ci, st):
            c = (nchunk - 1 - ci) if rev else ci
            rows = pl.ds(pl.multiple_of(c * c_sz, c_sz), c_sz)
            qr = q_ref[rows, cols]
            q = qr * _sigmoid(qr)
            fg = lbh + (1.0 - lbh) * _sigmoid(f_ref[rows, cols])
            k = 1.0 - fg
            v = i_ref[rows, cols]
            b = jnp.log(fg)
            for s in (1, 2, 4, 8, 16):
                if rev:
                    b = b + jnp.where(rowi < c_sz - s, pltpu.roll(b, c_sz - s, 0), 0.0)
                else:
                    b = b + jnp.where(rowi >= s, pltpu.roll(b, s, 0), 0.0)
            b_last = b[0:1, :] if rev else b[c_sz - 1:c_sz, :]
            o = _dot_nt((q * jnp.exp(b)).astype(BF16), st.astype(BF16))
            diag = []
            for jb in range(nsb):
                r8 = slice(jb * SUB, (jb + 1) * SUB)
                bj, qj, kj, vj = b[r8], q[r8], k[r8], v[r8]
                oj = jnp.zeros((SUB, LANE), F32)
                for s in range(SUB):
                    valid = (row8 <= s) if rev else (row8 >= s)
                    dec = jnp.exp(jnp.where(valid, bj - bj[s:s + 1, :], 0.0))
                    pr = jnp.where(valid, qj * (kj[s:s + 1, :] * dec), 0.0)
                    oj = oj + jnp.sum(pr, axis=-1, keepdims=True) * vj[s:s + 1, :]
                diag.append(oj)
            o = o + jnp.concatenate(diag, axis=0)
            att = jnp.zeros((c_sz, c_sz), F32)
            for jb in (range(1, nsb) if rev else range(nsb - 1)):
                edge = jb * SUB if rev else jb * SUB + SUB - 1
                b_ref_row = b[edge:edge + 1, :]
                in_blk = (rowi >= jb * SUB) & (rowi < (jb + 1) * SUB)
                kt = jnp.where(in_blk, k * jnp.exp(jnp.minimum(b_ref_row - b, 0.0)), 0.0)
                qq = q * jnp.exp(jnp.minimum(b - b_ref_row, 0.0))
                a = _dot_nt(qq.astype(BF16), kt.astype(BF16))
                t_ok = (rowa < jb * SUB) if rev else (rowa >= (jb + 1) * SUB)
                att = att + jnp.where(t_ok, a, 0.0)
            o = o + _dot(att.astype(BF16), v.astype(BF16))
            o_ref[rows, cols] = o
            kk = k * jnp.exp(b_last - b)
            return st * jnp.exp(b_last) + _dot_tn(v.astype(BF16), kk.astype(BF16))

        st_ref[hh] = lax.fori_loop(0, nchunk, chunk_body, st_ref[hh])
        return carry

    lax.fori_loop(0, H_B, head_body, 0)


def _hgrn(p, lb, rev, nctx):
    bsz, l, _ = p.shape
    db = lb.shape[-1]
    nblk = l // TM
    blk = functools.partial(_scan_block, rev=rev, nctx=nctx, nblk=nblk)
    fcol = 4 if rev else 3
    return pl.pallas_call(
        functools.partial(_hgrn_kernel, rev=rev),
        out_shape=jax.ShapeDtypeStruct((bsz, l, db), F32),
        grid=(bsz, nblk),
        in_specs=[
            pl.BlockSpec((None, TM, db), lambda b, p_: (b, blk(p_), 2)),
            pl.BlockSpec((None, TM, db), lambda b, p_: (b, blk(p_), fcol)),
            pl.BlockSpec((None, TM, db), lambda b, p_: (b, blk(p_), 5)),
            _const_spec((1, db)),
        ],
        out_specs=pl.BlockSpec((None, TM, db), lambda b, p_: (b, blk(p_), 0)),
        scratch_shapes=[pltpu.VMEM((H_B, LANE, LANE), F32)],
        compiler_params=_cp(("parallel", "arbitrary")),
        name="hgrn_bwd" if rev else "hgrn_fwd",
    )(p, p, p, lb)


def _s5_kernel(u_ref, bb_ref, lam_ref, cc_ref, o_ref, bu_scr, st_ref, *, rev, tb, bsz, nslab):
    half = S5_SLAB_G * S5_STATE

    @pl.when(pl.program_id(0) == 0)
    def _():
        st_ref[...] = jnp.zeros_like(st_ref)

    for j in range(nslab):
        uj = u_ref[:, :, j * LANE:(j + 1) * LANE].reshape(tb * bsz, LANE).astype(BF16)
        bu_scr[...] = _dot(uj, bb_ref[j])
        lre = jnp.broadcast_to(lam_ref[j, 0:1, :], (bsz, half))
        lim = jnp.broadcast_to(lam_ref[j, 1:2, :], (bsz, half))

        def step(i, carry):
            xre, xim = carry
            t = (tb - 1 - i) if rev else i
            rows = pl.ds(pl.multiple_of(t * bsz, bsz), bsz)
            nre = lre * xre - lim * xim + bu_scr[rows, 0:half]
            nim = lre * xim + lim * xre + bu_scr[rows, half:2 * half]
            bu_scr[rows, 0:half] = nre
            bu_scr[rows, half:2 * half] = nim
            return nre, nim

        xre, xim = lax.fori_loop(0, tb, step, (st_ref[j, :, 0:half], st_ref[j, :, half:2 * half]), unroll=4)
        st_ref[j, :, 0:half] = xre
        st_ref[j, :, half:2 * half] = xim
        y = _dot(bu_scr[...].astype(BF16), cc_ref[j])
        o_ref[:, :, j * LANE:(j + 1) * LANE] = y.reshape(tb, bsz, LANE)


def _s5(u_tm, bbig, lam, cbig, rev, ctx_len):
    l, bsz, dc = u_tm.shape
    nslab = dc // LANE
    tb = 128
    nblk, nctx = l // tb, ctx_len // tb
    blk = functools.partial(_scan_block, rev=rev, nctx=nctx, nblk=nblk)
    half = S5_SLAB_G * S5_STATE
    return pl.pallas_call(
        functools.partial(_s5_kernel, rev=rev, tb=tb, bsz=bsz, nslab=nslab),
        out_shape=jax.ShapeDtypeStruct((l, bsz, dc), F32),
        grid=(nblk,),
        in_specs=[
            pl.BlockSpec((tb, bsz, dc), lambda p_: (blk(p_), 0, 0)),
            _const_spec((nslab, LANE, 2 * half)),
            _const_spec((nslab, 2, half)),
            _const_spec((nslab, 2 * half, LANE)),
        ],
        out_specs=pl.BlockSpec((tb, bsz, dc), lambda p_: (blk(p_), 0, 0)),
        scratch_shapes=[pltpu.VMEM((tb * bsz, 2 * half), F32), pltpu.VMEM((nslab, bsz, 2 * half), F32)],
        compiler_params=_cp(("arbitrary",)),
        name="s5_bwd" if rev else "s5_fwd",
    )(u_tm, bbig, lam, cbig)


def _s5_params(lam_re, lam_im, log_step, b_re, b_im, c_re, c_im):
    g_c, p_s = lam_re.shape
    nslab = g_c // S5_SLAB_G
    lam = lax.complex(lam_re.astype(F32), lam_im.astype(F32))
    lam_dt = lam * jnp.exp(log_step.astype(F32))[:, None]
    lam_bar = jnp.exp(lam_dt)
    b_bar = ((lam_bar - 1.0) / lam)[:, :, None] * lax.complex(b_re.astype(F32), b_im.astype(F32))
    eye = jnp.eye(S5_SLAB_G, dtype=F32)

    def in_map(m):
        m = m.reshape(nslab, S5_SLAB_G, p_s, S5_IN)
        return jnp.einsum('jgpd,gh->jgdhp', m, eye).reshape(nslab, S5_SLAB_G * S5_IN, S5_SLAB_G * p_s)

    def out_map(m):
        m = m.reshape(nslab, S5_SLAB_G, S5_IN, p_s)
        return jnp.einsum('jgdp,gh->jgphd', m, eye).reshape(nslab, S5_SLAB_G * p_s, S5_SLAB_G * S5_IN)

    bbig = jnp.concatenate([in_map(jnp.real(b_bar)), in_map(jnp.imag(b_bar))], axis=-1).astype(BF16)
    cbig = jnp.concatenate([out_map(c_re.astype(F32)), out_map(-c_im.astype(F32))], axis=1).astype(BF16)
    lam_s = jnp.stack([jnp.real(lam_bar).reshape(nslab, -1), jnp.imag(lam_bar).reshape(nslab, -1)], axis=1)
    return bbig, lam_s, cbig


def _merge_a_kernel(ya_ref, of_ref, ob_ref, bg_ref, yf_ref, yb_ref, cu_ref, ga_ref, gb_ref, gc_ref,
                    hng_ref, dsk_ref, wglu_ref, wa_ref, wb_ref, wc_ref, o_ref):
    dc = dsk_ref.shape[-1]
    o = of_ref[...] + ob_ref[...]
    parts = []
    for hh in range(H_B):
        oh = o[:, hh * LANE:(hh + 1) * LANE]
        parts.append(oh * lax.rsqrt(jnp.mean(oh * oh, axis=-1, keepdims=True) + EPS))
    bg = bg_ref[...]
    yb = jnp.concatenate(parts, axis=-1) * hng_ref[...] * (bg * _sigmoid(bg))
    y = yf_ref[...] + yb_ref[...] + dsk_ref[...] * cu_ref[:, 0:dc]
    z = _dot(_gelu(y).astype(BF16), wglu_ref[...])
    yc = z[:, 0:dc] * _sigmoid(z[:, dc:2 * dc])
    merged = (_sigmoid(ga_ref[...]) * _dot(ya_ref[...].astype(BF16), wa_ref[...])
              + _sigmoid(gb_ref[...]) * _dot(yb.astype(BF16), wb_ref[...])
              + _sigmoid(gc_ref[...]) * _dot(yc.astype(BF16), wc_ref[...]))
    o_ref[...] = merged.astype(BF16)


def _merge_a(ya, o_f, o_b, p, y_f, y_b, hn_g, d_skip, w_glu, wa, wb, wc, off):
    bsz, l, da = ya.shape
    dc = y_f.shape[-1]
    d = wa.shape[-1]
    tm = 128
    o2 = off * (TM // tm)
    nblk = l // tm - o2
    tok = lambda b, j: (b, j + o2, 0)
    pcol = lambda c: (lambda b, j: (b, j + o2, c))
    return pl.pallas_call(
        _merge_a_kernel,
        out_shape=jax.ShapeDtypeStruct((bsz, nblk * tm, d), BF16),
        grid=(bsz, nblk),
        in_specs=[
            pl.BlockSpec((None, tm, da), tok),
            pl.BlockSpec((None, tm, da), tok),
            pl.BlockSpec((None, tm, da), tok),
            pl.BlockSpec((None, tm, da), pcol(6)),
            pl.BlockSpec((None, tm, dc), tok),
            pl.BlockSpec((None, tm, dc), tok),
            pl.BlockSpec((None, tm, da), pcol(7)),
            pl.BlockSpec((None, tm, d), pcol(4)),
            pl.BlockSpec((None, tm, d), pcol(5)),
            pl.BlockSpec((None, tm, d), pcol(6)),
            _const_spec((1, da)),
            _const_spec((1, dc)),
            _const_spec(w_glu.shape),
            _const_spec(wa.shape),
            _const_spec(wb.shape),
            _const_spec(wc.shape),
        ],
        out_specs=pl.BlockSpec((None, tm, d), lambda b, j: (b, j, 0)),
        compiler_params=_cp(("parallel", "parallel")),
        name="merge_a",
    )(ya, o_f, o_b, p, y_f, y_b, p, p, p, p, hn_g, d_skip, w_glu, wa, wb, wc)


def _merge_b_kernel(m_ref, x_ref, mod_ref, wo_ref, wq_ref, x1_ref, h2_ref, q_ref):
    x1 = x_ref[...] + mod_ref[2:3, :] * _dot(m_ref[...], wo_ref[...])
    x1_ref[...] = x1
    r = lax.rsqrt(jnp.mean(x1 * x1, axis=-1, keepdims=True) + EPS)
    h2 = x1 * r * (1.0 + mod_ref[4:5, :]) + mod_ref[3:4, :]
    h2_ref[...] = h2
    q_ref[...] = _dot(h2.astype(BF16), wq_ref[...])


def _merge_b(merged, xall, mod, wo, wq, off, nctx):
    bsz, lo, d = merged.shape
    nblk = lo // TM
    dq = wq.shape[-1]
    out = lambda b, j: (b, j, 0)
    return pl.pallas_call(
        _merge_b_kernel,
        out_shape=(jax.ShapeDtypeStruct((bsz, lo, d), F32), jax.ShapeDtypeStruct((bsz, lo, d), F32),
                   jax.ShapeDtypeStruct((bsz, lo, dq), F32)),
        grid=(bsz, nblk),
        in_specs=[
            pl.BlockSpec((None, TM, d), out),
            pl.BlockSpec((None, TM, d), lambda b, j: (b, j + off, 0)),
            pl.BlockSpec((None, None, N_MOD, d), lambda b, j: (b, (j + off >= nctx).astype(jnp.int32), 0, 0)),
            _const_spec(wo.shape),
            _const_spec(wq.shape),
        ],
        out_specs=(pl.BlockSpec((None, TM, d), out), pl.BlockSpec((None, TM, d), out),
                   pl.BlockSpec((None, TM, dq), out)),
        compiler_params=_cp(("parallel", "parallel")),
        name="merge_b",
    )(merged, xall, mod, wo, wq)


def _topk_rows(s, row_id, k):
    slot = lax.broadcasted_iota(jnp.int32, (k, s.shape[1]), 0)
    vals = jnp.zeros((k, s.shape[1]), F32)
    ids = jnp.zeros((k, s.shape[1]), F32)
    for r in range(k):
        m = jnp.max(s, axis=0, keepdims=True)
        am = jnp.min(jnp.where(s == m, row_id, 1e9), axis=0, keepdims=True)
        vals = jnp.where(slot == r, m, vals)
        ids = jnp.where(slot == r, am, ids)
        s = jnp.where(row_id == am, -jnp.inf, s)
    return vals, ids


def _pair_candidates():
    k = PEER_TOPK
    pieces = [("a", 0, 0, 16), ("a", 1, 0, 8), ("a", 2, 0, 8), ("a", 3, 0, 8), ("a", 4, 0, 8),
              ("b", 0, 0, 8), ("b", 1, 0, 8), ("b", 0, 8, 16)]
    ids, bias, seen = [], [], set()
    for kind, fixed, lo, hi in pieces:
        for r in range(lo, hi):
            a, b = (fixed, r) if kind == "a" else (r, fixed)
            ok = (a + 1) * (b + 1) <= k and (a, b) not in seen
            if ok:
                seen.add((a, b))
            ids.append(float(a * k + b))
            bias.append(0.0 if ok else -math.inf)
    assert len(seen) == sum(k // (a + 1) for a in range(k))
    return pieces, ids, bias


def _route_kernel(q_ref, keys_ref, cid_ref, cbias_ref, idx_ref, g_ref):
    k = PEER_TOPK
    pieces, _, _ = _pair_candidates()
    key_id = lax.broadcasted_iota(jnp.int32, (PEER_NKEYS, q_ref.shape[0]), 0).astype(F32)
    for h in range(PEER_HEADS):
        sv, si = [], []
        for half in range(2):
            c0 = (h * 2 + half) * PEER_NKEYS
            qh = q_ref[:, c0:c0 + PEER_NKEYS].astype(BF16)
            s = _dot_nt(keys_ref[h, half], qh)
            v_, i_ = _topk_rows(s, key_id, k)
            sv.append(v_)
            si.append(i_)
        cand = jnp.concatenate(
            [sv[0][f:f + 1, :] + sv[1][lo:hi, :] if kind == "a" else sv[0][lo:hi, :] + sv[1][f:f + 1, :]
             for kind, f, lo, hi in pieces], axis=0) + cbias_ref[...]
        fv, fi = _topk_rows(cand, cid_ref[...], k)
        fa = jnp.floor(fi * (1.0 / k))
        fb = fi - fa * k
        i1 = jnp.zeros_like(fi)
        i2 = jnp.zeros_like(fi)
        for a in range(k):
            i1 = jnp.where(fa == a, si[0][a:a + 1, :], i1)
            i2 = jnp.where(fb == a, si[1][a:a + 1, :], i2)
        e = jnp.exp(fv - fv[0:1, :])
        g_ref[h * k:(h + 1) * k, :] = e / jnp.sum(e, axis=0, keepdims=True)
        idx_ref[h * k:(h + 1) * k, :] = (i1 * PEER_NKEYS + i2).astype(jnp.int32)


def _route(q, keys):
    t, dq = q.shape
    tm = LANE
    nsel = PEER_HEADS * PEER_TOPK
    _, ids, bias = _pair_candidates()
    cid = jnp.broadcast_to(jnp.asarray(ids, F32)[:, None], (len(ids), tm))
    cbias = jnp.broadcast_to(jnp.asarray(bias, F32)[:, None], (len(bias), tm))
    return pl.pallas_call(
        _route_kernel,
        out_shape=(jax.ShapeDtypeStruct((nsel, t), jnp.int32), jax.ShapeDtypeStruct((nsel, t), F32)),
        grid=(t // tm,),
        in_specs=[pl.BlockSpec((tm, dq), lambda i: (i, 0)), _const_spec(keys.shape),
                  _const_spec(cid.shape), _const_spec(cbias.shape)],
        out_specs=(pl.BlockSpec((nsel, tm), lambda i: (0, i)), pl.BlockSpec((nsel, tm), lambda i: (0, i))),
        compiler_params=_cp(("parallel",)),
        name="peer_route",
    )(q, keys, cid, cbias)


PEER_TB = 32
PEER_SLOTS = 4
PEER_AHEAD = PEER_SLOTS - 1
ROW_SUB = 16


def _experts_kernel(idx_ref, g_ref, h_ref, x_ref, mod_ref, tab_ref, o_ref, buf, sem):
    nsel = PEER_HEADS * PEER_TOPK
    step = pl.program_id(0)

    def issue(t, slot):
        for j in range(nsel):
            e = idx_ref[0, t * nsel + j]
            pltpu.make_async_copy(tab_ref.at[e], buf.at[slot, :, j, :], sem.at[slot]).start()

    def wait(slot):
        pltpu.make_async_copy(buf.at[slot], buf.at[slot], sem.at[slot]).wait()

    @pl.when(step == 0)
    def _():
        for s in range(PEER_AHEAD):
            issue(s, s)

    eye = lax.broadcasted_iota(jnp.int32, (nsel, nsel), 0) == lax.broadcasted_iota(jnp.int32, (nsel, nsel), 1)

    def group(t4, c):
        for s in range(PEER_SLOTS):
            t = t4 * PEER_SLOTS + s
            wait(s)
            issue(t + PEER_AHEAD, (s + PEER_AHEAD) % PEER_SLOTS)
            row = pl.ds(t, 1)
            hrow = h_ref[row, :]
            acc = buf[s, 0] * hrow[:, 0:LANE]
            for cc in range(1, ROW_SUB):
                acc = acc + buf[s, cc] * hrow[:, cc * LANE:(cc + 1) * LANE]
            act = jnp.sum(acc, axis=-1, keepdims=True)
            gcol = jnp.sum(jnp.where(eye, g_ref[row, :], 0.0), axis=-1, keepdims=True)
            w = _gelu(act) * gcol
            out = jnp.concatenate([jnp.sum(w * buf[s, ROW_SUB + cc], axis=0, keepdims=True)
                                   for cc in range(ROW_SUB)], axis=-1)
            o_ref[row, :] = x_ref[row, :] + mod_ref[5:6, :] * out
        return c

    lax.fori_loop(0, PEER_TB // PEER_SLOTS, group, 0)

    @pl.when(step == pl.num_programs(0) - 1)
    def _():
        for s in range(PEER_AHEAD):
            wait(s)


def _experts(idx, g, h2, x1, mod, tab, off, nctx):
    t_all, d = h2.shape
    bsz = mod.shape[0]
    nsel = idx.shape[-1]
    tb = PEER_TB
    nb = t_all // tb
    per_b = nb // bsz
    per = TM // tb
    assert d == ROW_SUB * LANE and tb % PEER_SLOTS == 0
    main = idx.reshape(nb, tb * nsel)
    idx_ext = jnp.concatenate([main, jnp.roll(main, -1, axis=0)[:, :PEER_AHEAD * nsel]], axis=1)
    idx_ext = idx_ext.reshape(nb, 1, (tb + PEER_AHEAD) * nsel)
    tokb = lambda i: (i, 0)

    def mod_map(i):
        return (i // per_b, ((i % per_b) // per + off >= nctx).astype(jnp.int32), 0, 0)

    return pl.pallas_call(
        _experts_kernel,
        out_shape=jax.ShapeDtypeStruct((t_all, d), F32),
        grid=(nb,),
        in_specs=[
            pl.BlockSpec((None, 1, (tb + PEER_AHEAD) * nsel), lambda i: (i, 0, 0), memory_space=pltpu.SMEM),
            pl.BlockSpec((tb, nsel), tokb),
            pl.BlockSpec((tb, d), tokb),
            pl.BlockSpec((tb, d), tokb),
            pl.BlockSpec((None, None, N_MOD, d), mod_map),
            pl.BlockSpec(memory_space=pl.ANY),
        ],
        out_specs=pl.BlockSpec((tb, d), tokb),
        scratch_shapes=[pltpu.VMEM((PEER_SLOTS, 2 * ROW_SUB, nsel, LANE), F32),
                        pltpu.SemaphoreType.DMA((PEER_SLOTS,))],
        compiler_params=_cp(("arbitrary",)),
        name="peer_experts",
    )(idx_ext, g, h2, x1, mod, tab)


def _final_kernel(x_ref, g_ref, o_ref):
    x = x_ref[...]
    o_ref[...] = x * lax.rsqrt(jnp.mean(x * x, axis=-1, keepdims=True) + EPS) * g_ref[...]


def _final_norm(x, g):
    bsz, n, d = x.shape
    return pl.pallas_call(
        _final_kernel,
        out_shape=jax.ShapeDtypeStruct((bsz, n, d), F32),
        grid=(bsz, n // TM),
        in_specs=[pl.BlockSpec((None, TM, d), lambda b, j: (b, j, 0)), _const_spec((1, d))],
        out_specs=pl.BlockSpec((None, TM, d), lambda b, j: (b, j, 0)),
        compiler_params=_cp(("parallel", "parallel")),
        name="final_norm",
    )(x, g.reshape(1, d))


def _pos_embed_2d(n, d):
    n_rows = n // GRID_W
    quarter = d // 4
    omega = 1.0 / (POS_THETA ** (jnp.arange(quarter, dtype=F32) / quarter))

    def axis_code(length):
        ang = jnp.arange(length, dtype=F32)[:, None] * omega[None]
        return jnp.concatenate([jnp.sin(ang), jnp.cos(ang)], axis=-1)

    er = jnp.broadcast_to(axis_code(n_rows)[:, None], (n_rows, GRID_W, d // 2))
    ec = jnp.broadcast_to(axis_code(GRID_W)[None], (n_rows, GRID_W, d // 2))
    return jnp.concatenate([er, ec], axis=-1).reshape(n, d)


def kernel(x, c, ctx, c_ctx, w_mod, b_mod, w_in, gmlp_ln_g, gmlp_ln_b, gmlp_w_s, gmlp_b_s, hgrn_lb_logits, hgrn_norm_g, s5_lam_re, s5_lam_im, s5_log_step, s5_b_re, s5_b_im, s5_c_re, s5_c_im, s5_d, s5_w_glu, w_branch_a, w_branch_b, w_branch_c, w_out, peer_w_q, peer_sub_keys, peer_u, peer_v, final_norm_g):
    bsz, n, d = x.shape
    ctx_len = ctx.shape[1]
    depth = w_mod.shape[0]
    nctx = ctx_len // TM
    da = gmlp_ln_g.shape[-1]
    db = hgrn_norm_g.shape[-1]
    dc = s5_d.shape[-1]
    assert n % TM == 0 and ctx_len % TM == 0 and bsz % 8 == 0

    xall = _embed(x, ctx, _pos_embed_2d(n, d).astype(x.dtype))

    rows = -(-(bsz + 1) // 8) * 8
    cc = jnp.zeros((rows, d), F32).at[0].set(c_ctx).at[1:bsz + 1].set(c)
    mod_all = _modulation(cc, w_mod, b_mod)

    gamma = jax.nn.softmax(hgrn_lb_logits.astype(F32), axis=0)
    lower_bounds = jnp.cumsum(gamma, axis=0) - gamma[0:1]

    for l in range(depth):
        last = l == depth - 1
        off = nctx if last else 0
        mod_c = jnp.broadcast_to(mod_all[l, 0][None], (bsz, N_MOD * d))
        mod = jnp.stack([mod_c, mod_all[l, 1:bsz + 1]], axis=1).reshape(bsz, 2, N_MOD, d)

        cu0 = 2 * da + 5 * db
        w_pad = jnp.concatenate([w_in[l][:, :cu0 + dc], jnp.zeros((d, da - dc), F32), w_in[l][:, cu0 + dc:]],
                                axis=1).astype(BF16)
        p = _in_proj(xall, mod, w_pad, ctx_len)

        bias = jnp.repeat(gmlp_b_s[l].T, da // G_A, axis=1)
        ya = _gmlp(p, gmlp_ln_g[l].reshape(1, da), gmlp_ln_b[l].reshape(1, da), gmlp_w_s[l].astype(BF16), bias)

        o_f = _hgrn(p, lower_bounds[l, 0].reshape(1, db), False, nctx)
        o_b = _hgrn(p, lower_bounds[l, 1].reshape(1, db), True, nctx)

        u_tm = jnp.transpose(p[:, :, cu0:cu0 + dc], (1, 0, 2))
        ys = []
        for r in range(2):
            prm = _s5_params(s5_lam_re[l, r], s5_lam_im[l, r], s5_log_step[l, r], s5_b_re[l, r], s5_b_im[l, r],
                             s5_c_re[l, r], s5_c_im[l, r])
            ys.append(jnp.transpose(_s5(u_tm, *prm, rev=bool(r), ctx_len=ctx_len), (1, 0, 2)))

        merged = _merge_a(ya, o_f, o_b, p, ys[0], ys[1], hgrn_norm_g[l].reshape(1, db), s5_d[l].reshape(1, dc),
                          s5_w_glu[l].astype(BF16), w_branch_a[l].astype(BF16), w_branch_b[l].astype(BF16),
                          w_branch_c[l].astype(BF16), off)
        x1, h2, q = _merge_b(merged, xall, mod, w_out[l].astype(BF16), peer_w_q[l].astype(BF16), off, nctx)

        lo = x1.shape[1]
        idx_t, g_t = _route(q.reshape(bsz * lo, -1), peer_sub_keys[l].astype(BF16))
        n_exp = peer_u.shape[1]
        tab = jnp.concatenate([peer_u[l].reshape(n_exp, ROW_SUB, LANE), peer_v[l].reshape(n_exp, ROW_SUB, LANE)], axis=1)
        xall = _experts(idx_t.T, g_t.T, h2.reshape(bsz * lo, d), x1.reshape(bsz * lo, d), mod, tab, off, nctx)
        xall = xall.reshape(bsz, lo, d)

    return _final_norm(xall, final_norm_g.astype(F32))
```

```python
import functools
import math

import jax
import jax.numpy as jnp
from jax import lax
from jax.experimental import pallas as pl
from jax.experimental.pallas import tpu as pltpu

F32 = jnp.float32
BF16 = jnp.bfloat16

EPS = 1e-6
GRID_W = 64
POS_THETA = 10000.0
N_MOD = 6
G_A = 8
GMLP_CHUNK = 128
H_B = 8
HGRN_CHUNK = 32
S5_IN = 16
S5_STATE = 64
S5_SLAB_G = 8
PEER_HEADS = 8
PEER_NKEYS = 128
PEER_TOPK = 16

TM = 256
LANE = 128
SUB = 8
VMEM_LIMIT = 56 * 1024 * 1024


def _cp(sem, vmem=VMEM_LIMIT):
    return pltpu.CompilerParams(dimension_semantics=sem, vmem_limit_bytes=vmem)


def _const_spec(shape):
    nd = len(shape)
    return pl.BlockSpec(shape, lambda *_: (0,) * nd, pipeline_mode=pl.Buffered(1))


def _gelu(x):
    return x * (lax.erf(x * (1.0 / math.sqrt(2.0))) + 1.0) * 0.5


def _sigmoid(x):
    return jax.nn.sigmoid(x)


def _dot(a, b):
    return jnp.dot(a, b, preferred_element_type=F32)


def _dot_nt(a, b):
    return lax.dot_general(a, b, (((1,), (1,)), ((), ())), preferred_element_type=F32)


def _dot_tn(a, b):
    return lax.dot_general(a, b, (((0,), (0,)), ((), ())), preferred_element_type=F32)


def _embed_kernel(ctx_ref, x_ref, pos_ref, o_ref, *, nctx):
    j = pl.program_id(1)

    @pl.when(j < nctx)
    def _():
        o_ref[...] = ctx_ref[...]

    @pl.when(j >= nctx)
    def _():
        o_ref[...] = x_ref[...] + pos_ref[...]


def _embed(x, ctx, pos):
    bsz, n, d = x.shape
    nctx = ctx.shape[1] // TM
    nblk = nctx + n // TM
    return pl.pallas_call(
        functools.partial(_embed_kernel, nctx=nctx),
        out_shape=jax.ShapeDtypeStruct((bsz, nblk * TM, d), x.dtype),
        grid=(bsz, nblk),
        in_specs=[
            pl.BlockSpec((None, TM, d), lambda b, j: (b, jnp.minimum(j, nctx - 1), 0)),
            pl.BlockSpec((None, TM, d), lambda b, j: (b, jnp.maximum(j - nctx, 0), 0)),
            pl.BlockSpec((TM, d), lambda b, j: (jnp.maximum(j - nctx, 0), 0)),
        ],
        out_specs=pl.BlockSpec((None, TM, d), lambda b, j: (b, j, 0)),
        compiler_params=_cp(("parallel", "arbitrary")),
        name="embed",
    )(ctx, x, pos)


def _mod_kernel(c_ref, w_ref, b_ref, o_ref):
    c = c_ref[...]
    s = c * _sigmoid(c)
    o_ref[...] = _dot(s.astype(BF16), w_ref[...].astype(BF16)) + b_ref[...]


def _modulation(cc, w_mod, b_mod):
    depth, d, nd = w_mod.shape
    rows = cc.shape[0]
    tn = 1024
    return pl.pallas_call(
        _mod_kernel,
        out_shape=jax.ShapeDtypeStruct((depth, rows, nd), F32),
        grid=(depth, nd // tn),
        in_specs=[
            pl.BlockSpec((rows, d), lambda l, n: (0, 0)),
            pl.BlockSpec((None, d, tn), lambda l, n: (l, 0, n)),
            pl.BlockSpec((None, 1, tn), lambda l, n: (l, 0, n)),
        ],
        out_specs=pl.BlockSpec((None, rows, tn), lambda l, n: (l, 0, n)),
        compiler_params=_cp(("parallel", "arbitrary")),
        name="modulation",
    )(cc, w_mod, b_mod.reshape(depth, 1, nd))


def _inproj_kernel(x_ref, mod_ref, w_ref, o_ref, h_scr, *, tm, ctx_rows):
    j = pl.program_id(1)

    @pl.when(pl.program_id(2) == 0)
    def _():
        x = x_ref[...]
        r = lax.rsqrt(jnp.mean(x * x, axis=-1, keepdims=True) + EPS)
        row = j * tm + lax.broadcasted_iota(jnp.int32, (tm, 1), 0)
        is_ctx = row < ctx_rows
        sh = jnp.where(is_ctx, mod_ref[0, 0:1, :], mod_ref[1, 0:1, :])
        sc = jnp.where(is_ctx, mod_ref[0, 1:2, :], mod_ref[1, 1:2, :])
        h_scr[...] = (x * r * (1.0 + sc) + sh).astype(BF16)

    o_ref[...] = _dot(h_scr[...], w_ref[...])


def _in_proj(xall, mod, w_pad, ctx_rows):
    bsz, l, d = xall.shape
    ncol = w_pad.shape[1]
    tm = 768 if l % 768 == 0 else TM
    tn = 1024
    return pl.pallas_call(
        functools.partial(_inproj_kernel, tm=tm, ctx_rows=ctx_rows),
        out_shape=jax.ShapeDtypeStruct((bsz, l, ncol), F32),
        grid=(bsz, l // tm, ncol // tn),
        in_specs=[
            pl.BlockSpec((None, tm, d), lambda b, j, n: (b, j, 0)),
            pl.BlockSpec((None, 2, N_MOD, d), lambda b, j, n: (b, 0, 0, 0)),
            pl.BlockSpec((d, tn), lambda b, j, n: (0, n)),
        ],
        out_specs=pl.BlockSpec((None, tm, tn), lambda b, j, n: (b, j, n)),
        scratch_shapes=[pltpu.VMEM((tm, d), BF16)],
        compiler_params=_cp(("parallel", "parallel", "arbitrary")),
        name="in_proj",
    )(xall, mod, w_pad)


def _gmlp_kernel(u_ref, v_ref, lng_ref, lnb_ref, ws_ref, bias_ref, o_ref):
    v = _gelu(v_ref[...])
    mu = jnp.mean(v, axis=-1, keepdims=True)
    vc = v - mu
    var = jnp.mean(vc * vc, axis=-1, keepdims=True)
    v = (vc * lax.rsqrt(var + EPS) * lng_ref[...] + lnb_ref[...]).astype(BF16)
    for c in range(TM // GMLP_CHUNK):
        rows = slice(c * GMLP_CHUNK, (c + 1) * GMLP_CHUNK)
        for g in range(G_A):
            cols = slice(g * LANE, (g + 1) * LANE)
            mixed = _dot(ws_ref[g], v[rows, cols]) + bias_ref[:, cols]
            o_ref[rows, cols] = _gelu(u_ref[rows, cols]) * mixed


def _gmlp(p, ln_g, ln_b, w_s, bias):
    bsz, l, _ = p.shape
    da = ln_g.shape[-1]
    return pl.pallas_call(
        _gmlp_kernel,
        out_shape=jax.ShapeDtypeStruct((bsz, l, da), F32),
        grid=(bsz, l // TM),
        in_specs=[
            pl.BlockSpec((None, TM, da), lambda b, j: (b, j, 0)),
            pl.BlockSpec((None, TM, da), lambda b, j: (b, j, 1)),
            _const_spec((1, da)),
            _const_spec((1, da)),
            _const_spec((G_A, GMLP_CHUNK, GMLP_CHUNK)),
            _const_spec((GMLP_CHUNK, da)),
        ],
        out_specs=pl.BlockSpec((None, TM, da), lambda b, j: (b, j, 0)),
        compiler_params=_cp(("parallel", "parallel")),
        name="gmlp",
    )(p, p, ln_g, ln_b, w_s, bias)


def _scan_block(p, rev, nctx, nblk):
    if not rev:
        return p
    return jnp.where(p < nctx, nctx - 1 - p, nblk - 1 - (p - nctx))


def _hgrn_kernel(q_ref, f_ref, i_ref, lb_ref, o_ref, st_ref, *, rev):
    c_sz = HGRN_CHUNK
    nchunk = TM // c_sz

    @pl.when(pl.program_id(1) == 0)
    def _():
        st_ref[...] = jnp.zeros_like(st_ref)

    nsb = c_sz // SUB
    rowi = lax.broadcasted_iota(jnp.int32, (c_sz, LANE), 0)
    row8 = lax.broadcasted_iota(jnp.int32, (SUB, LANE), 0)
    rowa = lax.broadcasted_iota(jnp.int32, (c_sz, c_sz), 0)

    def chunk_body(ci, carry):
        c = (nchunk - 1 - ci) if rev else ci
        rows = pl.ds(pl.multiple_of(c * c_sz, c_sz), c_sz)
        for hh in range(H_B):
            cols = slice(hh * LANE, (hh + 1) * LANE)
            lbh = lb_ref[:, cols]
            st = st_ref[hh]
            qr = q_ref[rows, cols]
            q = qr * _sigmoid(qr)
            fg = lbh + (1.0 - lbh) * _sigmoid(f_ref[rows, cols])
            k = 1.0 - fg
            v = i_ref[rows, cols]
            b = jnp.log(fg)
            for s in (1, 2, 4, 8, 16):
                if rev:
                    b = b + jnp.where(rowi < c_sz - s, pltpu.roll(b, c_sz - s, 0), 0.0)
                else:
                    b = b + jnp.where(rowi >= s, pltpu.roll(b, s, 0), 0.0)
            b_last = b[0:1, :] if rev else b[c_sz - 1:c_sz, :]
            o = _dot_nt((q * jnp.exp(b)).astype(BF16), st.astype(BF16))
            diag = []
            for jb in range(nsb):
                r8 = slice(jb * SUB, (jb + 1) * SUB)
                bj, qj, kj, vj = b[r8], q[r8], k[r8], v[r8]
                oj = jnp.zeros((SUB, LANE), F32)
                for s in range(SUB):
                    valid = (row8 <= s) if rev else (row8 >= s)
                    dec = jnp.exp(jnp.where(valid, bj - bj[s:s + 1, :], 0.0))
                    pr = jnp.where(valid, qj * (kj[s:s + 1, :] * dec), 0.0)
                    oj = oj + jnp.sum(pr, axis=-1, keepdims=True) * vj[s:s + 1, :]
                diag.append(oj)
            o = o + jnp.concatenate(diag, axis=0)
            att = jnp.zeros((c_sz, c_sz), F32)
            for jb in (range(1, nsb) if rev else range(nsb - 1)):
                edge = jb * SUB if rev else jb * SUB + SUB - 1
                b_ref_row = b[edge:edge + 1, :]
                in_blk = (rowi >= jb * SUB) & (rowi < (jb + 1) * SUB)
                kt = jnp.where(in_blk, k * jnp.exp(jnp.minimum(b_ref_row - b, 0.0)), 0.0)
                qq = q * jnp.exp(jnp.minimum(b - b_ref_row, 0.0))
                a = _dot_nt(qq.astype(BF16), kt.astype(BF16))
                t_ok = (rowa < jb * SUB) if rev else (rowa >= (jb + 1) * SUB)
                att = att + jnp.where(t_ok, a, 0.0)
            o = o + _dot(att.astype(BF16), v.astype(BF16))
            o_ref[rows, cols] = o
            kk = k * jnp.exp(b_last - b)
            st_ref[hh] = st * jnp.exp(b_last) + _dot_tn(v.astype(BF16), kk.astype(BF16))
        return carry

    lax.fori_loop(0, nchunk, chunk_body, 0)


def _hgrn(p, lb, rev, nctx):
    bsz, l, _ = p.shape
    db = lb.shape[-1]
    nblk = l // TM
    blk = functools.partial(_scan_block, rev=rev, nctx=nctx, nblk=nblk)
    fcol = 4 if rev else 3
    return pl.pallas_call(
        functools.partial(_hgrn_kernel, rev=rev),
        out_shape=jax.ShapeDtypeStruct((bsz, l, db), F32),
        grid=(bsz, nblk),
        in_specs=[
            pl.BlockSpec((None, TM, db), lambda b, p_: (b, blk(p_), 2)),
            pl.BlockSpec((None, TM, db), lambda b, p_: (b, blk(p_), fcol)),
            pl.BlockSpec((None, TM, db), lambda b, p_: (b, blk(p_), 5)),
            _const_spec((1, db)),
        ],
        out_specs=pl.BlockSpec((None, TM, db), lambda b, p_: (b, blk(p_), 0)),
        scratch_shapes=[pltpu.VMEM((H_B, LANE, LANE), F32)],
        compiler_params=_cp(("parallel", "arbitrary")),
        name="hgrn_bwd" if rev else "hgrn_fwd",
    )(p, p, p, lb)


def _s5_kernel(u_ref, bb_ref, lam_ref, cc_ref, o_ref, bu_scr, st_ref, *, rev, tb, bsz, nslab):
    half = S5_SLAB_G * S5_STATE

    @pl.when(pl.program_id(0) == 0)
    def _():
        st_ref[...] = jnp.zeros_like(st_ref)

    for j in range(nslab):
        uj = u_ref[:, :, j * LANE:(j + 1) * LANE].reshape(tb * bsz, LANE).astype(BF16)
        bu_scr[...] = _dot(uj, bb_ref[j])
        lre = jnp.broadcast_to(lam_ref[j, 0:1, :], (bsz, half))
        lim = jnp.broadcast_to(lam_ref[j, 1:2, :], (bsz, half))

        def step(i, carry):
            xre, xim = carry
            t = (tb - 1 - i) if rev else i
            rows = pl.ds(pl.multiple_of(t * bsz, bsz), bsz)
            nre = lre * xre - lim * xim + bu_scr[rows, 0:half]
            nim = lre * xim + lim * xre + bu_scr[rows, half:2 * half]
            bu_scr[rows, 0:half] = nre
            bu_scr[rows, half:2 * half] = nim
            return nre, nim

        xre, xim = lax.fori_loop(0, tb, step, (st_ref[j, :, 0:half], st_ref[j, :, half:2 * half]), unroll=4)
        st_ref[j, :, 0:half] = xre
        st_ref[j, :, half:2 * half] = xim
        y = _dot(bu_scr[...].astype(BF16), cc_ref[j])
        o_ref[:, :, j * LANE:(j + 1) * LANE] = y.reshape(tb, bsz, LANE)


def _s5(u_tm, bbig, lam, cbig, rev, ctx_len):
    l, bsz, dc = u_tm.shape
    nslab = dc // LANE
    tb = 128
    nblk, nctx = l // tb, ctx_len // tb
    blk = functools.partial(_scan_block, rev=rev, nctx=nctx, nblk=nblk)
    half = S5_SLAB_G * S5_STATE
    return pl.pallas_call(
        functools.partial(_s5_kernel, rev=rev, tb=tb, bsz=bsz, nslab=nslab),
        out_shape=jax.ShapeDtypeStruct((l, bsz, dc), F32),
        grid=(nblk,),
        in_specs=[
            pl.BlockSpec((tb, bsz, dc), lambda p_: (blk(p_), 0, 0)),
            _const_spec((nslab, LANE, 2 * half)),
            _const_spec((nslab, 2, half)),
            _const_spec((nslab, 2 * half, LANE)),
        ],
        out_specs=pl.BlockSpec((tb, bsz, dc), lambda p_: (blk(p_), 0, 0)),
        scratch_shapes=[pltpu.VMEM((tb * bsz, 2 * half), F32), pltpu.VMEM((nslab, bsz, 2 * half), F32)],
        compiler_params=_cp(("arbitrary",)),
        name="s5_bwd" if rev else "s5_fwd",
    )(u_tm, bbig, lam, cbig)


def _s5_params(lam_re, lam_im, log_step, b_re, b_im, c_re, c_im):
    g_c, p_s = lam_re.shape
    nslab = g_c // S5_SLAB_G
    lam = lax.complex(lam_re.astype(F32), lam_im.astype(F32))
    lam_dt = lam * jnp.exp(log_step.astype(F32))[:, None]
    lam_bar = jnp.exp(lam_dt)
    b_bar = ((lam_bar - 1.0) / lam)[:, :, None] * lax.complex(b_re.astype(F32), b_im.astype(F32))
    eye = jnp.eye(S5_SLAB_G, dtype=F32)

    def in_map(m):
        m = m.reshape(nslab, S5_SLAB_G, p_s, S5_IN)
        return jnp.einsum('jgpd,gh->jgdhp', m, eye).reshape(nslab, S5_SLAB_G * S5_IN, S5_SLAB_G * p_s)

    def out_map(m):
        m = m.reshape(nslab, S5_SLAB_G, S5_IN, p_s)
        return jnp.einsum('jgdp,gh->jgphd', m, eye).reshape(nslab, S5_SLAB_G * p_s, S5_SLAB_G * S5_IN)

    bbig = jnp.concatenate([in_map(jnp.real(b_bar)), in_map(jnp.imag(b_bar))], axis=-1).astype(BF16)
    cbig = jnp.concatenate([out_map(c_re.astype(F32)), out_map(-c_im.astype(F32))], axis=1).astype(BF16)
    lam_s = jnp.stack([jnp.real(lam_bar).reshape(nslab, -1), jnp.imag(lam_bar).reshape(nslab, -1)], axis=1)
    return bbig, lam_s, cbig


def _merge_a_kernel(ya_ref, of_ref, ob_ref, bg_ref, yf_ref, yb_ref, cu_ref, ga_ref, gb_ref, gc_ref,
                    hng_ref, dsk_ref, wglu_ref, wa_ref, wb_ref, wc_ref, o_ref):
    dc = dsk_ref.shape[-1]
    o = of_ref[...] + ob_ref[...]
    parts = []
    for hh in range(H_B):
        oh = o[:, hh * LANE:(hh + 1) * LANE]
        parts.append(oh * lax.rsqrt(jnp.mean(oh * oh, axis=-1, keepdims=True) + EPS))
    bg = bg_ref[...]
    yb = jnp.concatenate(parts, axis=-1) * hng_ref[...] * (bg * _sigmoid(bg))
    y = yf_ref[...] + yb_ref[...] + dsk_ref[...] * cu_ref[:, 0:dc]
    z = _dot(_gelu(y).astype(BF16), wglu_ref[...])
    yc = z[:, 0:dc] * _sigmoid(z[:, dc:2 * dc])
    merged = (_sigmoid(ga_ref[...]) * _dot(ya_ref[...].astype(BF16), wa_ref[...])
              + _sigmoid(gb_ref[...]) * _dot(yb.astype(BF16), wb_ref[...])
              + _sigmoid(gc_ref[...]) * _dot(yc.astype(BF16), wc_ref[...]))
    o_ref[...] = merged.astype(BF16)


def _merge_a(ya, o_f, o_b, p, y_f, y_b, hn_g, d_skip, w_glu, wa, wb, wc, off):
    bsz, l, da = ya.shape
    dc = y_f.shape[-1]
    d = wa.shape[-1]
    tm = 128
    o2 = off * (TM // tm)
    nblk = l // tm - o2
    tok = lambda b, j: (b, j + o2, 0)
    pcol = lambda c: (lambda b, j: (b, j + o2, c))
    return pl.pallas_call(
        _merge_a_kernel,
        out_shape=jax.ShapeDtypeStruct((bsz, nblk * tm, d), BF16),
        grid=(bsz, nblk),
        in_specs=[
            pl.BlockSpec((None, tm, da), tok),
            pl.BlockSpec((None, tm, da), tok),
            pl.BlockSpec((None, tm, da), tok),
            pl.BlockSpec((None, tm, da), pcol(6)),
            pl.BlockSpec((None, tm, dc), tok),
            pl.BlockSpec((None, tm, dc), tok),
            pl.BlockSpec((None, tm, da), pcol(7)),
            pl.BlockSpec((None, tm, d), pcol(4)),
            pl.BlockSpec((None, tm, d), pcol(5)),
            pl.BlockSpec((None, tm, d), pcol(6)),
            _const_spec((1, da)),
            _const_spec((1, dc)),
            _const_spec(w_glu.shape),
            _const_spec(wa.shape),
            _const_spec(wb.shape),
            _const_spec(wc.shape),
        ],
        out_specs=pl.BlockSpec((None, tm, d), lambda b, j: (b, j, 0)),
        compiler_params=_cp(("parallel", "parallel")),
        name="merge_a",
    )(ya, o_f, o_b, p, y_f, y_b, p, p, p, p, hn_g, d_skip, w_glu, wa, wb, wc)


def _merge_b_kernel(m_ref, x_ref, mod_ref, wo_ref, wq_ref, x1_ref, h2_ref, q_ref):
    x1 = x_ref[...] + mod_ref[2:3, :] * _dot(m_ref[...], wo_ref[...])
    x1_ref[...] = x1
    r = lax.rsqrt(jnp.mean(x1 * x1, axis=-1, keepdims=True) + EPS)
    h2 = x1 * r * (1.0 + mod_ref[4:5, :]) + mod_ref[3:4, :]
    h2_ref[...] = h2
    q_ref[...] = _dot(h2.astype(BF16), wq_ref[...])


def _merge_b(merged, xall, mod, wo, wq, off, nctx):
    bsz, lo, d = merged.shape
    nblk = lo // TM
    dq = wq.shape[-1]
    out = lambda b, j: (b, j, 0)
    return pl.pallas_call(
        _merge_b_kernel,
        out_shape=(jax.ShapeDtypeStruct((bsz, lo, d), F32), jax.ShapeDtypeStruct((bsz, lo, d), F32),
                   jax.ShapeDtypeStruct((bsz, lo, dq), F32)),
        grid=(bsz, nblk),
        in_specs=[
            pl.BlockSpec((None, TM, d), out),
            pl.BlockSpec((None, TM, d), lambda b, j: (b, j + off, 0)),
            pl.BlockSpec((None, None, N_MOD, d), lambda b, j: (b, (j + off >= nctx).astype(jnp.int32), 0, 0)),
            _const_spec(wo.shape),
            _const_spec(wq.shape),
        ],
        out_specs=(pl.BlockSpec((None, TM, d), out), pl.BlockSpec((None, TM, d), out),
                   pl.BlockSpec((None, TM, dq), out)),
        compiler_params=_cp(("parallel", "parallel")),
        name="merge_b",
    )(merged, xall, mod, wo, wq)


def _topk_rows(s, row_id, k):
    slot = lax.broadcasted_iota(jnp.int32, (k, s.shape[1]), 0)
    vals = jnp.zeros((k, s.shape[1]), F32)
    ids = jnp.zeros((k, s.shape[1]), F32)
    for r in range(k):
        m = jnp.max(s, axis=0, keepdims=True)
        am = jnp.min(jnp.where(s == m, row_id, 1e9), axis=0, keepdims=True)
        vals = jnp.where(slot == r, m, vals)
        ids = jnp.where(slot == r, am, ids)
        s = jnp.where(row_id == am, -jnp.inf, s)
    return vals, ids


def _pair_candidates():
    k = PEER_TOPK
    pieces = [("a", 0, 0, 16), ("a", 1, 0, 8), ("a", 2, 0, 8), ("a", 3, 0, 8), ("a", 4, 0, 8),
              ("b", 0, 0, 8), ("b", 1, 0, 8), ("b", 0, 8, 16)]
    ids, bias, seen = [], [], set()
    for kind, fixed, lo, hi in pieces:
        for r in range(lo, hi):
            a, b = (fixed, r) if kind == "a" else (r, fixed)
            ok = (a + 1) * (b + 1) <= k and (a, b) not in seen
            if ok:
                seen.add((a, b))
            ids.append(float(a * k + b))
            bias.append(0.0 if ok else -math.inf)
    assert len(seen) == sum(k // (a + 1) for a in range(k))
    return pieces, ids, bias


def _route_kernel(q_ref, keys_ref, cid_ref, cbias_ref, idx_ref, g_ref):
    k = PEER_TOPK
    pieces, _, _ = _pair_candidates()
    key_id = lax.broadcasted_iota(jnp.int32, (PEER_NKEYS, q_ref.shape[0]), 0).astype(F32)
    for h in range(PEER_HEADS):
        sv, si = [], []
        for half in range(2):
            c0 = (h * 2 + half) * PEER_NKEYS
            qh = q_ref[:, c0:c0 + PEER_NKEYS].astype(BF16)
            s = _dot_nt(keys_ref[h, half], qh)
            v_, i_ = _topk_rows(s, key_id, k)
            sv.append(v_)
            si.append(i_)
        cand = jnp.concatenate(
            [sv[0][f:f + 1, :] + sv[1][lo:hi, :] if kind == "a" else sv[0][lo:hi, :] + sv[1][f:f + 1, :]
             for kind, f, lo, hi in pieces], axis=0) + cbias_ref[...]
        fv, fi = _topk_rows(cand, cid_ref[...], k)
        fa = jnp.floor(fi * (1.0 / k))
        fb = fi - fa * k
        i1 = jnp.zeros_like(fi)
        i2 = jnp.zeros_like(fi)
        for a in range(k):
            i1 = jnp.where(fa == a, si[0][a:a + 1, :], i1)
            i2 = jnp.where(fb == a, si[1][a:a + 1, :], i2)
        e = jnp.exp(fv - fv[0:1, :])
        g_ref[h * k:(h + 1) * k, :] = e / jnp.sum(e, axis=0, keepdims=True)
        idx_ref[h * k:(h + 1) * k, :] = (i1 * PEER_NKEYS + i2).astype(jnp.int32)


def _route(q, keys):
    t, dq = q.shape
    tm = LANE
    nsel = PEER_HEADS * PEER_TOPK
    _, ids, bias = _pair_candidates()
    cid = jnp.broadcast_to(jnp.asarray(ids, F32)[:, None], (len(ids), tm))
    cbias = jnp.broadcast_to(jnp.asarray(bias, F32)[:, None], (len(bias), tm))
    return pl.pallas_call(
        _route_kernel,
        out_shape=(jax.ShapeDtypeStruct((nsel, t), jnp.int32), jax.ShapeDtypeStruct((nsel, t), F32)),
        grid=(t // tm,),
        in_specs=[pl.BlockSpec((tm, dq), lambda i: (i, 0)), _const_spec(keys.shape),
                  _const_spec(cid.shape), _const_spec(cbias.shape)],
        out_specs=(pl.BlockSpec((nsel, tm), lambda i: (0, i)), pl.BlockSpec((nsel, tm), lambda i: (0, i))),
        compiler_params=_cp(("parallel",)),
        name="peer_route",
    )(q, keys, cid, cbias)


PEER_TB = 32
PEER_SLOTS = 4
PEER_AHEAD = PEER_SLOTS - 1
HALF_SUB = 8
HI_MASK = -65536


def _pack_expert_table(u, v):
    n_exp, d = u.shape

    def pack(x):
        bits = lax.bitcast_convert_type(x.astype(BF16), jnp.uint16).astype(jnp.uint32)
        word = (bits[:, :d // 2] << 16) | bits[:, d // 2:]
        return lax.bitcast_convert_type(word, jnp.int32).reshape(n_exp, HALF_SUB, LANE)

    return jnp.concatenate([pack(u), pack(v)], axis=1)


def _unpack_pair(word):
    return lax.bitcast_convert_type(word & HI_MASK, F32), lax.bitcast_convert_type(word << 16, F32)


def _experts_kernel(idx_ref, g_ref, h_ref, x_ref, mod_ref, tab_ref, o_ref, *scratch):
    bufs, sem = scratch[:PEER_SLOTS], scratch[PEER_SLOTS]
    nsel = PEER_HEADS * PEER_TOPK
    per = nsel // (2 * HALF_SUB)
    step = pl.program_id(0)

    def issue(t, slot, j0=0, j1=nsel):
        for j in range(j0, j1):
            e = idx_ref[0, t * nsel + j]
            pltpu.make_async_copy(tab_ref.at[e], bufs[slot].at[:, j, :], sem.at[slot]).start()

    def wait(slot):
        pltpu.make_async_copy(bufs[slot], bufs[slot], sem.at[slot]).wait()

    @pl.when(step == 0)
    def _():
        for s in range(PEER_AHEAD):
            issue(s, s)

    eye = lax.broadcasted_iota(jnp.int32, (nsel, nsel), 0) == lax.broadcasted_iota(jnp.int32, (nsel, nsel), 1)

    def group(t4, c):
        for s in range(PEER_SLOTS):
            t = t4 * PEER_SLOTS + s
            wait(s)
            nxt = (s + PEER_AHEAD) % PEER_SLOTS
            row = pl.ds(t, 1)
            hrow = h_ref[row, :]
            buf = bufs[s]
            acc = None
            for cc in range(HALF_SUB):
                u_hi, u_lo = _unpack_pair(buf[cc])
                term = (u_hi * hrow[:, cc * LANE:(cc + 1) * LANE]
                        + u_lo * hrow[:, (HALF_SUB + cc) * LANE:(HALF_SUB + cc + 1) * LANE])
                acc = term if acc is None else acc + term
                issue(t + PEER_AHEAD, nxt, cc * per, (cc + 1) * per)
            act = jnp.sum(acc, axis=-1, keepdims=True)
            gcol = jnp.sum(jnp.where(eye, g_ref[row, :], 0.0), axis=-1, keepdims=True)
            w = _gelu(act) * gcol
            outs = [None] * (2 * HALF_SUB)
            for cc in range(HALF_SUB):
                v_hi, v_lo = _unpack_pair(buf[HALF_SUB + cc])
                outs[cc] = jnp.sum(w * v_hi, axis=0, keepdims=True)
                outs[HALF_SUB + cc] = jnp.sum(w * v_lo, axis=0, keepdims=True)
                issue(t + PEER_AHEAD, nxt, (HALF_SUB + cc) * per, (HALF_SUB + cc + 1) * per)
            o_ref[row, :] = x_ref[row, :] + mod_ref[5:6, :] * jnp.concatenate(outs, axis=-1)
        return c

    lax.fori_loop(0, PEER_TB // PEER_SLOTS, group, 0)

    @pl.when(step == pl.num_programs(0) - 1)
    def _():
        for s in range(PEER_AHEAD):
            wait(s)


def _experts(idx, g, h2, x1, mod, tab, off, nctx):
    t_all, d = h2.shape
    bsz = mod.shape[0]
    nsel = idx.shape[-1]
    tb = PEER_TB
    nb = t_all // tb
    per_b = nb // bsz
    per = TM // tb
    assert d == 2 * HALF_SUB * LANE and tb % PEER_SLOTS == 0
    main = idx.reshape(nb, tb * nsel)
    idx_ext = jnp.concatenate([main, jnp.roll(main, -1, axis=0)[:, :PEER_AHEAD * nsel]], axis=1)
    idx_ext = idx_ext.reshape(nb, 1, (tb + PEER_AHEAD) * nsel)
    tokb = lambda i: (i, 0)

    def mod_map(i):
        return (i // per_b, ((i % per_b) // per + off >= nctx).astype(jnp.int32), 0, 0)

    return pl.pallas_call(
        _experts_kernel,
        out_shape=jax.ShapeDtypeStruct((t_all, d), F32),
        grid=(nb,),
        in_specs=[
            pl.BlockSpec((None, 1, (tb + PEER_AHEAD) * nsel), lambda i: (i, 0, 0), memory_space=pltpu.SMEM),
            pl.BlockSpec((tb, nsel), tokb),
            pl.BlockSpec((tb, d), tokb),
            pl.BlockSpec((tb, d), tokb),
            pl.BlockSpec((None, None, N_MOD, d), mod_map),
            pl.BlockSpec(memory_space=pl.ANY),
        ],
        out_specs=pl.BlockSpec((tb, d), tokb),
        scratch_shapes=[pltpu.VMEM((2 * HALF_SUB, nsel, LANE), jnp.int32) for _ in range(PEER_SLOTS)]
        + [pltpu.SemaphoreType.DMA((PEER_SLOTS,))],
        compiler_params=_cp(("arbitrary",)),
        name="peer_experts",
    )(idx_ext, g, h2, x1, mod, tab)


def _final_kernel(x_ref, g_ref, o_ref):
    x = x_ref[...]
    o_ref[...] = x * lax.rsqrt(jnp.mean(x * x, axis=-1, keepdims=True) + EPS) * g_ref[...]


def _final_norm(x, g):
    bsz, n, d = x.shape
    return pl.pallas_call(
        _final_kernel,
        out_shape=jax.ShapeDtypeStruct((bsz, n, d), F32),
        grid=(bsz, n // TM),
        in_specs=[pl.BlockSpec((None, TM, d), lambda b, j: (b, j, 0)), _const_spec((1, d))],
        out_specs=pl.BlockSpec((None, TM, d), lambda b, j: (b, j, 0)),
        compiler_params=_cp(("parallel", "parallel")),
        name="final_norm",
    )(x, g.reshape(1, d))


def _pos_embed_2d(n, d):
    n_rows = n // GRID_W
    quarter = d // 4
    omega = 1.0 / (POS_THETA ** (jnp.arange(quarter, dtype=F32) / quarter))

    def axis_code(length):
        ang = jnp.arange(length, dtype=F32)[:, None] * omega[None]
        return jnp.concatenate([jnp.sin(ang), jnp.cos(ang)], axis=-1)

    er = jnp.broadcast_to(axis_code(n_rows)[:, None], (n_rows, GRID_W, d // 2))
    ec = jnp.broadcast_to(axis_code(GRID_W)[None], (n_rows, GRID_W, d // 2))
    return jnp.concatenate([er, ec], axis=-1).reshape(n, d)


def kernel(x, c, ctx, c_ctx, w_mod, b_mod, w_in, gmlp_ln_g, gmlp_ln_b, gmlp_w_s, gmlp_b_s, hgrn_lb_logits, hgrn_norm_g, s5_lam_re, s5_lam_im, s5_log_step, s5_b_re, s5_b_im, s5_c_re, s5_c_im, s5_d, s5_w_glu, w_branch_a, w_branch_b, w_branch_c, w_out, peer_w_q, peer_sub_keys, peer_u, peer_v, final_norm_g):
    bsz, n, d = x.shape
    ctx_len = ctx.shape[1]
    depth = w_mod.shape[0]
    nctx = ctx_len // TM
    da = gmlp_ln_g.shape[-1]
    db = hgrn_norm_g.shape[-1]
    dc = s5_d.shape[-1]
    assert n % TM == 0 and ctx_len % TM == 0 and bsz % 8 == 0

    xall = _embed(x, ctx, _pos_embed_2d(n, d).astype(x.dtype))

    rows = -(-(bsz + 1) // 8) * 8
    cc = jnp.zeros((rows, d), F32).at[0].set(c_ctx).at[1:bsz + 1].set(c)
    mod_all = _modulation(cc, w_mod, b_mod)

    gamma = jax.nn.softmax(hgrn_lb_logits.astype(F32), axis=0)
    lower_bounds = jnp.cumsum(gamma, axis=0) - gamma[0:1]

    for l in range(depth):
        last = l == depth - 1
        off = nctx if last else 0
        mod_c = jnp.broadcast_to(mod_all[l, 0][None], (bsz, N_MOD * d))
        mod = jnp.stack([mod_c, mod_all[l, 1:bsz + 1]], axis=1).reshape(bsz, 2, N_MOD, d)

        cu0 = 2 * da + 5 * db
        w_pad = jnp.concatenate([w_in[l][:, :cu0 + dc], jnp.zeros((d, da - dc), F32), w_in[l][:, cu0 + dc:]],
                                axis=1).astype(BF16)
        p = _in_proj(xall, mod, w_pad, ctx_len)

        bias = jnp.repeat(gmlp_b_s[l].T, da // G_A, axis=1)
        ya = _gmlp(p, gmlp_ln_g[l].reshape(1, da), gmlp_ln_b[l].reshape(1, da), gmlp_w_s[l].astype(BF16), bias)

        o_f = _hgrn(p, lower_bounds[l, 0].reshape(1, db), False, nctx)
        o_b = _hgrn(p, lower_bounds[l, 1].reshape(1, db), True, nctx)

        u_tm = jnp.transpose(p[:, :, cu0:cu0 + dc], (1, 0, 2))
        ys = []
        for r in range(2):
            prm = _s5_params(s5_lam_re[l, r], s5_lam_im[l, r], s5_log_step[l, r], s5_b_re[l, r], s5_b_im[l, r],
                             s5_c_re[l, r], s5_c_im[l, r])
            ys.append(jnp.transpose(_s5(u_tm, *prm, rev=bool(r), ctx_len=ctx_len), (1, 0, 2)))

        merged = _merge_a(ya, o_f, o_b, p, ys[0], ys[1], hgrn_norm_g[l].reshape(1, db), s5_d[l].reshape(1, dc),
                          s5_w_glu[l].astype(BF16), w_branch_a[l].astype(BF16), w_branch_b[l].astype(BF16),
                          w_branch_c[l].astype(BF16), off)
        x1, h2, q = _merge_b(merged, xall, mod, w_out[l].astype(BF16), peer_w_q[l].astype(BF16), off, nctx)

        lo = x1.shape[1]
        idx_t, g_t = _route(q.reshape(bsz * lo, -1), peer_sub_keys[l].astype(BF16))
        tab = _pack_expert_table(peer_u[l], peer_v[l])
        xall = _experts(idx_t.T, g_t.T, h2.reshape(bsz * lo, d), x1.reshape(bsz * lo, d), mod, tab, off, nctx)
        xall = xall.reshape(bsz, lo, d)

    return _final_norm(xall, final_norm_g.astype(F32))
```

```python
import functools
import math

import jax
import jax.numpy as jnp
from jax import lax
from jax.experimental import pallas as pl
from jax.experimental.pallas import tpu as pltpu

F32 = jnp.float32
BF16 = jnp.bfloat16

EPS = 1e-6
GRID_W = 64
POS_THETA = 10000.0
N_MOD = 6
G_A = 8
GMLP_CHUNK = 128
H_B = 8
HGRN_CHUNK = 32
S5_IN = 16
S5_STATE = 64
S5_SLAB_G = 8
PEER_HEADS = 8
PEER_NKEYS = 128
PEER_TOPK = 16

TM = 256
LANE = 128
SUB = 8
VMEM_LIMIT = 56 * 1024 * 1024


def _cp(sem, vmem=VMEM_LIMIT):
    return pltpu.CompilerParams(dimension_semantics=sem, vmem_limit_bytes=vmem)


def _const_spec(shape):
    nd = len(shape)
    return pl.BlockSpec(shape, lambda *_: (0,) * nd, pipeline_mode=pl.Buffered(1))


def _gelu(x):
    return x * (lax.erf(x * (1.0 / math.sqrt(2.0))) + 1.0) * 0.5


def _sigmoid(x):
    return jax.nn.sigmoid(x)


def _dot(a, b):
    return jnp.dot(a, b, preferred_element_type=F32)


def _dot_nt(a, b):
    return lax.dot_general(a, b, (((1,), (1,)), ((), ())), preferred_element_type=F32)


def _dot_tn(a, b):
    return lax.dot_general(a, b, (((0,), (0,)), ((), ())), preferred_element_type=F32)


def _embed_kernel(ctx_ref, x_ref, pos_ref, o_ref, *, nctx):
    j = pl.program_id(1)

    @pl.when(j < nctx)
    def _():
        o_ref[...] = ctx_ref[...]

    @pl.when(j >= nctx)
    def _():
        o_ref[...] = x_ref[...] + pos_ref[...]


def _embed(x, ctx, pos):
    bsz, n, d = x.shape
    nctx = ctx.shape[1] // TM
    nblk = nctx + n // TM
    return pl.pallas_call(
        functools.partial(_embed_kernel, nctx=nctx),
        out_shape=jax.ShapeDtypeStruct((bsz, nblk * TM, d), x.dtype),
        grid=(bsz, nblk),
        in_specs=[
            pl.BlockSpec((None, TM, d), lambda b, j: (b, jnp.minimum(j, nctx - 1), 0)),
            pl.BlockSpec((None, TM, d), lambda b, j: (b, jnp.maximum(j - nctx, 0), 0)),
            pl.BlockSpec((TM, d), lambda b, j: (jnp.maximum(j - nctx, 0), 0)),
        ],
        out_specs=pl.BlockSpec((None, TM, d), lambda b, j: (b, j, 0)),
        compiler_params=_cp(("parallel", "arbitrary")),
        name="embed",
    )(ctx, x, pos)


def _mod_kernel(c_ref, w_ref, b_ref, o_ref):
    c = c_ref[...]
    s = c * _sigmoid(c)
    o_ref[...] = _dot(s.astype(BF16), w_ref[...].astype(BF16)) + b_ref[...]


def _modulation(cc, w_mod, b_mod):
    depth, d, nd = w_mod.shape
    rows = cc.shape[0]
    tn = 1024
    return pl.pallas_call(
        _mod_kernel,
        out_shape=jax.ShapeDtypeStruct((depth, rows, nd), F32),
        grid=(depth, nd // tn),
        in_specs=[
            pl.BlockSpec((rows, d), lambda l, n: (0, 0)),
            pl.BlockSpec((None, d, tn), lambda l, n: (l, 0, n)),
            pl.BlockSpec((None, 1, tn), lambda l, n: (l, 0, n)),
        ],
        out_specs=pl.BlockSpec((None, rows, tn), lambda l, n: (l, 0, n)),
        compiler_params=_cp(("parallel", "arbitrary")),
        name="modulation",
    )(cc, w_mod, b_mod.reshape(depth, 1, nd))


def _inproj_kernel(x_ref, mod_ref, w_ref, o_ref, h_scr, *, tm, ctx_rows):
    j = pl.program_id(1)

    @pl.when(pl.program_id(2) == 0)
    def _():
        x = x_ref[...]
        r = lax.rsqrt(jnp.mean(x * x, axis=-1, keepdims=True) + EPS)
        row = j * tm + lax.broadcasted_iota(jnp.int32, (tm, 1), 0)
        is_ctx = row < ctx_rows
        sh = jnp.where(is_ctx, mod_ref[0, 0:1, :], mod_ref[1, 0:1, :])
        sc = jnp.where(is_ctx, mod_ref[0, 1:2, :], mod_ref[1, 1:2, :])
        h_scr[...] = (x * r * (1.0 + sc) + sh).astype(BF16)

    o_ref[...] = _dot(h_scr[...], w_ref[...])


def _in_proj(xall, mod, w_pad, ctx_rows):
    bsz, l, d = xall.shape
    ncol = w_pad.shape[1]
    tm = 768 if l % 768 == 0 else TM
    tn = 1024
    return pl.pallas_call(
        functools.partial(_inproj_kernel, tm=tm, ctx_rows=ctx_rows),
        out_shape=jax.ShapeDtypeStruct((bsz, l, ncol), F32),
        grid=(bsz, l // tm, ncol // tn),
        in_specs=[
            pl.BlockSpec((None, tm, d), lambda b, j, n: (b, j, 0)),
            pl.BlockSpec((None, 2, N_MOD, d), lambda b, j, n: (b, 0, 0, 0)),
            pl.BlockSpec((d, tn), lambda b, j, n: (0, n)),
        ],
        out_specs=pl.BlockSpec((None, tm, tn), lambda b, j, n: (b, j, n)),
        scratch_shapes=[pltpu.VMEM((tm, d), BF16)],
        compiler_params=_cp(("parallel", "parallel", "arbitrary")),
        name="in_proj",
    )(xall, mod, w_pad)


def _gmlp_kernel(u_ref, v_ref, lng_ref, lnb_ref, ws_ref, bias_ref, o_ref):
    v = _gelu(v_ref[...])
    mu = jnp.mean(v, axis=-1, keepdims=True)
    vc = v - mu
    var = jnp.mean(vc * vc, axis=-1, keepdims=True)
    v = (vc * lax.rsqrt(var + EPS) * lng_ref[...] + lnb_ref[...]).astype(BF16)
    for c in range(TM // GMLP_CHUNK):
        rows = slice(c * GMLP_CHUNK, (c + 1) * GMLP_CHUNK)
        for g in range(G_A):
            cols = slice(g * LANE, (g + 1) * LANE)
            mixed = _dot(ws_ref[g], v[rows, cols]) + bias_ref[:, cols]
            o_ref[rows, cols] = _gelu(u_ref[rows, cols]) * mixed


def _gmlp(p, ln_g, ln_b, w_s, bias):
    bsz, l, _ = p.shape
    da = ln_g.shape[-1]
    return pl.pallas_call(
        _gmlp_kernel,
        out_shape=jax.ShapeDtypeStruct((bsz, l, da), F32),
        grid=(bsz, l // TM),
        in_specs=[
            pl.BlockSpec((None, TM, da), lambda b, j: (b, j, 0)),
            pl.BlockSpec((None, TM, da), lambda b, j: (b, j, 1)),
            _const_spec((1, da)),
            _const_spec((1, da)),
            _const_spec((G_A, GMLP_CHUNK, GMLP_CHUNK)),
            _const_spec((GMLP_CHUNK, da)),
        ],
        out_specs=pl.BlockSpec((None, TM, da), lambda b, j: (b, j, 0)),
        compiler_params=_cp(("parallel", "parallel")),
        name="gmlp",
    )(p, p, ln_g, ln_b, w_s, bias)


def _scan_block(p, rev, nctx, nblk):
    if not rev:
        return p
    return jnp.where(p < nctx, nctx - 1 - p, nblk - 1 - (p - nctx))


def _hgrn_kernel(q_ref, f_ref, i_ref, lb_ref, o_ref, st_ref, *, rev):
    c_sz = HGRN_CHUNK
    nchunk = TM // c_sz

    @pl.when(pl.program_id(1) == 0)
    def _():
        st_ref[...] = jnp.zeros_like(st_ref)

    nsb = c_sz // SUB
    rowi = lax.broadcasted_iota(jnp.int32, (c_sz, LANE), 0)
    row8 = lax.broadcasted_iota(jnp.int32, (SUB, LANE), 0)
    rowa = lax.broadcasted_iota(jnp.int32, (c_sz, c_sz), 0)

    def chunk_body(ci, carry):
        c = (nchunk - 1 - ci) if rev else ci
        rows = pl.ds(pl.multiple_of(c * c_sz, c_sz), c_sz)
        for hh in range(H_B):
            cols = slice(hh * LANE, (hh + 1) * LANE)
            lbh = lb_ref[:, cols]
            st = st_ref[hh]
            qr = q_ref[rows, cols]
            q = qr * _sigmoid(qr)
            fg = lbh + (1.0 - lbh) * _sigmoid(f_ref[rows, cols])
            k = 1.0 - fg
            v = i_ref[rows, cols]
            b = jnp.log(fg)
            for s in (1, 2, 4, 8, 16):
                if rev:
                    b = b + jnp.where(rowi < c_sz - s, pltpu.roll(b, c_sz - s, 0), 0.0)
                else:
                    b = b + jnp.where(rowi >= s, pltpu.roll(b, s, 0), 0.0)
            b_last = b[0:1, :] if rev else b[c_sz - 1:c_sz, :]
            o = _dot_nt((q * jnp.exp(b)).astype(BF16), st.astype(BF16))
            diag = []
            for jb in range(nsb):
                r8 = slice(jb * SUB, (jb + 1) * SUB)
                bj, qj, kj, vj = b[r8], q[r8], k[r8], v[r8]
                oj = jnp.zeros((SUB, LANE), F32)
                for s in range(SUB):
                    valid = (row8 <= s) if rev else (row8 >= s)
                    dec = jnp.exp(jnp.where(valid, bj - bj[s:s + 1, :], 0.0))
                    pr = jnp.where(valid, qj * (kj[s:s + 1, :] * dec), 0.0)
                    oj = oj + jnp.sum(pr, axis=-1, keepdims=True) * vj[s:s + 1, :]
                diag.append(oj)
            o = o + jnp.concatenate(diag, axis=0)
            att = jnp.zeros((c_sz, c_sz), F32)
            for jb in (range(1, nsb) if rev else range(nsb - 1)):
                edge = jb * SUB if rev else jb * SUB + SUB - 1
                b_ref_row = b[edge:edge + 1, :]
                in_blk = (rowi >= jb * SUB) & (rowi < (jb + 1) * SUB)
                kt = jnp.where(in_blk, k * jnp.exp(jnp.minimum(b_ref_row - b, 0.0)), 0.0)
                qq = q * jnp.exp(jnp.minimum(b - b_ref_row, 0.0))
                a = _dot_nt(qq.astype(BF16), kt.astype(BF16))
                t_ok = (rowa < jb * SUB) if rev else (rowa >= (jb + 1) * SUB)
                att = att + jnp.where(t_ok, a, 0.0)
            o = o + _dot(att.astype(BF16), v.astype(BF16))
            o_ref[rows, cols] = o
            kk = k * jnp.exp(b_last - b)
            st_ref[hh] = st * jnp.exp(b_last) + _dot_tn(v.astype(BF16), kk.astype(BF16))
        return carry

    lax.fori_loop(0, nchunk, chunk_body, 0)


def _hgrn(p, lb, rev, nctx):
    bsz, l, _ = p.shape
    db = lb.shape[-1]
    nblk = l // TM
    blk = functools.partial(_scan_block, rev=rev, nctx=nctx, nblk=nblk)
    fcol = 4 if rev else 3
    return pl.pallas_call(
        functools.partial(_hgrn_kernel, rev=rev),
        out_shape=jax.ShapeDtypeStruct((bsz, l, db), F32),
        grid=(bsz, nblk),
        in_specs=[
            pl.BlockSpec((None, TM, db), lambda b, p_: (b, blk(p_), 2)),
            pl.BlockSpec((None, TM, db), lambda b, p_: (b, blk(p_), fcol)),
            pl.BlockSpec((None, TM, db), lambda b, p_: (b, blk(p_), 5)),
            _const_spec((1, db)),
        ],
        out_specs=pl.BlockSpec((None, TM, db), lambda b, p_: (b, blk(p_), 0)),
        scratch_shapes=[pltpu.VMEM((H_B, LANE, LANE), F32)],
        compiler_params=_cp(("parallel", "arbitrary")),
        name="hgrn_bwd" if rev else "hgrn_fwd",
    )(p, p, p, lb)


def _s5_kernel(u_ref, bb_ref, lam_ref, cc_ref, o_ref, bu_scr, st_ref, *, rev, tb, bsz, nslab):
    half = S5_SLAB_G * S5_STATE

    @pl.when(pl.program_id(0) == 0)
    def _():
        st_ref[...] = jnp.zeros_like(st_ref)

    for j in range(nslab):
        uj = u_ref[:, :, j * LANE:(j + 1) * LANE].reshape(tb * bsz, LANE).astype(BF16)
        bu_scr[...] = _dot(uj, bb_ref[j])
        lre = jnp.broadcast_to(lam_ref[j, 0:1, :], (bsz, half))
        lim = jnp.broadcast_to(lam_ref[j, 1:2, :], (bsz, half))

        def step(i, carry):
            xre, xim = carry
            t = (tb - 1 - i) if rev else i
            rows = pl.ds(pl.multiple_of(t * bsz, bsz), bsz)
            nre = lre * xre - lim * xim + bu_scr[rows, 0:half]
            nim = lre * xim + lim * xre + bu_scr[rows, half:2 * half]
            bu_scr[rows, 0:half] = nre
            bu_scr[rows, half:2 * half] = nim
            return nre, nim

        xre, xim = lax.fori_loop(0, tb, step, (st_ref[j, :, 0:half], st_ref[j, :, half:2 * half]), unroll=4)
        st_ref[j, :, 0:half] = xre
        st_ref[j, :, half:2 * half] = xim
        y = _dot(bu_scr[...].astype(BF16), cc_ref[j])
        o_ref[:, :, j * LANE:(j + 1) * LANE] = y.reshape(tb, bsz, LANE)


def _s5(u_tm, bbig, lam, cbig, rev, ctx_len):
    l, bsz, dc = u_tm.shape
    nslab = dc // LANE
    tb = 128
    nblk, nctx = l // tb, ctx_len // tb
    blk = functools.partial(_scan_block, rev=rev, nctx=nctx, nblk=nblk)
    half = S5_SLAB_G * S5_STATE
    return pl.pallas_call(
        functools.partial(_s5_kernel, rev=rev, tb=tb, bsz=bsz, nslab=nslab),
        out_shape=jax.ShapeDtypeStruct((l, bsz, dc), F32),
        grid=(nblk,),
        in_specs=[
            pl.BlockSpec((tb, bsz, dc), lambda p_: (blk(p_), 0, 0)),
            _const_spec((nslab, LANE, 2 * half)),
            _const_spec((nslab, 2, half)),
            _const_spec((nslab, 2 * half, LANE)),
        ],
        out_specs=pl.BlockSpec((tb, bsz, dc), lambda p_: (blk(p_), 0, 0)),
        scratch_shapes=[pltpu.VMEM((tb * bsz, 2 * half), F32), pltpu.VMEM((nslab, bsz, 2 * half), F32)],
        compiler_params=_cp(("arbitrary",)),
        name="s5_bwd" if rev else "s5_fwd",
    )(u_tm, bbig, lam, cbig)


def _s5_params(lam_re, lam_im, log_step, b_re, b_im, c_re, c_im):
    g_c, p_s = lam_re.shape
    nslab = g_c // S5_SLAB_G
    lam = lax.complex(lam_re.astype(F32), lam_im.astype(F32))
    lam_dt = lam * jnp.exp(log_step.astype(F32))[:, None]
    lam_bar = jnp.exp(lam_dt)
    b_bar = ((lam_bar - 1.0) / lam)[:, :, None] * lax.complex(b_re.astype(F32), b_im.astype(F32))
    eye = jnp.eye(S5_SLAB_G, dtype=F32)

    def in_map(m):
        m = m.reshape(nslab, S5_SLAB_G, p_s, S5_IN)
        return jnp.einsum('jgpd,gh->jgdhp', m, eye).reshape(nslab, S5_SLAB_G * S5_IN, S5_SLAB_G * p_s)

    def out_map(m):
        m = m.reshape(nslab, S5_SLAB_G, S5_IN, p_s)
        return jnp.einsum('jgdp,gh->jgphd', m, eye).reshape(nslab, S5_SLAB_G * p_s, S5_SLAB_G * S5_IN)

    bbig = jnp.concatenate([in_map(jnp.real(b_bar)), in_map(jnp.imag(b_bar))], axis=-1).astype(BF16)
    cbig = jnp.concatenate([out_map(c_re.astype(F32)), out_map(-c_im.astype(F32))], axis=1).astype(BF16)
    lam_s = jnp.stack([jnp.real(lam_bar).reshape(nslab, -1), jnp.imag(lam_bar).reshape(nslab, -1)], axis=1)
    return bbig, lam_s, cbig


def _merge_a_kernel(ya_ref, of_ref, ob_ref, bg_ref, yf_ref, yb_ref, cu_ref, ga_ref, gb_ref, gc_ref,
                    hng_ref, dsk_ref, wglu_ref, wa_ref, wb_ref, wc_ref, o_ref):
    dc = dsk_ref.shape[-1]
    o = of_ref[...] + ob_ref[...]
    parts = []
    for hh in range(H_B):
        oh = o[:, hh * LANE:(hh + 1) * LANE]
        parts.append(oh * lax.rsqrt(jnp.mean(oh * oh, axis=-1, keepdims=True) + EPS))
    bg = bg_ref[...]
    yb = jnp.concatenate(parts, axis=-1) * hng_ref[...] * (bg * _sigmoid(bg))
    y = yf_ref[...] + yb_ref[...] + dsk_ref[...] * cu_ref[:, 0:dc]
    z = _dot(_gelu(y).astype(BF16), wglu_ref[...])
    yc = z[:, 0:dc] * _sigmoid(z[:, dc:2 * dc])
    merged = (_sigmoid(ga_ref[...]) * _dot(ya_ref[...].astype(BF16), wa_ref[...])
              + _sigmoid(gb_ref[...]) * _dot(yb.astype(BF16), wb_ref[...])
              + _sigmoid(gc_ref[...]) * _dot(yc.astype(BF16), wc_ref[...]))
    o_ref[...] = merged.astype(BF16)


def _merge_a(ya, o_f, o_b, p, y_f, y_b, hn_g, d_skip, w_glu, wa, wb, wc, off):
    bsz, l, da = ya.shape
    dc = y_f.shape[-1]
    d = wa.shape[-1]
    tm = 128
    o2 = off * (TM // tm)
    nblk = l // tm - o2
    tok = lambda b, j: (b, j + o2, 0)
    pcol = lambda c: (lambda b, j: (b, j + o2, c))
    return pl.pallas_call(
        _merge_a_kernel,
        out_shape=jax.ShapeDtypeStruct((bsz, nblk * tm, d), BF16),
        grid=(bsz, nblk),
        in_specs=[
            pl.BlockSpec((None, tm, da), tok),
            pl.BlockSpec((None, tm, da), tok),
            pl.BlockSpec((None, tm, da), tok),
            pl.BlockSpec((None, tm, da), pcol(6)),
            pl.BlockSpec((None, tm, dc), tok),
            pl.BlockSpec((None, tm, dc), tok),
            pl.BlockSpec((None, tm, da), pcol(7)),
            pl.BlockSpec((None, tm, d), pcol(4)),
            pl.BlockSpec((None, tm, d), pcol(5)),
            pl.BlockSpec((None, tm, d), pcol(6)),
            _const_spec((1, da)),
            _const_spec((1, dc)),
            _const_spec(w_glu.shape),
            _const_spec(wa.shape),
            _const_spec(wb.shape),
            _const_spec(wc.shape),
        ],
        out_specs=pl.BlockSpec((None, tm, d), lambda b, j: (b, j, 0)),
        compiler_params=_cp(("parallel", "parallel")),
        name="merge_a",
    )(ya, o_f, o_b, p, y_f, y_b, p, p, p, p, hn_g, d_skip, w_glu, wa, wb, wc)


def _merge_b_kernel(m_ref, x_ref, mod_ref, wo_ref, wq_ref, x1_ref, h2_ref, q_ref):
    x1 = x_ref[...] + mod_ref[2:3, :] * _dot(m_ref[...], wo_ref[...])
    x1_ref[...] = x1
    r = lax.rsqrt(jnp.mean(x1 * x1, axis=-1, keepdims=True) + EPS)
    h2 = x1 * r * (1.0 + mod_ref[4:5, :]) + mod_ref[3:4, :]
    h2_ref[...] = h2
    q_ref[...] = _dot(h2.astype(BF16), wq_ref[...])


def _merge_b(merged, xall, mod, wo, wq, off, nctx):
    bsz, lo, d = merged.shape
    nblk = lo // TM
    dq = wq.shape[-1]
    out = lambda b, j: (b, j, 0)
    return pl.pallas_call(
        _merge_b_kernel,
        out_shape=(jax.ShapeDtypeStruct((bsz, lo, d), F32), jax.ShapeDtypeStruct((bsz, lo, d), F32),
                   jax.ShapeDtypeStruct((bsz, lo, dq), F32)),
        grid=(bsz, nblk),
        in_specs=[
            pl.BlockSpec((None, TM, d), out),
            pl.BlockSpec((None, TM, d), lambda b, j: (b, j + off, 0)),
            pl.BlockSpec((None, None, N_MOD, d), lambda b, j: (b, (j + off >= nctx).astype(jnp.int32), 0, 0)),
            _const_spec(wo.shape),
            _const_spec(wq.shape),
        ],
        out_specs=(pl.BlockSpec((None, TM, d), out), pl.BlockSpec((None, TM, d), out),
                   pl.BlockSpec((None, TM, dq), out)),
        compiler_params=_cp(("parallel", "parallel")),
        name="merge_b",
    )(merged, xall, mod, wo, wq)


def _topk_rows(s, row_id, k):
    slot = lax.broadcasted_iota(jnp.int32, (k, s.shape[1]), 0)
    vals = jnp.zeros((k, s.shape[1]), F32)
    ids = jnp.zeros((k, s.shape[1]), F32)
    for r in range(k):
        m = jnp.max(s, axis=0, keepdims=True)
        am = jnp.min(jnp.where(s == m, row_id, 1e9), axis=0, keepdims=True)
        vals = jnp.where(slot == r, m, vals)
        ids = jnp.where(slot == r, am, ids)
        s = jnp.where(row_id == am, -jnp.inf, s)
    return vals, ids


def _pair_candidates():
    k = PEER_TOPK
    pieces = [("a", 0, 0, 16), ("a", 1, 0, 8), ("a", 2, 0, 8), ("a", 3, 0, 8), ("a", 4, 0, 8),
              ("b", 0, 0, 8), ("b", 1, 0, 8), ("b", 0, 8, 16)]
    ids, bias, seen = [], [], set()
    for kind, fixed, lo, hi in pieces:
        for r in range(lo, hi):
            a, b = (fixed, r) if kind == "a" else (r, fixed)
            ok = (a + 1) * (b + 1) <= k and (a, b) not in seen
            if ok:
                seen.add((a, b))
            ids.append(float(a * k + b))
            bias.append(0.0 if ok else -math.inf)
    assert len(seen) == sum(k // (a + 1) for a in range(k))
    return pieces, ids, bias


def _route_kernel(q_ref, keys_ref, cid_ref, cbias_ref, idx_ref, g_ref):
    k = PEER_TOPK
    pieces, _, _ = _pair_candidates()
    key_id = lax.broadcasted_iota(jnp.int32, (PEER_NKEYS, q_ref.shape[0]), 0).astype(F32)
    for h in range(PEER_HEADS):
        sv, si = [], []
        for half in range(2):
            c0 = (h * 2 + half) * PEER_NKEYS
            qh = q_ref[:, c0:c0 + PEER_NKEYS].astype(BF16)
            s = _dot_nt(keys_ref[h, half], qh)
            v_, i_ = _topk_rows(s, key_id, k)
            sv.append(v_)
            si.append(i_)
        cand = jnp.concatenate(
            [sv[0][f:f + 1, :] + sv[1][lo:hi, :] if kind == "a" else sv[0][lo:hi, :] + sv[1][f:f + 1, :]
             for kind, f, lo, hi in pieces], axis=0) + cbias_ref[...]
        fv, fi = _topk_rows(cand, cid_ref[...], k)
        fa = jnp.floor(fi * (1.0 / k))
        fb = fi - fa * k
        i1 = jnp.zeros_like(fi)
        i2 = jnp.zeros_like(fi)
        for a in range(k):
            i1 = jnp.where(fa == a, si[0][a:a + 1, :], i1)
            i2 = jnp.where(fb == a, si[1][a:a + 1, :], i2)
        e = jnp.exp(fv - fv[0:1, :])
        g_ref[h * k:(h + 1) * k, :] = e / jnp.sum(e, axis=0, keepdims=True)
        idx_ref[h * k:(h + 1) * k, :] = (i1 * PEER_NKEYS + i2).astype(jnp.int32)


def _route(q, keys):
    t, dq = q.shape
    tm = LANE
    nsel = PEER_HEADS * PEER_TOPK
    _, ids, bias = _pair_candidates()
    cid = jnp.broadcast_to(jnp.asarray(ids, F32)[:, None], (len(ids), tm))
    cbias = jnp.broadcast_to(jnp.asarray(bias, F32)[:, None], (len(bias), tm))
    return pl.pallas_call(
        _route_kernel,
        out_shape=(jax.ShapeDtypeStruct((nsel, t), jnp.int32), jax.ShapeDtypeStruct((nsel, t), F32)),
        grid=(t // tm,),
        in_specs=[pl.BlockSpec((tm, dq), lambda i: (i, 0)), _const_spec(keys.shape),
                  _const_spec(cid.shape), _const_spec(cbias.shape)],
        out_specs=(pl.BlockSpec((nsel, tm), lambda i: (0, i)), pl.BlockSpec((nsel, tm), lambda i: (0, i))),
        compiler_params=_cp(("parallel",)),
        name="peer_route",
    )(q, keys, cid, cbias)


PEER_TB = 32
PEER_SLOTS = 4
PEER_AHEAD = PEER_SLOTS - 1
HALF_SUB = 8
HI_MASK = -65536


def _pack_expert_table(u, v):
    n_exp, d = u.shape

    def pack(x):
        bits = lax.bitcast_convert_type(x.astype(BF16), jnp.uint16).astype(jnp.uint32)
        word = (bits[:, :d // 2] << 16) | bits[:, d // 2:]
        return lax.bitcast_convert_type(word, jnp.int32).reshape(n_exp, HALF_SUB, LANE)

    return jnp.concatenate([pack(u), pack(v)], axis=1)


def _unpack_pair(word):
    return lax.bitcast_convert_type(word & HI_MASK, F32), lax.bitcast_convert_type(word << 16, F32)


def _experts_kernel(idx_ref, g_ref, h_ref, x_ref, mod_ref, tab_ref, o_ref, *scratch):
    bufs, sem = scratch[:PEER_SLOTS], scratch[PEER_SLOTS]
    nsel = PEER_HEADS * PEER_TOPK
    per = nsel // (2 * HALF_SUB)
    step = pl.program_id(0)

    def issue(t, slot, j0=0, j1=nsel):
        for j in range(j0, j1):
            e = idx_ref[0, t * nsel + j]
            pltpu.make_async_copy(tab_ref.at[e], bufs[slot].at[:, j, :], sem.at[slot]).start(priority=j % 2)

    def wait(slot):
        pltpu.make_async_copy(bufs[slot], bufs[slot], sem.at[slot]).wait()

    @pl.when(step == 0)
    def _():
        for s in range(PEER_AHEAD):
            issue(s, s)

    eye = lax.broadcasted_iota(jnp.int32, (nsel, nsel), 0) == lax.broadcasted_iota(jnp.int32, (nsel, nsel), 1)

    def group(t4, c):
        for s in range(PEER_SLOTS):
            t = t4 * PEER_SLOTS + s
            wait(s)
            nxt = (s + PEER_AHEAD) % PEER_SLOTS
            row = pl.ds(t, 1)
            hrow = h_ref[row, :]
            buf = bufs[s]
            acc = None
            for cc in range(HALF_SUB):
                u_hi, u_lo = _unpack_pair(buf[cc])
                term = (u_hi * hrow[:, cc * LANE:(cc + 1) * LANE]
                        + u_lo * hrow[:, (HALF_SUB + cc) * LANE:(HALF_SUB + cc + 1) * LANE])
                acc = term if acc is None else acc + term
                issue(t + PEER_AHEAD, nxt, cc * per, (cc + 1) * per)
            act = jnp.sum(acc, axis=-1, keepdims=True)
            gcol = jnp.sum(jnp.where(eye, g_ref[row, :], 0.0), axis=-1, keepdims=True)
            w = _gelu(act) * gcol
            outs = [None] * (2 * HALF_SUB)
            for cc in range(HALF_SUB):
                v_hi, v_lo = _unpack_pair(buf[HALF_SUB + cc])
                outs[cc] = jnp.sum(w * v_hi, axis=0, keepdims=True)
                outs[HALF_SUB + cc] = jnp.sum(w * v_lo, axis=0, keepdims=True)
                issue(t + PEER_AHEAD, nxt, (HALF_SUB + cc) * per, (HALF_SUB + cc + 1) * per)
            o_ref[row, :] = x_ref[row, :] + mod_ref[5:6, :] * jnp.concatenate(outs, axis=-1)
        return c

    lax.fori_loop(0, PEER_TB // PEER_SLOTS, group, 0)

    @pl.when(step == pl.num_programs(0) - 1)
    def _():
        for s in range(PEER_AHEAD):
            wait(s)


def _experts(idx, g, h2, x1, mod, tab, off, nctx):
    t_all, d = h2.shape
    bsz = mod.shape[0]
    nsel = idx.shape[-1]
    tb = PEER_TB
    nb = t_all // tb
    per_b = nb // bsz
    per = TM // tb
    assert d == 2 * HALF_SUB * LANE and tb % PEER_SLOTS == 0
    main = idx.reshape(nb, tb * nsel)
    idx_ext = jnp.concatenate([main, jnp.roll(main, -1, axis=0)[:, :PEER_AHEAD * nsel]], axis=1)
    idx_ext = idx_ext.reshape(nb, 1, (tb + PEER_AHEAD) * nsel)
    tokb = lambda i: (i, 0)

    def mod_map(i):
        return (i // per_b, ((i % per_b) // per + off >= nctx).astype(jnp.int32), 0, 0)

    return pl.pallas_call(
        _experts_kernel,
        out_shape=jax.ShapeDtypeStruct((t_all, d), F32),
        grid=(nb,),
        in_specs=[
            pl.BlockSpec((None, 1, (tb + PEER_AHEAD) * nsel), lambda i: (i, 0, 0), memory_space=pltpu.SMEM),
            pl.BlockSpec((tb, nsel), tokb),
            pl.BlockSpec((tb, d), tokb),
            pl.BlockSpec((tb, d), tokb),
            pl.BlockSpec((None, None, N_MOD, d), mod_map),
            pl.BlockSpec(memory_space=pl.ANY),
        ],
        out_specs=pl.BlockSpec((tb, d), tokb),
        scratch_shapes=[pltpu.VMEM((2 * HALF_SUB, nsel, LANE), jnp.int32) for _ in range(PEER_SLOTS)]
        + [pltpu.SemaphoreType.DMA((PEER_SLOTS,))],
        compiler_params=_cp(("arbitrary",)),
        name="peer_experts",
    )(idx_ext, g, h2, x1, mod, tab)


def _final_kernel(x_ref, g_ref, o_ref):
    x = x_ref[...]
    o_ref[...] = x * lax.rsqrt(jnp.mean(x * x, axis=-1, keepdims=True) + EPS) * g_ref[...]


def _final_norm(x, g):
    bsz, n, d = x.shape
    return pl.pallas_call(
        _final_kernel,
        out_shape=jax.ShapeDtypeStruct((bsz, n, d), F32),
        grid=(bsz, n // TM),
        in_specs=[pl.BlockSpec((None, TM, d), lambda b, j: (b, j, 0)), _const_spec((1, d))],
        out_specs=pl.BlockSpec((None, TM, d), lambda b, j: (b, j, 0)),
        compiler_params=_cp(("parallel", "parallel")),
        name="final_norm",
    )(x, g.reshape(1, d))


def _pos_embed_2d(n, d):
    n_rows = n // GRID_W
    quarter = d // 4
    omega = 1.0 / (POS_THETA ** (jnp.arange(quarter, dtype=F32) / quarter))

    def axis_code(length):
        ang = jnp.arange(length, dtype=F32)[:, None] * omega[None]
        return jnp.concatenate([jnp.sin(ang), jnp.cos(ang)], axis=-1)

    er = jnp.broadcast_to(axis_code(n_rows)[:, None], (n_rows, GRID_W, d // 2))
    ec = jnp.broadcast_to(axis_code(GRID_W)[None], (n_rows, GRID_W, d // 2))
    return jnp.concatenate([er, ec], axis=-1).reshape(n, d)


def kernel(x, c, ctx, c_ctx, w_mod, b_mod, w_in, gmlp_ln_g, gmlp_ln_b, gmlp_w_s, gmlp_b_s, hgrn_lb_logits, hgrn_norm_g, s5_lam_re, s5_lam_im, s5_log_step, s5_b_re, s5_b_im, s5_c_re, s5_c_im, s5_d, s5_w_glu, w_branch_a, w_branch_b, w_branch_c, w_out, peer_w_q, peer_sub_keys, peer_u, peer_v, final_norm_g):
    bsz, n, d = x.shape
    ctx_len = ctx.shape[1]
    depth = w_mod.shape[0]
    nctx = ctx_len // TM
    da = gmlp_ln_g.shape[-1]
    db = hgrn_norm_g.shape[-1]
    dc = s5_d.shape[-1]
    assert n % TM == 0 and ctx_len % TM == 0 and bsz % 8 == 0

    xall = _embed(x, ctx, _pos_embed_2d(n, d).astype(x.dtype))

    rows = -(-(bsz + 1) // 8) * 8
    cc = jnp.zeros((rows, d), F32).at[0].set(c_ctx).at[1:bsz + 1].set(c)
    mod_all = _modulation(cc, w_mod, b_mod)

    gamma = jax.nn.softmax(hgrn_lb_logits.astype(F32), axis=0)
    lower_bounds = jnp.cumsum(gamma, axis=0) - gamma[0:1]

    for l in range(depth):
        last = l == depth - 1
        off = nctx if last else 0
        mod_c = jnp.broadcast_to(mod_all[l, 0][None], (bsz, N_MOD * d))
        mod = jnp.stack([mod_c, mod_all[l, 1:bsz + 1]], axis=1).reshape(bsz, 2, N_MOD, d)

        cu0 = 2 * da + 5 * db
        w_pad = jnp.concatenate([w_in[l][:, :cu0 + dc], jnp.zeros((d, da - dc), F32), w_in[l][:, cu0 + dc:]],
                                axis=1).astype(BF16)
        p = _in_proj(xall, mod, w_pad, ctx_len)

        bias = jnp.repeat(gmlp_b_s[l].T, da // G_A, axis=1)
        ya = _gmlp(p, gmlp_ln_g[l].reshape(1, da), gmlp_ln_b[l].reshape(1, da), gmlp_w_s[l].astype(BF16), bias)

        o_f = _hgrn(p, lower_bounds[l, 0].reshape(1, db), False, nctx)
        o_b = _hgrn(p, lower_bounds[l, 1].reshape(1, db), True, nctx)

        u_tm = jnp.transpose(p[:, :, cu0:cu0 + dc], (1, 0, 2))
        ys = []
        for r in range(2):
            prm = _s5_params(s5_lam_re[l, r], s5_lam_im[l, r], s5_log_step[l, r], s5_b_re[l, r], s5_b_im[l, r],
                             s5_c_re[l, r], s5_c_im[l, r])
            ys.append(jnp.transpose(_s5(u_tm, *prm, rev=bool(r), ctx_len=ctx_len), (1, 0, 2)))

        merged = _merge_a(ya, o_f, o_b, p, ys[0], ys[1], hgrn_norm_g[l].reshape(1, db), s5_d[l].reshape(1, dc),
                          s5_w_glu[l].astype(BF16), w_branch_a[l].astype(BF16), w_branch_b[l].astype(BF16),
                          w_branch_c[l].astype(BF16), off)
        x1, h2, q = _merge_b(merged, xall, mod, w_out[l].astype(BF16), peer_w_q[l].astype(BF16), off, nctx)

        lo = x1.shape[1]
        idx_t, g_t = _route(q.reshape(bsz * lo, -1), peer_sub_keys[l].astype(BF16))
        tab = _pack_expert_table(peer_u[l], peer_v[l])
        xall = _experts(idx_t.T, g_t.T, h2.reshape(bsz * lo, d), x1.reshape(bsz * lo, d), mod, tab, off, nctx)
        xall = xall.reshape(bsz, lo, d)

    return _final_norm(xall, final_norm_g.astype(F32))
```

```python
import functools
import math

import jax
import jax.numpy as jnp
from jax import lax
from jax.experimental import pallas as pl
from jax.experimental.pallas import tpu as pltpu

F32 = jnp.float32
BF16 = jnp.bfloat16

EPS = 1e-6
GRID_W = 64
POS_THETA = 10000.0
N_MOD = 6
G_A = 8
GMLP_CHUNK = 128
H_B = 8
HGRN_CHUNK = 32
S5_IN = 16
S5_STATE = 64
S5_SLAB_G = 8
PEER_HEADS = 8
PEER_NKEYS = 128
PEER_TOPK = 16

TM = 256
LANE = 128
SUB = 8
VMEM_LIMIT = 56 * 1024 * 1024


def _cp(sem, vmem=VMEM_LIMIT):
    return pltpu.CompilerParams(dimension_semantics=sem, vmem_limit_bytes=vmem)


def _const_spec(shape):
    nd = len(shape)
    return pl.BlockSpec(shape, lambda *_: (0,) * nd, pipeline_mode=pl.Buffered(1))


def _gelu(x):
    return x * (lax.erf(x * (1.0 / math.sqrt(2.0))) + 1.0) * 0.5


def _sigmoid(x):
    return jax.nn.sigmoid(x)


def _dot(a, b):
    return jnp.dot(a, b, preferred_element_type=F32)


def _dot_nt(a, b):
    return lax.dot_general(a, b, (((1,), (1,)), ((), ())), preferred_element_type=F32)


def _dot_tn(a, b):
    return lax.dot_general(a, b, (((0,), (0,)), ((), ())), preferred_element_type=F32)


def _embed_kernel(ctx_ref, x_ref, pos_ref, o_ref, *, nctx):
    j = pl.program_id(1)

    @pl.when(j < nctx)
    def _():
        o_ref[...] = ctx_ref[...]

    @pl.when(j >= nctx)
    def _():
        o_ref[...] = x_ref[...] + pos_ref[...]


def _embed(x, ctx, pos):
    bsz, n, d = x.shape
    nctx = ctx.shape[1] // TM
    nblk = nctx + n // TM
    return pl.pallas_call(
        functools.partial(_embed_kernel, nctx=nctx),
        out_shape=jax.ShapeDtypeStruct((bsz, nblk * TM, d), x.dtype),
        grid=(bsz, nblk),
        in_specs=[
            pl.BlockSpec((None, TM, d), lambda b, j: (b, jnp.minimum(j, nctx - 1), 0)),
            pl.BlockSpec((None, TM, d), lambda b, j: (b, jnp.maximum(j - nctx, 0), 0)),
            pl.BlockSpec((TM, d), lambda b, j: (jnp.maximum(j - nctx, 0), 0)),
        ],
        out_specs=pl.BlockSpec((None, TM, d), lambda b, j: (b, j, 0)),
        compiler_params=_cp(("parallel", "arbitrary")),
        name="embed",
    )(ctx, x, pos)


def _mod_kernel(c_ref, w_ref, b_ref, o_ref):
    c = c_ref[...]
    s = c * _sigmoid(c)
    o_ref[...] = _dot(s.astype(BF16), w_ref[...].astype(BF16)) + b_ref[...]


def _modulation(cc, w_mod, b_mod):
    depth, d, nd = w_mod.shape
    rows = cc.shape[0]
    tn = 1024
    return pl.pallas_call(
        _mod_kernel,
        out_shape=jax.ShapeDtypeStruct((depth, rows, nd), F32),
        grid=(depth, nd // tn),
        in_specs=[
            pl.BlockSpec((rows, d), lambda l, n: (0, 0)),
            pl.BlockSpec((None, d, tn), lambda l, n: (l, 0, n)),
            pl.BlockSpec((None, 1, tn), lambda l, n: (l, 0, n)),
        ],
        out_specs=pl.BlockSpec((None, rows, tn), lambda l, n: (l, 0, n)),
        compiler_params=_cp(("parallel", "arbitrary")),
        name="modulation",
    )(cc, w_mod, b_mod.reshape(depth, 1, nd))


def _inproj_kernel(x_ref, mod_ref, w_ref, o_ref, h_scr, *, tm, ctx_rows):
    j = pl.program_id(1)

    @pl.when(pl.program_id(2) == 0)
    def _():
        x = x_ref[...]
        r = lax.rsqrt(jnp.mean(x * x, axis=-1, keepdims=True) + EPS)
        row = j * tm + lax.broadcasted_iota(jnp.int32, (tm, 1), 0)
        is_ctx = row < ctx_rows
        sh = jnp.where(is_ctx, mod_ref[0, 0:1, :], mod_ref[1, 0:1, :])
        sc = jnp.where(is_ctx, mod_ref[0, 1:2, :], mod_ref[1, 1:2, :])
        h_scr[...] = (x * r * (1.0 + sc) + sh).astype(BF16)

    o_ref[...] = _dot(h_scr[...], w_ref[...])


def _in_proj(xall, mod, w_pad, ctx_rows):
    bsz, l, d = xall.shape
    ncol = w_pad.shape[1]
    tm = 768 if l % 768 == 0 else TM
    tn = 1024
    return pl.pallas_call(
        functools.partial(_inproj_kernel, tm=tm, ctx_rows=ctx_rows),
        out_shape=jax.ShapeDtypeStruct((bsz, l, ncol), F32),
        grid=(bsz, l // tm, ncol // tn),
        in_specs=[
            pl.BlockSpec((None, tm, d), lambda b, j, n: (b, j, 0)),
            pl.BlockSpec((None, 2, N_MOD, d), lambda b, j, n: (b, 0, 0, 0)),
            pl.BlockSpec((d, tn), lambda b, j, n: (0, n)),
        ],
        out_specs=pl.BlockSpec((None, tm, tn), lambda b, j, n: (b, j, n)),
        scratch_shapes=[pltpu.VMEM((tm, d), BF16)],
        compiler_params=_cp(("parallel", "parallel", "arbitrary")),
        name="in_proj",
    )(xall, mod, w_pad)


def _gmlp_kernel(u_ref, v_ref, lng_ref, lnb_ref, ws_ref, bias_ref, o_ref):
    v = _gelu(v_ref[...])
    mu = jnp.mean(v, axis=-1, keepdims=True)
    vc = v - mu
    var = jnp.mean(vc * vc, axis=-1, keepdims=True)
    v = (vc * lax.rsqrt(var + EPS) * lng_ref[...] + lnb_ref[...]).astype(BF16)
    for c in range(TM // GMLP_CHUNK):
        rows = slice(c * GMLP_CHUNK, (c + 1) * GMLP_CHUNK)
        for g in range(G_A):
            cols = slice(g * LANE, (g + 1) * LANE)
            mixed = _dot(ws_ref[g], v[rows, cols]) + bias_ref[:, cols]
            o_ref[rows, cols] = _gelu(u_ref[rows, cols]) * mixed


def _gmlp(p, ln_g, ln_b, w_s, bias):
    bsz, l, _ = p.shape
    da = ln_g.shape[-1]
    return pl.pallas_call(
        _gmlp_kernel,
        out_shape=jax.ShapeDtypeStruct((bsz, l, da), F32),
        grid=(bsz, l // TM),
        in_specs=[
            pl.BlockSpec((None, TM, da), lambda b, j: (b, j, 0)),
            pl.BlockSpec((None, TM, da), lambda b, j: (b, j, 1)),
            _const_spec((1, da)),
            _const_spec((1, da)),
            _const_spec((G_A, GMLP_CHUNK, GMLP_CHUNK)),
            _const_spec((GMLP_CHUNK, da)),
        ],
        out_specs=pl.BlockSpec((None, TM, da), lambda b, j: (b, j, 0)),
        compiler_params=_cp(("parallel", "parallel")),
        name="gmlp",
    )(p, p, ln_g, ln_b, w_s, bias)


def _scan_block(p, rev, nctx, nblk):
    if not rev:
        return p
    return jnp.where(p < nctx, nctx - 1 - p, nblk - 1 - (p - nctx))


def _hgrn_kernel(q_ref, f_ref, i_ref, lb_ref, o_ref, st_ref, *, rev):
    c_sz = HGRN_CHUNK
    nchunk = TM // c_sz

    @pl.when(pl.program_id(1) == 0)
    def _():
        st_ref[...] = jnp.zeros_like(st_ref)

    nsb = c_sz // SUB
    rowi = lax.broadcasted_iota(jnp.int32, (c_sz, LANE), 0)
    row8 = lax.broadcasted_iota(jnp.int32, (SUB, LANE), 0)
    rowa = lax.broadcasted_iota(jnp.int32, (c_sz, c_sz), 0)

    def chunk_body(ci, carry):
        c = (nchunk - 1 - ci) if rev else ci
        rows = pl.ds(pl.multiple_of(c * c_sz, c_sz), c_sz)
        for hh in range(H_B):
            cols = slice(hh * LANE, (hh + 1) * LANE)
            lbh = lb_ref[:, cols]
            st = st_ref[hh]
            qr = q_ref[rows, cols]
            q = qr * _sigmoid(qr)
            fg = lbh + (1.0 - lbh) * _sigmoid(f_ref[rows, cols])
            k = 1.0 - fg
            v = i_ref[rows, cols]
            b = jnp.log(fg)
            for s in (1, 2, 4, 8, 16):
                if rev:
                    b = b + jnp.where(rowi < c_sz - s, pltpu.roll(b, c_sz - s, 0), 0.0)
                else:
                    b = b + jnp.where(rowi >= s, pltpu.roll(b, s, 0), 0.0)
            b_last = b[0:1, :] if rev else b[c_sz - 1:c_sz, :]
            o = _dot_nt((q * jnp.exp(b)).astype(BF16), st.astype(BF16))
            diag = []
            for jb in range(nsb):
                r8 = slice(jb * SUB, (jb + 1) * SUB)
                bj, qj, kj, vj = b[r8], q[r8], k[r8], v[r8]
                oj = jnp.zeros((SUB, LANE), F32)
                for s in range(SUB):
                    valid = (row8 <= s) if rev else (row8 >= s)
                    dec = jnp.exp(bj - bj[s:s + 1, :])
                    pr = jnp.where(valid, qj * (kj[s:s + 1, :] * dec), 0.0)
                    oj = oj + jnp.sum(pr, axis=-1, keepdims=True) * vj[s:s + 1, :]
                diag.append(oj)
            o = o + jnp.concatenate(diag, axis=0)
            att = jnp.zeros((c_sz, c_sz), F32)
            for jb in (range(1, nsb) if rev else range(nsb - 1)):
                edge = jb * SUB if rev else jb * SUB + SUB - 1
                b_ref_row = b[edge:edge + 1, :]
                in_blk = (rowi >= jb * SUB) & (rowi < (jb + 1) * SUB)
                kt = jnp.where(in_blk, k * jnp.exp(jnp.minimum(b_ref_row - b, 0.0)), 0.0)
                qq = q * jnp.exp(jnp.minimum(b - b_ref_row, 0.0))
                a = _dot_nt(qq.astype(BF16), kt.astype(BF16))
                t_ok = (rowa < jb * SUB) if rev else (rowa >= (jb + 1) * SUB)
                att = att + jnp.where(t_ok, a, 0.0)
            o = o + _dot(att.astype(BF16), v.astype(BF16))
            o_ref[rows, cols] = o
            kk = k * jnp.exp(b_last - b)
            st_ref[hh] = st * jnp.exp(b_last) + _dot_tn(v.astype(BF16), kk.astype(BF16))
        return carry

    lax.fori_loop(0, nchunk, chunk_body, 0)


def _hgrn(p, lb, rev, nctx):
    bsz, l, _ = p.shape
    db = lb.shape[-1]
    nblk = l // TM
    blk = functools.partial(_scan_block, rev=rev, nctx=nctx, nblk=nblk)
    fcol = 4 if rev else 3
    return pl.pallas_call(
        functools.partial(_hgrn_kernel, rev=rev),
        out_shape=jax.ShapeDtypeStruct((bsz, l, db), F32),
        grid=(bsz, nblk),
        in_specs=[
            pl.BlockSpec((None, TM, db), lambda b, p_: (b, blk(p_), 2)),
            pl.BlockSpec((None, TM, db), lambda b, p_: (b, blk(p_), fcol)),
            pl.BlockSpec((None, TM, db), lambda b, p_: (b, blk(p_), 5)),
            _const_spec((1, db)),
        ],
        out_specs=pl.BlockSpec((None, TM, db), lambda b, p_: (b, blk(p_), 0)),
        scratch_shapes=[pltpu.VMEM((H_B, LANE, LANE), F32)],
        compiler_params=_cp(("parallel", "arbitrary")),
        name="hgrn_bwd" if rev else "hgrn_fwd",
    )(p, p, p, lb)


def _s5_kernel(u_ref, bb_ref, lam_ref, cc_ref, o_ref, bu_scr, st_ref, *, rev, tb, bsz, nslab):
    half = S5_SLAB_G * S5_STATE

    @pl.when(pl.program_id(0) == 0)
    def _():
        st_ref[...] = jnp.zeros_like(st_ref)

    for j in range(nslab):
        uj = u_ref[:, :, j * LANE:(j + 1) * LANE].reshape(tb * bsz, LANE).astype(BF16)
        bu_scr[...] = _dot(uj, bb_ref[j])
        lre = jnp.broadcast_to(lam_ref[j, 0:1, :], (bsz, half))
        lim = jnp.broadcast_to(lam_ref[j, 1:2, :], (bsz, half))

        def step(i, carry):
            xre, xim = carry
            t = (tb - 1 - i) if rev else i
            rows = pl.ds(pl.multiple_of(t * bsz, bsz), bsz)
            nre = lre * xre - lim * xim + bu_scr[rows, 0:half]
            nim = lre * xim + lim * xre + bu_scr[rows, half:2 * half]
            bu_scr[rows, 0:half] = nre
            bu_scr[rows, half:2 * half] = nim
            return nre, nim

        xre, xim = lax.fori_loop(0, tb, step, (st_ref[j, :, 0:half], st_ref[j, :, half:2 * half]), unroll=4)
        st_ref[j, :, 0:half] = xre
        st_ref[j, :, half:2 * half] = xim
        y = _dot(bu_scr[...].astype(BF16), cc_ref[j])
        o_ref[:, :, j * LANE:(j + 1) * LANE] = y.reshape(tb, bsz, LANE)


def _s5(u_tm, bbig, lam, cbig, rev, ctx_len):
    l, bsz, dc = u_tm.shape
    nslab = dc // LANE
    tb = 128
    nblk, nctx = l // tb, ctx_len // tb
    blk = functools.partial(_scan_block, rev=rev, nctx=nctx, nblk=nblk)
    half = S5_SLAB_G * S5_STATE
    return pl.pallas_call(
        functools.partial(_s5_kernel, rev=rev, tb=tb, bsz=bsz, nslab=nslab),
        out_shape=jax.ShapeDtypeStruct((l, bsz, dc), F32),
        grid=(nblk,),
        in_specs=[
            pl.BlockSpec((tb, bsz, dc), lambda p_: (blk(p_), 0, 0)),
            _const_spec((nslab, LANE, 2 * half)),
            _const_spec((nslab, 2, half)),
            _const_spec((nslab, 2 * half, LANE)),
        ],
        out_specs=pl.BlockSpec((tb, bsz, dc), lambda p_: (blk(p_), 0, 0)),
        scratch_shapes=[pltpu.VMEM((tb * bsz, 2 * half), F32), pltpu.VMEM((nslab, bsz, 2 * half), F32)],
        compiler_params=_cp(("arbitrary",)),
        name="s5_bwd" if rev else "s5_fwd",
    )(u_tm, bbig, lam, cbig)


def _s5_params(lam_re, lam_im, log_step, b_re, b_im, c_re, c_im):
    g_c, p_s = lam_re.shape
    nslab = g_c // S5_SLAB_G
    lam = lax.complex(lam_re.astype(F32), lam_im.astype(F32))
    lam_dt = lam * jnp.exp(log_step.astype(F32))[:, None]
    lam_bar = jnp.exp(lam_dt)
    b_bar = ((lam_bar - 1.0) / lam)[:, :, None] * lax.complex(b_re.astype(F32), b_im.astype(F32))
    eye = jnp.eye(S5_SLAB_G, dtype=F32)

    def in_map(m):
        m = m.reshape(nslab, S5_SLAB_G, p_s, S5_IN)
        return jnp.einsum('jgpd,gh->jgdhp', m, eye).reshape(nslab, S5_SLAB_G * S5_IN, S5_SLAB_G * p_s)

    def out_map(m):
        m = m.reshape(nslab, S5_SLAB_G, S5_IN, p_s)
        return jnp.einsum('jgdp,gh->jgphd', m, eye).reshape(nslab, S5_SLAB_G * p_s, S5_SLAB_G * S5_IN)

    bbig = jnp.concatenate([in_map(jnp.real(b_bar)), in_map(jnp.imag(b_bar))], axis=-1).astype(BF16)
    cbig = jnp.concatenate([out_map(c_re.astype(F32)), out_map(-c_im.astype(F32))], axis=1).astype(BF16)
    lam_s = jnp.stack([jnp.real(lam_bar).reshape(nslab, -1), jnp.imag(lam_bar).reshape(nslab, -1)], axis=1)
    return bbig, lam_s, cbig


def _merge_a_kernel(ya_ref, of_ref, ob_ref, bg_ref, yf_ref, yb_ref, cu_ref, ga_ref, gb_ref, gc_ref,
                    hng_ref, dsk_ref, wglu_ref, wa_ref, wb_ref, wc_ref, o_ref):
    dc = dsk_ref.shape[-1]
    o = of_ref[...] + ob_ref[...]
    parts = []
    for hh in range(H_B):
        oh = o[:, hh * LANE:(hh + 1) * LANE]
        parts.append(oh * lax.rsqrt(jnp.mean(oh * oh, axis=-1, keepdims=True) + EPS))
    bg = bg_ref[...]
    yb = jnp.concatenate(parts, axis=-1) * hng_ref[...] * (bg * _sigmoid(bg))
    y = yf_ref[...] + yb_ref[...] + dsk_ref[...] * cu_ref[:, 0:dc]
    z = _dot(_gelu(y).astype(BF16), wglu_ref[...])
    yc = z[:, 0:dc] * _sigmoid(z[:, dc:2 * dc])
    merged = (_sigmoid(ga_ref[...]) * _dot(ya_ref[...].astype(BF16), wa_ref[...])
              + _sigmoid(gb_ref[...]) * _dot(yb.astype(BF16), wb_ref[...])
              + _sigmoid(gc_ref[...]) * _dot(yc.astype(BF16), wc_ref[...]))
    o_ref[...] = merged.astype(BF16)


def _merge_a(ya, o_f, o_b, p, y_f, y_b, hn_g, d_skip, w_glu, wa, wb, wc, off):
    bsz, l, da = ya.shape
    dc = y_f.shape[-1]
    d = wa.shape[-1]
    tm = 128
    o2 = off * (TM // tm)
    nblk = l // tm - o2
    tok = lambda b, j: (b, j + o2, 0)
    pcol = lambda c: (lambda b, j: (b, j + o2, c))
    return pl.pallas_call(
        _merge_a_kernel,
        out_shape=jax.ShapeDtypeStruct((bsz, nblk * tm, d), BF16),
        grid=(bsz, nblk),
        in_specs=[
            pl.BlockSpec((None, tm, da), tok),
            pl.BlockSpec((None, tm, da), tok),
            pl.BlockSpec((None, tm, da), tok),
            pl.BlockSpec((None, tm, da), pcol(6)),
            pl.BlockSpec((None, tm, dc), tok),
            pl.BlockSpec((None, tm, dc), tok),
            pl.BlockSpec((None, tm, da), pcol(7)),
            pl.BlockSpec((None, tm, d), pcol(4)),
            pl.BlockSpec((None, tm, d), pcol(5)),
            pl.BlockSpec((None, tm, d), pcol(6)),
            _const_spec((1, da)),
            _const_spec((1, dc)),
            _const_spec(w_glu.shape),
            _const_spec(wa.shape),
            _const_spec(wb.shape),
            _const_spec(wc.shape),
        ],
        out_specs=pl.BlockSpec((None, tm, d), lambda b, j: (b, j, 0)),
        compiler_params=_cp(("parallel", "parallel")),
        name="merge_a",
    )(ya, o_f, o_b, p, y_f, y_b, p, p, p, p, hn_g, d_skip, w_glu, wa, wb, wc)


def _merge_b_kernel(m_ref, x_ref, mod_ref, wo_ref, wq_ref, x1_ref, h2_ref, q_ref):
    x1 = x_ref[...] + mod_ref[2:3, :] * _dot(m_ref[...], wo_ref[...])
    x1_ref[...] = x1
    r = lax.rsqrt(jnp.mean(x1 * x1, axis=-1, keepdims=True) + EPS)
    h2 = x1 * r * (1.0 + mod_ref[4:5, :]) + mod_ref[3:4, :]
    h2_ref[...] = h2
    q_ref[...] = _dot(h2.astype(BF16), wq_ref[...])


def _merge_b(merged, xall, mod, wo, wq, off, nctx):
    bsz, lo, d = merged.shape
    nblk = lo // TM
    dq = wq.shape[-1]
    out = lambda b, j: (b, j, 0)
    return pl.pallas_call(
        _merge_b_kernel,
        out_shape=(jax.ShapeDtypeStruct((bsz, lo, d), F32), jax.ShapeDtypeStruct((bsz, lo, d), F32),
                   jax.ShapeDtypeStruct((bsz, lo, dq), F32)),
        grid=(bsz, nblk),
        in_specs=[
            pl.BlockSpec((None, TM, d), out),
            pl.BlockSpec((None, TM, d), lambda b, j: (b, j + off, 0)),
            pl.BlockSpec((None, None, N_MOD, d), lambda b, j: (b, (j + off >= nctx).astype(jnp.int32), 0, 0)),
            _const_spec(wo.shape),
            _const_spec(wq.shape),
        ],
        out_specs=(pl.BlockSpec((None, TM, d), out), pl.BlockSpec((None, TM, d), out),
                   pl.BlockSpec((None, TM, dq), out)),
        compiler_params=_cp(("parallel", "parallel")),
        name="merge_b",
    )(merged, xall, mod, wo, wq)


def _topk_rows(s, row_id, k):
    slot = lax.broadcasted_iota(jnp.int32, (k, s.shape[1]), 0)
    vals = jnp.zeros((k, s.shape[1]), F32)
    ids = jnp.zeros((k, s.shape[1]), F32)
    for r in range(k):
        m = jnp.max(s, axis=0, keepdims=True)
        am = jnp.min(jnp.where(s == m, row_id, 1e9), axis=0, keepdims=True)
        vals = jnp.where(slot == r, m, vals)
        ids = jnp.where(slot == r, am, ids)
        s = jnp.where(row_id == am, -jnp.inf, s)
    return vals, ids


def _pair_candidates():
    k = PEER_TOPK
    pieces = [("a", 0, 0, 16), ("a", 1, 0, 8), ("a", 2, 0, 8), ("a", 3, 0, 8), ("a", 4, 0, 8),
              ("b", 0, 0, 8), ("b", 1, 0, 8), ("b", 0, 8, 16)]
    ids, bias, seen = [], [], set()
    for kind, fixed, lo, hi in pieces:
        for r in range(lo, hi):
            a, b = (fixed, r) if kind == "a" else (r, fixed)
            ok = (a + 1) * (b + 1) <= k and (a, b) not in seen
            if ok:
                seen.add((a, b))
            ids.append(float(a * k + b))
            bias.append(0.0 if ok else -math.inf)
    assert len(seen) == sum(k // (a + 1) for a in range(k))
    return pieces, ids, bias


def _route_kernel(q_ref, keys_ref, cid_ref, cbias_ref, idx_ref, g_ref):
    k = PEER_TOPK
    pieces, _, _ = _pair_candidates()
    key_id = lax.broadcasted_iota(jnp.int32, (PEER_NKEYS, q_ref.shape[0]), 0).astype(F32)
    for h in range(PEER_HEADS):
        sv, si = [], []
        for half in range(2):
            c0 = (h * 2 + half) * PEER_NKEYS
            qh = q_ref[:, c0:c0 + PEER_NKEYS].astype(BF16)
            s = _dot_nt(keys_ref[h, half], qh)
            v_, i_ = _topk_rows(s, key_id, k)
            sv.append(v_)
            si.append(i_)
        cand = jnp.concatenate(
            [sv[0][f:f + 1, :] + sv[1][lo:hi, :] if kind == "a" else sv[0][lo:hi, :] + sv[1][f:f + 1, :]
             for kind, f, lo, hi in pieces], axis=0) + cbias_ref[...]
        fv, fi = _topk_rows(cand, cid_ref[...], k)
        fa = jnp.floor(fi * (1.0 / k))
        fb = fi - fa * k
        i1 = jnp.zeros_like(fi)
        i2 = jnp.zeros_like(fi)
        for a in range(k):
            i1 = jnp.where(fa == a, si[0][a:a + 1, :], i1)
            i2 = jnp.where(fb == a, si[1][a:a + 1, :], i2)
        e = jnp.exp(fv - fv[0:1, :])
        g_ref[h * k:(h + 1) * k, :] = e / jnp.sum(e, axis=0, keepdims=True)
        idx_ref[h * k:(h + 1) * k, :] = (i1 * PEER_NKEYS + i2).astype(jnp.int32)


def _route(q, keys):
    t, dq = q.shape
    tm = LANE
    nsel = PEER_HEADS * PEER_TOPK
    _, ids, bias = _pair_candidates()
    cid = jnp.broadcast_to(jnp.asarray(ids, F32)[:, None], (len(ids), tm))
    cbias = jnp.broadcast_to(jnp.asarray(bias, F32)[:, None], (len(bias), tm))
    return pl.pallas_call(
        _route_kernel,
        out_shape=(jax.ShapeDtypeStruct((nsel, t), jnp.int32), jax.ShapeDtypeStruct((nsel, t), F32)),
        grid=(t // tm,),
        in_specs=[pl.BlockSpec((tm, dq), lambda i: (i, 0)), _const_spec(keys.shape),
                  _const_spec(cid.shape), _const_spec(cbias.shape)],
        out_specs=(pl.BlockSpec((nsel, tm), lambda i: (0, i)), pl.BlockSpec((nsel, tm), lambda i: (0, i))),
        compiler_params=_cp(("parallel",)),
        name="peer_route",
    )(q, keys, cid, cbias)


PEER_TB = 64
PEER_SLOTS = 4
PEER_AHEAD = PEER_SLOTS - 1
HALF_SUB = 8
HI_MASK = -65536


def _pack_expert_table(u, v):
    n_exp, d = u.shape

    def pack(x):
        bits = lax.bitcast_convert_type(x.astype(BF16), jnp.uint16).astype(jnp.uint32)
        word = (bits[:, :d // 2] << 16) | bits[:, d // 2:]
        return lax.bitcast_convert_type(word, jnp.int32).reshape(n_exp, HALF_SUB, LANE)

    return jnp.concatenate([pack(u), pack(v)], axis=1)


def _unpack_pair(word):
    return lax.bitcast_convert_type(word & HI_MASK, F32), lax.bitcast_convert_type(word << 16, F32)


def _experts_kernel(idx_ref, g_ref, h_ref, x_ref, mod_ref, fin_ref, tab_ref, o_ref, *scratch, final_norm):
    bufs, sem = scratch[:PEER_SLOTS], scratch[PEER_SLOTS]
    nsel = PEER_HEADS * PEER_TOPK
    per = nsel // (2 * HALF_SUB)
    step = pl.program_id(0)

    def issue(t, slot, j0=0, j1=nsel):
        for j in range(j0, j1):
            e = idx_ref[0, t * nsel + j]
            pltpu.make_async_copy(tab_ref.at[e], bufs[slot].at[:, j, :], sem.at[slot]).start(priority=j % 2)

    def wait(slot):
        pltpu.make_async_copy(bufs[slot], bufs[slot], sem.at[slot]).wait()

    @pl.when(step == 0)
    def _():
        for s in range(PEER_AHEAD):
            issue(s, s)

    eye = lax.broadcasted_iota(jnp.int32, (nsel, nsel), 0) == lax.broadcasted_iota(jnp.int32, (nsel, nsel), 1)

    def group(t4, c):
        for s in range(PEER_SLOTS):
            t = t4 * PEER_SLOTS + s
            wait(s)
            nxt = (s + PEER_AHEAD) % PEER_SLOTS
            row = pl.ds(t, 1)
            hrow = h_ref[row, :]
            buf = bufs[s]
            acc = None
            for cc in range(HALF_SUB):
                u_hi, u_lo = _unpack_pair(buf[cc])
                term = (u_hi * hrow[:, cc * LANE:(cc + 1) * LANE]
                        + u_lo * hrow[:, (HALF_SUB + cc) * LANE:(HALF_SUB + cc + 1) * LANE])
                acc = term if acc is None else acc + term
                issue(t + PEER_AHEAD, nxt, cc * per, (cc + 1) * per)
            act = jnp.sum(acc, axis=-1, keepdims=True)
            gcol = jnp.sum(jnp.where(eye, g_ref[row, :], 0.0), axis=-1, keepdims=True)
            w = _gelu(act) * gcol
            outs = [None] * (2 * HALF_SUB)
            for cc in range(HALF_SUB):
                v_hi, v_lo = _unpack_pair(buf[HALF_SUB + cc])
                outs[cc] = jnp.sum(w * v_hi, axis=0, keepdims=True)
                outs[HALF_SUB + cc] = jnp.sum(w * v_lo, axis=0, keepdims=True)
                issue(t + PEER_AHEAD, nxt, (HALF_SUB + cc) * per, (HALF_SUB + cc + 1) * per)
            o_ref[row, :] = x_ref[row, :] + mod_ref[5:6, :] * jnp.concatenate(outs, axis=-1)
        return c

    lax.fori_loop(0, PEER_TB // PEER_SLOTS, group, 0)

    if final_norm:
        y = o_ref[...]
        o_ref[...] = y * lax.rsqrt(jnp.mean(y * y, axis=-1, keepdims=True) + EPS) * fin_ref[...]

    @pl.when(step == pl.num_programs(0) - 1)
    def _():
        for s in range(PEER_AHEAD):
            wait(s)


def _experts(idx, g, h2, x1, mod, fin_g, tab, off, nctx, final_norm):
    t_all, d = h2.shape
    bsz = mod.shape[0]
    nsel = idx.shape[-1]
    tb = PEER_TB
    nb = t_all // tb
    per_b = nb // bsz
    per = TM // tb
    assert d == 2 * HALF_SUB * LANE and tb % PEER_SLOTS == 0
    main = idx.reshape(nb, tb * nsel)
    idx_ext = jnp.concatenate([main, jnp.roll(main, -1, axis=0)[:, :PEER_AHEAD * nsel]], axis=1)
    idx_ext = idx_ext.reshape(nb, 1, (tb + PEER_AHEAD) * nsel)
    tokb = lambda i: (i, 0)

    def mod_map(i):
        return (i // per_b, ((i % per_b) // per + off >= nctx).astype(jnp.int32), 0, 0)

    return pl.pallas_call(
        functools.partial(_experts_kernel, final_norm=final_norm),
        out_shape=jax.ShapeDtypeStruct((t_all, d), F32),
        grid=(nb,),
        in_specs=[
            pl.BlockSpec((None, 1, (tb + PEER_AHEAD) * nsel), lambda i: (i, 0, 0), memory_space=pltpu.SMEM),
            pl.BlockSpec((tb, nsel), tokb),
            pl.BlockSpec((tb, d), tokb),
            pl.BlockSpec((tb, d), tokb),
            pl.BlockSpec((None, None, N_MOD, d), mod_map),
            _const_spec((1, d)),
            pl.BlockSpec(memory_space=pl.ANY),
        ],
        out_specs=pl.BlockSpec((tb, d), tokb),
        scratch_shapes=[pltpu.VMEM((2 * HALF_SUB, nsel, LANE), jnp.int32) for _ in range(PEER_SLOTS)]
        + [pltpu.SemaphoreType.DMA((PEER_SLOTS,))],
        compiler_params=_cp(("arbitrary",)),
        name="peer_experts",
    )(idx_ext, g, h2, x1, mod, fin_g, tab)


def _pos_embed_2d(n, d):
    n_rows = n // GRID_W
    quarter = d // 4
    omega = 1.0 / (POS_THETA ** (jnp.arange(quarter, dtype=F32) / quarter))

    def axis_code(length):
        ang = jnp.arange(length, dtype=F32)[:, None] * omega[None]
        return jnp.concatenate([jnp.sin(ang), jnp.cos(ang)], axis=-1)

    er = jnp.broadcast_to(axis_code(n_rows)[:, None], (n_rows, GRID_W, d // 2))
    ec = jnp.broadcast_to(axis_code(GRID_W)[None], (n_rows, GRID_W, d // 2))
    return jnp.concatenate([er, ec], axis=-1).reshape(n, d)


def kernel(x, c, ctx, c_ctx, w_mod, b_mod, w_in, gmlp_ln_g, gmlp_ln_b, gmlp_w_s, gmlp_b_s, hgrn_lb_logits, hgrn_norm_g, s5_lam_re, s5_lam_im, s5_log_step, s5_b_re, s5_b_im, s5_c_re, s5_c_im, s5_d, s5_w_glu, w_branch_a, w_branch_b, w_branch_c, w_out, peer_w_q, peer_sub_keys, peer_u, peer_v, final_norm_g):
    bsz, n, d = x.shape
    ctx_len = ctx.shape[1]
    depth = w_mod.shape[0]
    nctx = ctx_len // TM
    da = gmlp_ln_g.shape[-1]
    db = hgrn_norm_g.shape[-1]
    dc = s5_d.shape[-1]
    assert n % TM == 0 and ctx_len % TM == 0 and bsz % 8 == 0

    xall = _embed(x, ctx, _pos_embed_2d(n, d).astype(x.dtype))

    rows = -(-(bsz + 1) // 8) * 8
    cc = jnp.zeros((rows, d), F32).at[0].set(c_ctx).at[1:bsz + 1].set(c)
    mod_all = _modulation(cc, w_mod, b_mod)

    gamma = jax.nn.softmax(hgrn_lb_logits.astype(F32), axis=0)
    lower_bounds = jnp.cumsum(gamma, axis=0) - gamma[0:1]

    for l in range(depth):
        last = l == depth - 1
        off = nctx if last else 0
        mod_c = jnp.broadcast_to(mod_all[l, 0][None], (bsz, N_MOD * d))
        mod = jnp.stack([mod_c, mod_all[l, 1:bsz + 1]], axis=1).reshape(bsz, 2, N_MOD, d)

        cu0 = 2 * da + 5 * db
        w_pad = jnp.concatenate([w_in[l][:, :cu0 + dc], jnp.zeros((d, da - dc), F32), w_in[l][:, cu0 + dc:]],
                                axis=1).astype(BF16)
        p = _in_proj(xall, mod, w_pad, ctx_len)

        bias = jnp.repeat(gmlp_b_s[l].T, da // G_A, axis=1)
        ya = _gmlp(p, gmlp_ln_g[l].reshape(1, da), gmlp_ln_b[l].reshape(1, da), gmlp_w_s[l].astype(BF16), bias)

        o_f = _hgrn(p, lower_bounds[l, 0].reshape(1, db), False, nctx)
        o_b = _hgrn(p, lower_bounds[l, 1].reshape(1, db), True, nctx)

        u_tm = jnp.transpose(p[:, :, cu0:cu0 + dc], (1, 0, 2))
        ys = []
        for r in range(2):
            prm = _s5_params(s5_lam_re[l, r], s5_lam_im[l, r], s5_log_step[l, r], s5_b_re[l, r], s5_b_im[l, r],
                             s5_c_re[l, r], s5_c_im[l, r])
            ys.append(jnp.transpose(_s5(u_tm, *prm, rev=bool(r), ctx_len=ctx_len), (1, 0, 2)))

        merged = _merge_a(ya, o_f, o_b, p, ys[0], ys[1], hgrn_norm_g[l].reshape(1, db), s5_d[l].reshape(1, dc),
                          s5_w_glu[l].astype(BF16), w_branch_a[l].astype(BF16), w_branch_b[l].astype(BF16),
                          w_branch_c[l].astype(BF16), off)
        x1, h2, q = _merge_b(merged, xall, mod, w_out[l].astype(BF16), peer_w_q[l].astype(BF16), off, nctx)

        lo = x1.shape[1]
        idx_t, g_t = _route(q.reshape(bsz * lo, -1), peer_sub_keys[l].astype(BF16))
        tab = _pack_expert_table(peer_u[l], peer_v[l])
        xall = _experts(idx_t.T, g_t.T, h2.reshape(bsz * lo, d), x1.reshape(bsz * lo, d), mod,
                        final_norm_g.astype(F32).reshape(1, d), tab, off, nctx, final_norm=last)
        xall = xall.reshape(bsz, lo, d)

    return xall
```

```python
import functools
import math

import jax
import jax.numpy as jnp
from jax import lax
from jax.experimental import pallas as pl
from jax.experimental.pallas import tpu as pltpu

F32 = jnp.float32
BF16 = jnp.bfloat16

EPS = 1e-6
GRID_W = 64
POS_THETA = 10000.0
N_MOD = 6
G_A = 8
GMLP_CHUNK = 128
H_B = 8
HGRN_CHUNK = 32
S5_IN = 16
S5_STATE = 64
S5_SLAB_G = 8
PEER_HEADS = 8
PEER_NKEYS = 128
PEER_TOPK = 16

TM = 256
LANE = 128
SUB = 8
VMEM_LIMIT = 56 * 1024 * 1024


def _cp(sem, vmem=VMEM_LIMIT):
    return pltpu.CompilerParams(dimension_semantics=sem, vmem_limit_bytes=vmem)


def _const_spec(shape):
    nd = len(shape)
    return pl.BlockSpec(shape, lambda *_: (0,) * nd, pipeline_mode=pl.Buffered(1))


def _gelu(x):
    return x * (lax.erf(x * (1.0 / math.sqrt(2.0))) + 1.0) * 0.5


def _sigmoid(x):
    return jax.nn.sigmoid(x)


def _dot(a, b):
    return jnp.dot(a, b, preferred_element_type=F32)


def _dot_nt(a, b):
    return lax.dot_general(a, b, (((1,), (1,)), ((), ())), preferred_element_type=F32)


def _dot_tn(a, b):
    return lax.dot_general(a, b, (((0,), (0,)), ((), ())), preferred_element_type=F32)


def _embed_kernel(ctx_ref, x_ref, pos_ref, o_ref, *, nctx):
    j = pl.program_id(1)

    @pl.when(j < nctx)
    def _():
        o_ref[...] = ctx_ref[...]

    @pl.when(j >= nctx)
    def _():
        o_ref[...] = x_ref[...] + pos_ref[...]


def _embed(x, ctx, pos):
    bsz, n, d = x.shape
    nctx = ctx.shape[1] // TM
    nblk = nctx + n // TM
    return pl.pallas_call(
        functools.partial(_embed_kernel, nctx=nctx),
        out_shape=jax.ShapeDtypeStruct((bsz, nblk * TM, d), x.dtype),
        grid=(bsz, nblk),
        in_specs=[
            pl.BlockSpec((None, TM, d), lambda b, j: (b, jnp.minimum(j, nctx - 1), 0)),
            pl.BlockSpec((None, TM, d), lambda b, j: (b, jnp.maximum(j - nctx, 0), 0)),
            pl.BlockSpec((TM, d), lambda b, j: (jnp.maximum(j - nctx, 0), 0)),
        ],
        out_specs=pl.BlockSpec((None, TM, d), lambda b, j: (b, j, 0)),
        compiler_params=_cp(("parallel", "arbitrary")),
        name="embed",
    )(ctx, x, pos)


def _mod_kernel(c_ref, w_ref, b_ref, o_ref):
    c = c_ref[...]
    s = c * _sigmoid(c)
    o_ref[...] = _dot(s.astype(BF16), w_ref[...].astype(BF16)) + b_ref[...]


def _modulation(cc, w_mod, b_mod):
    depth, d, nd = w_mod.shape
    rows = cc.shape[0]
    tn = 1024
    return pl.pallas_call(
        _mod_kernel,
        out_shape=jax.ShapeDtypeStruct((depth, rows, nd), F32),
        grid=(depth, nd // tn),
        in_specs=[
            pl.BlockSpec((rows, d), lambda l, n: (0, 0)),
            pl.BlockSpec((None, d, tn), lambda l, n: (l, 0, n)),
            pl.BlockSpec((None, 1, tn), lambda l, n: (l, 0, n)),
        ],
        out_specs=pl.BlockSpec((None, rows, tn), lambda l, n: (l, 0, n)),
        compiler_params=_cp(("parallel", "arbitrary")),
        name="modulation",
    )(cc, w_mod, b_mod.reshape(depth, 1, nd))


def _inproj_kernel(x_ref, mod_ref, w_ref, o_ref, h_scr, *, tm, ctx_rows):
    j = pl.program_id(1)

    @pl.when(pl.program_id(2) == 0)
    def _():
        x = x_ref[...]
        r = lax.rsqrt(jnp.mean(x * x, axis=-1, keepdims=True) + EPS)
        row = j * tm + lax.broadcasted_iota(jnp.int32, (tm, 1), 0)
        is_ctx = row < ctx_rows
        sh = jnp.where(is_ctx, mod_ref[0, 0:1, :], mod_ref[1, 0:1, :])
        sc = jnp.where(is_ctx, mod_ref[0, 1:2, :], mod_ref[1, 1:2, :])
        h_scr[...] = (x * r * (1.0 + sc) + sh).astype(BF16)

    o_ref[...] = _dot(h_scr[...], w_ref[...])


def _in_proj(xall, mod, w_pad, ctx_rows):
    bsz, l, d = xall.shape
    ncol = w_pad.shape[1]
    tm = 768 if l % 768 == 0 else TM
    tn = 1024
    return pl.pallas_call(
        functools.partial(_inproj_kernel, tm=tm, ctx_rows=ctx_rows),
        out_shape=jax.ShapeDtypeStruct((bsz, l, ncol), F32),
        grid=(bsz, l // tm, ncol // tn),
        in_specs=[
            pl.BlockSpec((None, tm, d), lambda b, j, n: (b, j, 0)),
            pl.BlockSpec((None, 2, N_MOD, d), lambda b, j, n: (b, 0, 0, 0)),
            pl.BlockSpec((d, tn), lambda b, j, n: (0, n)),
        ],
        out_specs=pl.BlockSpec((None, tm, tn), lambda b, j, n: (b, j, n)),
        scratch_shapes=[pltpu.VMEM((tm, d), BF16)],
        compiler_params=_cp(("parallel", "parallel", "arbitrary")),
        name="in_proj",
    )(xall, mod, w_pad)


def _gmlp_kernel(u_ref, v_ref, lng_ref, lnb_ref, ws_ref, bias_ref, o_ref):
    v = _gelu(v_ref[...])
    mu = jnp.mean(v, axis=-1, keepdims=True)
    vc = v - mu
    var = jnp.mean(vc * vc, axis=-1, keepdims=True)
    v = (vc * lax.rsqrt(var + EPS) * lng_ref[...] + lnb_ref[...]).astype(BF16)
    for c in range(TM // GMLP_CHUNK):
        rows = slice(c * GMLP_CHUNK, (c + 1) * GMLP_CHUNK)
        for g in range(G_A):
            cols = slice(g * LANE, (g + 1) * LANE)
            mixed = _dot(ws_ref[g], v[rows, cols]) + bias_ref[:, cols]
            o_ref[rows, cols] = _gelu(u_ref[rows, cols]) * mixed


def _gmlp(p, ln_g, ln_b, w_s, bias):
    bsz, l, _ = p.shape
    da = ln_g.shape[-1]
    return pl.pallas_call(
        _gmlp_kernel,
        out_shape=jax.ShapeDtypeStruct((bsz, l, da), F32),
        grid=(bsz, l // TM),
        in_specs=[
            pl.BlockSpec((None, TM, da), lambda b, j: (b, j, 0)),
            pl.BlockSpec((None, TM, da), lambda b, j: (b, j, 1)),
            _const_spec((1, da)),
            _const_spec((1, da)),
            _const_spec((G_A, GMLP_CHUNK, GMLP_CHUNK)),
            _const_spec((GMLP_CHUNK, da)),
        ],
        out_specs=pl.BlockSpec((None, TM, da), lambda b, j: (b, j, 0)),
        compiler_params=_cp(("parallel", "parallel")),
        name="gmlp",
    )(p, p, ln_g, ln_b, w_s, bias)


def _scan_block(p, rev, nctx, nblk):
    if not rev:
        return p
    return jnp.where(p < nctx, nctx - 1 - p, nblk - 1 - (p - nctx))


def _hgrn_kernel(q_ref, f_ref, i_ref, lb_ref, o_ref, st_ref, *, rev):
    c_sz = HGRN_CHUNK
    nchunk = TM // c_sz

    @pl.when(pl.program_id(1) == 0)
    def _():
        st_ref[...] = jnp.zeros_like(st_ref)

    nsb = c_sz // SUB
    rowi = lax.broadcasted_iota(jnp.int32, (c_sz, LANE), 0)
    row8 = lax.broadcasted_iota(jnp.int32, (SUB, LANE), 0)
    rowa = lax.broadcasted_iota(jnp.int32, (c_sz, c_sz), 0)

    def chunk_body(ci, carry):
        c = (nchunk - 1 - ci) if rev else ci
        rows = pl.ds(pl.multiple_of(c * c_sz, c_sz), c_sz)
        for hh in range(H_B):
            cols = slice(hh * LANE, (hh + 1) * LANE)
            lbh = lb_ref[:, cols]
            st = st_ref[hh]
            qr = q_ref[rows, cols]
            q = qr * _sigmoid(qr)
            fg = lbh + (1.0 - lbh) * _sigmoid(f_ref[rows, cols])
            k = 1.0 - fg
            v = i_ref[rows, cols]
            b = jnp.log(fg)
            for s in (1, 2, 4, 8, 16):
                if rev:
                    b = b + jnp.where(rowi < c_sz - s, pltpu.roll(b, c_sz - s, 0), 0.0)
                else:
                    b = b + jnp.where(rowi >= s, pltpu.roll(b, s, 0), 0.0)
            b_last = b[0:1, :] if rev else b[c_sz - 1:c_sz, :]
            o = _dot_nt((q * jnp.exp(b)).astype(BF16), st.astype(BF16))
            diag = []
            for jb in range(nsb):
                r8 = slice(jb * SUB, (jb + 1) * SUB)
                bj, qj, kj, vj = b[r8], q[r8], k[r8], v[r8]
                oj = jnp.zeros((SUB, LANE), F32)
                for s in range(SUB):
                    valid = (row8 <= s) if rev else (row8 >= s)
                    dec = jnp.exp(bj - bj[s:s + 1, :])
                    pr = jnp.where(valid, qj * (kj[s:s + 1, :] * dec), 0.0)
                    oj = oj + jnp.sum(pr, axis=-1, keepdims=True) * vj[s:s + 1, :]
                diag.append(oj)
            o = o + jnp.concatenate(diag, axis=0)
            att = jnp.zeros((c_sz, c_sz), F32)
            for jb in (range(1, nsb) if rev else range(nsb - 1)):
                edge = jb * SUB if rev else jb * SUB + SUB - 1
                b_ref_row = b[edge:edge + 1, :]
                in_blk = (rowi >= jb * SUB) & (rowi < (jb + 1) * SUB)
                kt = jnp.where(in_blk, k * jnp.exp(jnp.minimum(b_ref_row - b, 0.0)), 0.0)
                qq = q * jnp.exp(jnp.minimum(b - b_ref_row, 0.0))
                a = _dot_nt(qq.astype(BF16), kt.astype(BF16))
                t_ok = (rowa < jb * SUB) if rev else (rowa >= (jb + 1) * SUB)
                att = att + jnp.where(t_ok, a, 0.0)
            o = o + _dot(att.astype(BF16), v.astype(BF16))
            o_ref[rows, cols] = o
            kk = k * jnp.exp(b_last - b)
            st_ref[hh] = st * jnp.exp(b_last) + _dot_tn(v.astype(BF16), kk.astype(BF16))
        return carry

    lax.fori_loop(0, nchunk, chunk_body, 0)


def _hgrn(p, lb, rev, nctx):
    bsz, l, _ = p.shape
    db = lb.shape[-1]
    nblk = l // TM
    blk = functools.partial(_scan_block, rev=rev, nctx=nctx, nblk=nblk)
    fcol = 4 if rev else 3
    return pl.pallas_call(
        functools.partial(_hgrn_kernel, rev=rev),
        out_shape=jax.ShapeDtypeStruct((bsz, l, db), F32),
        grid=(bsz, nblk),
        in_specs=[
            pl.BlockSpec((None, TM, db), lambda b, p_: (b, blk(p_), 2)),
            pl.BlockSpec((None, TM, db), lambda b, p_: (b, blk(p_), fcol)),
            pl.BlockSpec((None, TM, db), lambda b, p_: (b, blk(p_), 5)),
            _const_spec((1, db)),
        ],
        out_specs=pl.BlockSpec((None, TM, db), lambda b, p_: (b, blk(p_), 0)),
        scratch_shapes=[pltpu.VMEM((H_B, LANE, LANE), F32)],
        compiler_params=_cp(("parallel", "arbitrary")),
        name="hgrn_bwd" if rev else "hgrn_fwd",
    )(p, p, p, lb)


def _s5_kernel(u_ref, bb_ref, lam_ref, cc_ref, o_ref, bu_scr, st_ref, *, rev, tb, bsz, nslab):
    half = S5_SLAB_G * S5_STATE

    @pl.when(pl.program_id(0) == 0)
    def _():
        st_ref[...] = jnp.zeros_like(st_ref)

    for j in range(nslab):
        uj = jnp.swapaxes(u_ref[:, :, j * LANE:(j + 1) * LANE], 0, 1).reshape(tb * bsz, LANE).astype(BF16)
        bu_scr[...] = _dot(uj, bb_ref[j])
        lre = jnp.broadcast_to(lam_ref[j, 0:1, :], (bsz, half))
        lim = jnp.broadcast_to(lam_ref[j, 1:2, :], (bsz, half))

        def step(i, carry):
            xre, xim = carry
            t = (tb - 1 - i) if rev else i
            rows = pl.ds(pl.multiple_of(t * bsz, bsz), bsz)
            nre = lre * xre - lim * xim + bu_scr[rows, 0:half]
            nim = lre * xim + lim * xre + bu_scr[rows, half:2 * half]
            bu_scr[rows, 0:half] = nre
            bu_scr[rows, half:2 * half] = nim
            return nre, nim

        xre, xim = lax.fori_loop(0, tb, step, (st_ref[j, :, 0:half], st_ref[j, :, half:2 * half]), unroll=4)
        st_ref[j, :, 0:half] = xre
        st_ref[j, :, half:2 * half] = xim
        y = _dot(bu_scr[...].astype(BF16), cc_ref[j])
        o_ref[:, :, j * LANE:(j + 1) * LANE] = jnp.swapaxes(y.reshape(tb, bsz, LANE), 0, 1)


def _s5(p, ucol, bbig, lam, cbig, rev, ctx_len):
    bsz, l, _ = p.shape
    nslab = bbig.shape[0]
    dc = nslab * LANE
    tb = 128
    nblk, nctx = l // tb, ctx_len // tb
    blk = functools.partial(_scan_block, rev=rev, nctx=nctx, nblk=nblk)
    half = S5_SLAB_G * S5_STATE
    return pl.pallas_call(
        functools.partial(_s5_kernel, rev=rev, tb=tb, bsz=bsz, nslab=nslab),
        out_shape=jax.ShapeDtypeStruct((bsz, l, dc), F32),
        grid=(nblk,),
        in_specs=[
            pl.BlockSpec((bsz, tb, ucol[1]), lambda p_: (0, blk(p_), ucol[0])),
            _const_spec((nslab, LANE, 2 * half)),
            _const_spec((nslab, 2, half)),
            _const_spec((nslab, 2 * half, LANE)),
        ],
        out_specs=pl.BlockSpec((bsz, tb, dc), lambda p_: (0, blk(p_), 0)),
        scratch_shapes=[pltpu.VMEM((tb * bsz, 2 * half), F32), pltpu.VMEM((nslab, bsz, 2 * half), F32)],
        compiler_params=_cp(("arbitrary",)),
        name="s5_bwd" if rev else "s5_fwd",
    )(p, bbig, lam, cbig)


def _s5_params(lam_re, lam_im, log_step, b_re, b_im, c_re, c_im):
    g_c, p_s = lam_re.shape
    nslab = g_c // S5_SLAB_G
    lam = lax.complex(lam_re.astype(F32), lam_im.astype(F32))
    lam_dt = lam * jnp.exp(log_step.astype(F32))[:, None]
    lam_bar = jnp.exp(lam_dt)
    b_bar = ((lam_bar - 1.0) / lam)[:, :, None] * lax.complex(b_re.astype(F32), b_im.astype(F32))
    eye = jnp.eye(S5_SLAB_G, dtype=F32)

    def in_map(m):
        m = m.reshape(nslab, S5_SLAB_G, p_s, S5_IN)
        return jnp.einsum('jgpd,gh->jgdhp', m, eye).reshape(nslab, S5_SLAB_G * S5_IN, S5_SLAB_G * p_s)

    def out_map(m):
        m = m.reshape(nslab, S5_SLAB_G, S5_IN, p_s)
        return jnp.einsum('jgdp,gh->jgphd', m, eye).reshape(nslab, S5_SLAB_G * p_s, S5_SLAB_G * S5_IN)

    bbig = jnp.concatenate([in_map(jnp.real(b_bar)), in_map(jnp.imag(b_bar))], axis=-1).astype(BF16)
    cbig = jnp.concatenate([out_map(c_re.astype(F32)), out_map(-c_im.astype(F32))], axis=1).astype(BF16)
    lam_s = jnp.stack([jnp.real(lam_bar).reshape(nslab, -1), jnp.imag(lam_bar).reshape(nslab, -1)], axis=1)
    return bbig, lam_s, cbig


def _merge_a_kernel(ya_ref, of_ref, ob_ref, bg_ref, yf_ref, yb_ref, cu_ref, ga_ref, gb_ref, gc_ref,
                    hng_ref, dsk_ref, wglu_ref, wa_ref, wb_ref, wc_ref, o_ref):
    dc = dsk_ref.shape[-1]
    o = of_ref[...] + ob_ref[...]
    parts = []
    for hh in range(H_B):
        oh = o[:, hh * LANE:(hh + 1) * LANE]
        parts.append(oh * lax.rsqrt(jnp.mean(oh * oh, axis=-1, keepdims=True) + EPS))
    bg = bg_ref[...]
    yb = jnp.concatenate(parts, axis=-1) * hng_ref[...] * (bg * _sigmoid(bg))
    y = yf_ref[...] + yb_ref[...] + dsk_ref[...] * cu_ref[:, 0:dc]
    z = _dot(_gelu(y).astype(BF16), wglu_ref[...])
    yc = z[:, 0:dc] * _sigmoid(z[:, dc:2 * dc])
    merged = (_sigmoid(ga_ref[...]) * _dot(ya_ref[...].astype(BF16), wa_ref[...])
              + _sigmoid(gb_ref[...]) * _dot(yb.astype(BF16), wb_ref[...])
              + _sigmoid(gc_ref[...]) * _dot(yc.astype(BF16), wc_ref[...]))
    o_ref[...] = merged.astype(BF16)


def _merge_a(ya, o_f, o_b, p, y_f, y_b, hn_g, d_skip, w_glu, wa, wb, wc, off):
    bsz, l, da = ya.shape
    dc = y_f.shape[-1]
    d = wa.shape[-1]
    tm = 128
    o2 = off * (TM // tm)
    nblk = l // tm - o2
    tok = lambda b, j: (b, j + o2, 0)
    pcol = lambda c: (lambda b, j: (b, j + o2, c))
    return pl.pallas_call(
        _merge_a_kernel,
        out_shape=jax.ShapeDtypeStruct((bsz, nblk * tm, d), BF16),
        grid=(bsz, nblk),
        in_specs=[
            pl.BlockSpec((None, tm, da), tok),
            pl.BlockSpec((None, tm, da), tok),
            pl.BlockSpec((None, tm, da), tok),
            pl.BlockSpec((None, tm, da), pcol(6)),
            pl.BlockSpec((None, tm, dc), tok),
            pl.BlockSpec((None, tm, dc), tok),
            pl.BlockSpec((None, tm, da), pcol(7)),
            pl.BlockSpec((None, tm, d), pcol(4)),
            pl.BlockSpec((None, tm, d), pcol(5)),
            pl.BlockSpec((None, tm, d), pcol(6)),
            _const_spec((1, da)),
            _const_spec((1, dc)),
            _const_spec(w_glu.shape),
            _const_spec(wa.shape),
            _const_spec(wb.shape),
            _const_spec(wc.shape),
        ],
        out_specs=pl.BlockSpec((None, tm, d), lambda b, j: (b, j, 0)),
        compiler_params=_cp(("parallel", "parallel")),
        name="merge_a",
    )(ya, o_f, o_b, p, y_f, y_b, p, p, p, p, hn_g, d_skip, w_glu, wa, wb, wc)


def _merge_b_kernel(m_ref, x_ref, mod_ref, wo_ref, wq_ref, x1_ref, h2_ref, q_ref):
    x1 = x_ref[...] + mod_ref[2:3, :] * _dot(m_ref[...], wo_ref[...])
    x1_ref[...] = x1
    r = lax.rsqrt(jnp.mean(x1 * x1, axis=-1, keepdims=True) + EPS)
    h2 = x1 * r * (1.0 + mod_ref[4:5, :]) + mod_ref[3:4, :]
    h2_ref[...] = h2
    q_ref[...] = _dot(h2.astype(BF16), wq_ref[...])


def _merge_b(merged, xall, mod, wo, wq, off, nctx):
    bsz, lo, d = merged.shape
    nblk = lo // TM
    dq = wq.shape[-1]
    out = lambda b, j: (b, j, 0)
    return pl.pallas_call(
        _merge_b_kernel,
        out_shape=(jax.ShapeDtypeStruct((bsz, lo, d), F32), jax.ShapeDtypeStruct((bsz, lo, d), F32),
                   jax.ShapeDtypeStruct((bsz, lo, dq), F32)),
        grid=(bsz, nblk),
        in_specs=[
            pl.BlockSpec((None, TM, d), out),
            pl.BlockSpec((None, TM, d), lambda b, j: (b, j + off, 0)),
            pl.BlockSpec((None, None, N_MOD, d), lambda b, j: (b, (j + off >= nctx).astype(jnp.int32), 0, 0)),
            _const_spec(wo.shape),
            _const_spec(wq.shape),
        ],
        out_specs=(pl.BlockSpec((None, TM, d), out), pl.BlockSpec((None, TM, d), out),
                   pl.BlockSpec((None, TM, dq), out)),
        compiler_params=_cp(("parallel", "parallel")),
        name="merge_b",
    )(merged, xall, mod, wo, wq)


def _topk_rows(s, row_id, k):
    slot = lax.broadcasted_iota(jnp.int32, (k, s.shape[1]), 0)
    vals = jnp.zeros((k, s.shape[1]), F32)
    ids = jnp.zeros((k, s.shape[1]), F32)
    for r in range(k):
        m = jnp.max(s, axis=0, keepdims=True)
        am = jnp.min(jnp.where(s == m, row_id, 1e9), axis=0, keepdims=True)
        vals = jnp.where(slot == r, m, vals)
        ids = jnp.where(slot == r, am, ids)
        s = jnp.where(row_id == am, -jnp.inf, s)
    return vals, ids


def _pair_candidates():
    k = PEER_TOPK
    pieces = [("a", 0, 0, 16), ("a", 1, 0, 8), ("a", 2, 0, 8), ("a", 3, 0, 8), ("a", 4, 0, 8),
              ("b", 0, 0, 8), ("b", 1, 0, 8), ("b", 0, 8, 16)]
    ids, bias, seen = [], [], set()
    for kind, fixed, lo, hi in pieces:
        for r in range(lo, hi):
            a, b = (fixed, r) if kind == "a" else (r, fixed)
            ok = (a + 1) * (b + 1) <= k and (a, b) not in seen
            if ok:
                seen.add((a, b))
            ids.append(float(a * k + b))
            bias.append(0.0 if ok else -math.inf)
    assert len(seen) == sum(k // (a + 1) for a in range(k))
    return pieces, ids, bias


def _route_kernel(q_ref, keys_ref, cid_ref, cbias_ref, idx_ref, g_ref):
    k = PEER_TOPK
    pieces, _, _ = _pair_candidates()
    key_id = lax.broadcasted_iota(jnp.int32, (PEER_NKEYS, q_ref.shape[0]), 0).astype(F32)
    for h in range(PEER_HEADS):
        sv, si = [], []
        for half in range(2):
            c0 = (h * 2 + half) * PEER_NKEYS
            qh = q_ref[:, c0:c0 + PEER_NKEYS].astype(BF16)
            s = _dot_nt(keys_ref[h, half], qh)
            v_, i_ = _topk_rows(s, key_id, k)
            sv.append(v_)
            si.append(i_)
        cand = jnp.concatenate(
            [sv[0][f:f + 1, :] + sv[1][lo:hi, :] if kind == "a" else sv[0][lo:hi, :] + sv[1][f:f + 1, :]
             for kind, f, lo, hi in pieces], axis=0) + cbias_ref[...]
        fv, fi = _topk_rows(cand, cid_ref[...], k)
        fa = jnp.floor(fi * (1.0 / k))
        fb = fi - fa * k
        i1 = jnp.zeros_like(fi)
        i2 = jnp.zeros_like(fi)
        for a in range(k):
            i1 = jnp.where(fa == a, si[0][a:a + 1, :], i1)
            i2 = jnp.where(fb == a, si[1][a:a + 1, :], i2)
        e = jnp.exp(fv - fv[0:1, :])
        g_ref[h * k:(h + 1) * k, :] = e / jnp.sum(e, axis=0, keepdims=True)
        idx_ref[h * k:(h + 1) * k, :] = (i1 * PEER_NKEYS + i2).astype(jnp.int32)


def _route(q, keys):
    t, dq = q.shape
    tm = LANE
    nsel = PEER_HEADS * PEER_TOPK
    _, ids, bias = _pair_candidates()
    cid = jnp.broadcast_to(jnp.asarray(ids, F32)[:, None], (len(ids), tm))
    cbias = jnp.broadcast_to(jnp.asarray(bias, F32)[:, None], (len(bias), tm))
    return pl.pallas_call(
        _route_kernel,
        out_shape=(jax.ShapeDtypeStruct((nsel, t), jnp.int32), jax.ShapeDtypeStruct((nsel, t), F32)),
        grid=(t // tm,),
        in_specs=[pl.BlockSpec((tm, dq), lambda i: (i, 0)), _const_spec(keys.shape),
                  _const_spec(cid.shape), _const_spec(cbias.shape)],
        out_specs=(pl.BlockSpec((nsel, tm), lambda i: (0, i)), pl.BlockSpec((nsel, tm), lambda i: (0, i))),
        compiler_params=_cp(("parallel",)),
        name="peer_route",
    )(q, keys, cid, cbias)


PEER_TB = 64
PEER_SLOTS = 4
PEER_AHEAD = PEER_SLOTS - 1
HALF_SUB = 8
HI_MASK = -65536


def _pack_expert_table(u, v):
    n_exp, d = u.shape
    tb = 256
    assert d == 2 * HALF_SUB * LANE and n_exp % tb == 0

    def pack(x):
        bits = lax.bitcast_convert_type(x.astype(BF16).astype(F32), jnp.int32)
        word = (bits[:, :d // 2] & HI_MASK) | lax.shift_right_logical(bits[:, d // 2:], 16)
        return word.reshape(x.shape[0], HALF_SUB, LANE)

    def body(u_ref, v_ref, o_ref):
        o_ref[:, 0:HALF_SUB, :] = pack(u_ref[...])
        o_ref[:, HALF_SUB:2 * HALF_SUB, :] = pack(v_ref[...])

    return pl.pallas_call(
        body,
        out_shape=jax.ShapeDtypeStruct((n_exp, 2 * HALF_SUB, LANE), jnp.int32),
        grid=(n_exp // tb,),
        in_specs=[pl.BlockSpec((tb, d), lambda i: (i, 0)), pl.BlockSpec((tb, d), lambda i: (i, 0))],
        out_specs=pl.BlockSpec((tb, 2 * HALF_SUB, LANE), lambda i: (i, 0, 0)),
        compiler_params=_cp(("parallel",)),
        name="pack_experts",
    )(u, v)


def _unpack_pair(word):
    return lax.bitcast_convert_type(word & HI_MASK, F32), lax.bitcast_convert_type(word << 16, F32)


def _experts_kernel(idx_ref, g_ref, h_ref, x_ref, mod_ref, fin_ref, tab_ref, o_ref, *scratch, final_norm):
    bufs, sem = scratch[:PEER_SLOTS], scratch[PEER_SLOTS]
    nsel = PEER_HEADS * PEER_TOPK
    per = nsel // (2 * HALF_SUB)
    step = pl.program_id(0)

    def issue(t, slot, j0=0, j1=nsel):
        for j in range(j0, j1):
            e = idx_ref[0, t * nsel + j]
            pltpu.make_async_copy(tab_ref.at[e], bufs[slot].at[:, j, :], sem.at[slot]).start(priority=j % 2)

    def wait(slot):
        pltpu.make_async_copy(bufs[slot], bufs[slot], sem.at[slot]).wait()

    @pl.when(step == 0)
    def _():
        for s in range(PEER_AHEAD):
            issue(s, s)

    eye = lax.broadcasted_iota(jnp.int32, (nsel, nsel), 0) == lax.broadcasted_iota(jnp.int32, (nsel, nsel), 1)

    def group(t4, c):
        for s in range(PEER_SLOTS):
            t = t4 * PEER_SLOTS + s
            wait(s)
            nxt = (s + PEER_AHEAD) % PEER_SLOTS
            row = pl.ds(t, 1)
            hrow = h_ref[row, :]
            buf = bufs[s]
            acc = None
            for cc in range(HALF_SUB):
                u_hi, u_lo = _unpack_pair(buf[cc])
                term = (u_hi * hrow[:, cc * LANE:(cc + 1) * LANE]
                        + u_lo * hrow[:, (HALF_SUB + cc) * LANE:(HALF_SUB + cc + 1) * LANE])
                acc = term if acc is None else acc + term
                issue(t + PEER_AHEAD, nxt, cc * per, (cc + 1) * per)
            act = jnp.sum(acc, axis=-1, keepdims=True)
            gcol = jnp.sum(jnp.where(eye, g_ref[row, :], 0.0), axis=-1, keepdims=True)
            w = _gelu(act) * gcol
            outs = [None] * (2 * HALF_SUB)
            for cc in range(HALF_SUB):
                v_hi, v_lo = _unpack_pair(buf[HALF_SUB + cc])
                outs[cc] = jnp.sum(w * v_hi, axis=0, keepdims=True)
                outs[HALF_SUB + cc] = jnp.sum(w * v_lo, axis=0, keepdims=True)
                issue(t + PEER_AHEAD, nxt, (HALF_SUB + cc) * per, (HALF_SUB + cc + 1) * per)
            o_ref[row, :] = x_ref[row, :] + mod_ref[5:6, :] * jnp.concatenate(outs, axis=-1)
        return c

    lax.fori_loop(0, PEER_TB // PEER_SLOTS, group, 0)

    if final_norm:
        y = o_ref[...]
        o_ref[...] = y * lax.rsqrt(jnp.mean(y * y, axis=-1, keepdims=True) + EPS) * fin_ref[...]

    @pl.when(step == pl.num_programs(0) - 1)
    def _():
        for s in range(PEER_AHEAD):
            wait(s)


def _experts(idx, g, h2, x1, mod, fin_g, tab, off, nctx, final_norm):
    t_all, d = h2.shape
    bsz = mod.shape[0]
    nsel = idx.shape[-1]
    tb = PEER_TB
    nb = t_all // tb
    per_b = nb // bsz
    per = TM // tb
    assert d == 2 * HALF_SUB * LANE and tb % PEER_SLOTS == 0
    main = idx.reshape(nb, tb * nsel)
    idx_ext = jnp.concatenate([main, jnp.roll(main, -1, axis=0)[:, :PEER_AHEAD * nsel]], axis=1)
    idx_ext = idx_ext.reshape(nb, 1, (tb + PEER_AHEAD) * nsel)
    tokb = lambda i: (i, 0)

    def mod_map(i):
        return (i // per_b, ((i % per_b) // per + off >= nctx).astype(jnp.int32), 0, 0)

    return pl.pallas_call(
        functools.partial(_experts_kernel, final_norm=final_norm),
        out_shape=jax.ShapeDtypeStruct((t_all, d), F32),
        grid=(nb,),
        in_specs=[
            pl.BlockSpec((None, 1, (tb + PEER_AHEAD) * nsel), lambda i: (i, 0, 0), memory_space=pltpu.SMEM),
            pl.BlockSpec((tb, nsel), tokb),
            pl.BlockSpec((tb, d), tokb),
            pl.BlockSpec((tb, d), tokb),
            pl.BlockSpec((None, None, N_MOD, d), mod_map),
            _const_spec((1, d)),
            pl.BlockSpec(memory_space=pl.ANY),
        ],
        out_specs=pl.BlockSpec((tb, d), tokb),
        scratch_shapes=[pltpu.VMEM((2 * HALF_SUB, nsel, LANE), jnp.int32) for _ in range(PEER_SLOTS)]
        + [pltpu.SemaphoreType.DMA((PEER_SLOTS,))],
        compiler_params=_cp(("arbitrary",)),
        name="peer_experts",
    )(idx_ext, g, h2, x1, mod, fin_g, tab)


def _pos_embed_2d(n, d):
    n_rows = n // GRID_W
    quarter = d // 4
    omega = 1.0 / (POS_THETA ** (jnp.arange(quarter, dtype=F32) / quarter))

    def axis_code(length):
        ang = jnp.arange(length, dtype=F32)[:, None] * omega[None]
        return jnp.concatenate([jnp.sin(ang), jnp.cos(ang)], axis=-1)

    er = jnp.broadcast_to(axis_code(n_rows)[:, None], (n_rows, GRID_W, d // 2))
    ec = jnp.broadcast_to(axis_code(GRID_W)[None], (n_rows, GRID_W, d // 2))
    return jnp.concatenate([er, ec], axis=-1).reshape(n, d)


def kernel(x, c, ctx, c_ctx, w_mod, b_mod, w_in, gmlp_ln_g, gmlp_ln_b, gmlp_w_s, gmlp_b_s, hgrn_lb_logits, hgrn_norm_g, s5_lam_re, s5_lam_im, s5_log_step, s5_b_re, s5_b_im, s5_c_re, s5_c_im, s5_d, s5_w_glu, w_branch_a, w_branch_b, w_branch_c, w_out, peer_w_q, peer_sub_keys, peer_u, peer_v, final_norm_g):
    bsz, n, d = x.shape
    ctx_len = ctx.shape[1]
    depth = w_mod.shape[0]
    nctx = ctx_len // TM
    da = gmlp_ln_g.shape[-1]
    db = hgrn_norm_g.shape[-1]
    dc = s5_d.shape[-1]
    assert n % TM == 0 and ctx_len % TM == 0 and bsz % 8 == 0

    xall = _embed(x, ctx, _pos_embed_2d(n, d).astype(x.dtype))

    rows = -(-(bsz + 1) // 8) * 8
    cc = jnp.zeros((rows, d), F32).at[0].set(c_ctx).at[1:bsz + 1].set(c)
    mod_all = _modulation(cc, w_mod, b_mod)

    gamma = jax.nn.softmax(hgrn_lb_logits.astype(F32), axis=0)
    lower_bounds = jnp.cumsum(gamma, axis=0) - gamma[0:1]

    for l in range(depth):
        last = l == depth - 1
        off = nctx if last else 0
        mod_c = jnp.broadcast_to(mod_all[l, 0][None], (bsz, N_MOD * d))
        mod = jnp.stack([mod_c, mod_all[l, 1:bsz + 1]], axis=1).reshape(bsz, 2, N_MOD, d)

        cu0 = 2 * da + 5 * db
        w_pad = jnp.concatenate([w_in[l][:, :cu0 + dc], jnp.zeros((d, da - dc), F32), w_in[l][:, cu0 + dc:]],
                                axis=1).astype(BF16)
        p = _in_proj(xall, mod, w_pad, ctx_len)

        bias = jnp.repeat(gmlp_b_s[l].T, da // G_A, axis=1)
        ya = _gmlp(p, gmlp_ln_g[l].reshape(1, da), gmlp_ln_b[l].reshape(1, da), gmlp_w_s[l].astype(BF16), bias)

        o_f = _hgrn(p, lower_bounds[l, 0].reshape(1, db), False, nctx)
        o_b = _hgrn(p, lower_bounds[l, 1].reshape(1, db), True, nctx)

        ys = []
        for r in range(2):
            prm = _s5_params(s5_lam_re[l, r], s5_lam_im[l, r], s5_log_step[l, r], s5_b_re[l, r], s5_b_im[l, r],
                             s5_c_re[l, r], s5_c_im[l, r])
            ys.append(_s5(p, (cu0 // da, da), *prm, rev=bool(r), ctx_len=ctx_len))

        merged = _merge_a(ya, o_f, o_b, p, ys[0], ys[1], hgrn_norm_g[l].reshape(1, db), s5_d[l].reshape(1, dc),
                          s5_w_glu[l].astype(BF16), w_branch_a[l].astype(BF16), w_branch_b[l].astype(BF16),
                          w_branch_c[l].astype(BF16), off)
        x1, h2, q = _merge_b(merged, xall, mod, w_out[l].astype(BF16), peer_w_q[l].astype(BF16), off, nctx)

        lo = x1.shape[1]
        idx_t, g_t = _route(q.reshape(bsz * lo, -1), peer_sub_keys[l].astype(BF16))
        tab = _pack_expert_table(peer_u[l], peer_v[l])
        xall = _experts(idx_t.T, g_t.T, h2.reshape(bsz * lo, d), x1.reshape(bsz * lo, d), mod,
                        final_norm_g.astype(F32).reshape(1, d), tab, off, nctx, final_norm=last)
        xall = xall.reshape(bsz, lo, d)

    return xall
```

```python
import functools
import math

import jax
import jax.numpy as jnp
from jax import lax
from jax.experimental import pallas as pl
from jax.experimental.pallas import tpu as pltpu

F32 = jnp.float32
BF16 = jnp.bfloat16

EPS = 1e-6
GRID_W = 64
POS_THETA = 10000.0
N_MOD = 6
G_A = 8
GMLP_CHUNK = 128
H_B = 8
HGRN_CHUNK = 32
HGRN_HEADS_PER_BODY = 8
S5_IN = 16
S5_STATE = 64
S5_SLAB_G = 8
PEER_HEADS = 8
PEER_NKEYS = 128
PEER_TOPK = 16

TM = 256
LANE = 128
SUB = 8
VMEM_LIMIT = 56 * 1024 * 1024


def _cp(sem, vmem=VMEM_LIMIT):
    return pltpu.CompilerParams(dimension_semantics=sem, vmem_limit_bytes=vmem)


def _const_spec(shape):
    nd = len(shape)
    return pl.BlockSpec(shape, lambda *_: (0,) * nd, pipeline_mode=pl.Buffered(1))


def _gelu(x):
    return x * (lax.erf(x * (1.0 / math.sqrt(2.0))) + 1.0) * 0.5


def _sigmoid(x):
    return jax.nn.sigmoid(x)


def _dot(a, b):
    return jnp.dot(a, b, preferred_element_type=F32)


def _dot_nt(a, b):
    return lax.dot_general(a, b, (((1,), (1,)), ((), ())), preferred_element_type=F32)


def _dot_tn(a, b):
    return lax.dot_general(a, b, (((0,), (0,)), ((), ())), preferred_element_type=F32)


def _embed_kernel(ctx_ref, x_ref, pos_ref, o_ref, *, nctx):
    j = pl.program_id(1)

    @pl.when(j < nctx)
    def _():
        o_ref[...] = ctx_ref[...]

    @pl.when(j >= nctx)
    def _():
        o_ref[...] = x_ref[...] + pos_ref[...]


def _embed(x, ctx, pos):
    bsz, n, d = x.shape
    nctx = ctx.shape[1] // TM
    nblk = nctx + n // TM
    return pl.pallas_call(
        functools.partial(_embed_kernel, nctx=nctx),
        out_shape=jax.ShapeDtypeStruct((bsz, nblk * TM, d), x.dtype),
        grid=(bsz, nblk),
        in_specs=[
            pl.BlockSpec((None, TM, d), lambda b, j: (b, jnp.minimum(j, nctx - 1), 0)),
            pl.BlockSpec((None, TM, d), lambda b, j: (b, jnp.maximum(j - nctx, 0), 0)),
            pl.BlockSpec((TM, d), lambda b, j: (jnp.maximum(j - nctx, 0), 0)),
        ],
        out_specs=pl.BlockSpec((None, TM, d), lambda b, j: (b, j, 0)),
        compiler_params=_cp(("parallel", "arbitrary")),
        name="embed",
    )(ctx, x, pos)


def _mod_kernel(c_ref, w_ref, b_ref, o_ref):
    c = c_ref[...]
    s = c * _sigmoid(c)
    o_ref[...] = _dot(s.astype(BF16), w_ref[...].astype(BF16)) + b_ref[...]


def _modulation(cc, w_mod, b_mod):
    depth, d, nd = w_mod.shape
    rows = cc.shape[0]
    tn = 1024
    return pl.pallas_call(
        _mod_kernel,
        out_shape=jax.ShapeDtypeStruct((depth, rows, nd), F32),
        grid=(depth, nd // tn),
        in_specs=[
            pl.BlockSpec((rows, d), lambda l, n: (0, 0)),
            pl.BlockSpec((None, d, tn), lambda l, n: (l, 0, n)),
            pl.BlockSpec((None, 1, tn), lambda l, n: (l, 0, n)),
        ],
        out_specs=pl.BlockSpec((None, rows, tn), lambda l, n: (l, 0, n)),
        compiler_params=_cp(("parallel", "arbitrary")),
        name="modulation",
    )(cc, w_mod, b_mod.reshape(depth, 1, nd))


def _inproj_kernel(x_ref, mod_ref, w_ref, o_ref, h_scr, *, tm, ctx_rows):
    j = pl.program_id(1)

    @pl.when(pl.program_id(2) == 0)
    def _():
        x = x_ref[...]
        r = lax.rsqrt(jnp.mean(x * x, axis=-1, keepdims=True) + EPS)
        row = j * tm + lax.broadcasted_iota(jnp.int32, (tm, 1), 0)
        is_ctx = row < ctx_rows
        sh = jnp.where(is_ctx, mod_ref[0, 0:1, :], mod_ref[1, 0:1, :])
        sc = jnp.where(is_ctx, mod_ref[0, 1:2, :], mod_ref[1, 1:2, :])
        h_scr[...] = (x * r * (1.0 + sc) + sh).astype(BF16)

    o_ref[...] = _dot(h_scr[...], w_ref[...])


def _in_proj(xall, mod, w_pad, ctx_rows):
    bsz, l, d = xall.shape
    ncol = w_pad.shape[1]
    tm = 768 if l % 768 == 0 else TM
    tn = 1024
    return pl.pallas_call(
        functools.partial(_inproj_kernel, tm=tm, ctx_rows=ctx_rows),
        out_shape=jax.ShapeDtypeStruct((bsz, l, ncol), F32),
        grid=(bsz, l // tm, ncol // tn),
        in_specs=[
            pl.BlockSpec((None, tm, d), lambda b, j, n: (b, j, 0)),
            pl.BlockSpec((None, 2, N_MOD, d), lambda b, j, n: (b, 0, 0, 0)),
            pl.BlockSpec((d, tn), lambda b, j, n: (0, n)),
        ],
        out_specs=pl.BlockSpec((None, tm, tn), lambda b, j, n: (b, j, n)),
        scratch_shapes=[pltpu.VMEM((tm, d), BF16)],
        compiler_params=_cp(("parallel", "parallel", "arbitrary")),
        name="in_proj",
    )(xall, mod, w_pad)


def _gmlp_kernel(u_ref, v_ref, lng_ref, lnb_ref, ws_ref, bias_ref, o_ref):
    v = _gelu(v_ref[...])
    mu = jnp.mean(v, axis=-1, keepdims=True)
    vc = v - mu
    var = jnp.mean(vc * vc, axis=-1, keepdims=True)
    v = (vc * lax.rsqrt(var + EPS) * lng_ref[...] + lnb_ref[...]).astype(BF16)
    for c in range(TM // GMLP_CHUNK):
        rows = slice(c * GMLP_CHUNK, (c + 1) * GMLP_CHUNK)
        for g in range(G_A):
            cols = slice(g * LANE, (g + 1) * LANE)
            mixed = _dot(ws_ref[g], v[rows, cols]) + bias_ref[:, cols]
            o_ref[rows, cols] = _gelu(u_ref[rows, cols]) * mixed


def _gmlp(p, ln_g, ln_b, w_s, bias):
    bsz, l, _ = p.shape
    da = ln_g.shape[-1]
    return pl.pallas_call(
        _gmlp_kernel,
        out_shape=jax.ShapeDtypeStruct((bsz, l, da), F32),
        grid=(bsz, l // TM),
        in_specs=[
            pl.BlockSpec((None, TM, da), lambda b, j: (b, j, 0)),
            pl.BlockSpec((None, TM, da), lambda b, j: (b, j, 1)),
            _const_spec((1, da)),
            _const_spec((1, da)),
            _const_spec((G_A, GMLP_CHUNK, GMLP_CHUNK)),
            _const_spec((GMLP_CHUNK, da)),
        ],
        out_specs=pl.BlockSpec((None, TM, da), lambda b, j: (b, j, 0)),
        compiler_params=_cp(("parallel", "parallel")),
        name="gmlp",
    )(p, p, ln_g, ln_b, w_s, bias)


def _scan_block(p, rev, nctx, nblk):
    if not rev:
        return p
    return jnp.where(p < nctx, nctx - 1 - p, nblk - 1 - (p - nctx))


def _hgrn_kernel(q_ref, f_ref, i_ref, lb_ref, o_ref, st_ref, *, rev):
    c_sz = HGRN_CHUNK
    nchunk = TM // c_sz

    @pl.when(pl.program_id(1) == 0)
    def _():
        st_ref[...] = jnp.zeros_like(st_ref)

    nsb = c_sz // SUB
    rowi = lax.broadcasted_iota(jnp.int32, (c_sz, LANE), 0)
    row8 = lax.broadcasted_iota(jnp.int32, (SUB, LANE), 0)
    rowa = lax.broadcasted_iota(jnp.int32, (c_sz, c_sz), 0)

    hpb = HGRN_HEADS_PER_BODY
    ngrp = H_B // hpb

    def chunk_body(it, carry):
        ci, hg = it // ngrp, it % ngrp
        c = (nchunk - 1 - ci) if rev else ci
        rows = pl.ds(pl.multiple_of(c * c_sz, c_sz), c_sz)
        for hl in range(hpb):
            if ngrp == 1:
                hh, cols = hl, slice(hl * LANE, (hl + 1) * LANE)
            else:
                hh = hg * hpb + hl
                cols = pl.ds(pl.multiple_of(hh * LANE, LANE), LANE)
            lbh = lb_ref[:, cols]
            st = st_ref[hh]
            qr = q_ref[rows, cols]
            q = qr * _sigmoid(qr)
            fg = lbh + (1.0 - lbh) * _sigmoid(f_ref[rows, cols])
            k = 1.0 - fg
            v = i_ref[rows, cols]
            b = jnp.log(fg)
            for s in (1, 2, 4, 8, 16):
                if rev:
                    b = b + jnp.where(rowi < c_sz - s, pltpu.roll(b, c_sz - s, 0), 0.0)
                else:
                    b = b + jnp.where(rowi >= s, pltpu.roll(b, s, 0), 0.0)
            b_last = b[0:1, :] if rev else b[c_sz - 1:c_sz, :]
            o = _dot_nt((q * jnp.exp(b)).astype(BF16), st.astype(BF16))
            diag = []
            for jb in range(nsb):
                r8 = slice(jb * SUB, (jb + 1) * SUB)
                bj, qj, kj, vj = b[r8], q[r8], k[r8], v[r8]
                oj = jnp.zeros((SUB, LANE), F32)
                for s in range(SUB):
                    valid = (row8 <= s) if rev else (row8 >= s)
                    dec = jnp.exp(bj - bj[s:s + 1, :])
                    pr = jnp.where(valid, qj * (kj[s:s + 1, :] * dec), 0.0)
                    oj = oj + jnp.sum(pr, axis=-1, keepdims=True) * vj[s:s + 1, :]
                diag.append(oj)
            o = o + jnp.concatenate(diag, axis=0)
            att = jnp.zeros((c_sz, c_sz), F32)
            for jb in (range(1, nsb) if rev else range(nsb - 1)):
                edge = jb * SUB if rev else jb * SUB + SUB - 1
                b_ref_row = b[edge:edge + 1, :]
                in_blk = (rowi >= jb * SUB) & (rowi < (jb + 1) * SUB)
                kt = jnp.where(in_blk, k * jnp.exp(jnp.minimum(b_ref_row - b, 0.0)), 0.0)
                qq = q * jnp.exp(jnp.minimum(b - b_ref_row, 0.0))
                a = _dot_nt(qq.astype(BF16), kt.astype(BF16))
                t_ok = (rowa < jb * SUB) if rev else (rowa >= (jb + 1) * SUB)
                att = att + jnp.where(t_ok, a, 0.0)
            o = o + _dot(att.astype(BF16), v.astype(BF16))
            o_ref[rows, cols] = o
            kk = k * jnp.exp(b_last - b)
            st_ref[hh] = st * jnp.exp(b_last) + _dot_tn(v.astype(BF16), kk.astype(BF16))
        return carry

    lax.fori_loop(0, nchunk * ngrp, chunk_body, 0)


def _hgrn(p, lb, rev, nctx):
    bsz, l, _ = p.shape
    db = lb.shape[-1]
    nblk = l // TM
    blk = functools.partial(_scan_block, rev=rev, nctx=nctx, nblk=nblk)
    fcol = 4 if rev else 3
    return pl.pallas_call(
        functools.partial(_hgrn_kernel, rev=rev),
        out_shape=jax.ShapeDtypeStruct((bsz, l, db), F32),
        grid=(bsz, nblk),
        in_specs=[
            pl.BlockSpec((None, TM, db), lambda b, p_: (b, blk(p_), 2)),
            pl.BlockSpec((None, TM, db), lambda b, p_: (b, blk(p_), fcol)),
            pl.BlockSpec((None, TM, db), lambda b, p_: (b, blk(p_), 5)),
            _const_spec((1, db)),
        ],
        out_specs=pl.BlockSpec((None, TM, db), lambda b, p_: (b, blk(p_), 0)),
        scratch_shapes=[pltpu.VMEM((H_B, LANE, LANE), F32)],
        compiler_params=_cp(("parallel", "arbitrary")),
        name="hgrn_bwd" if rev else "hgrn_fwd",
    )(p, p, p, lb)


def _s5_kernel(u_ref, bb_ref, lam_ref, cc_ref, o_ref, bu_scr, st_ref, *, rev, tb, bsz, nslab):
    half = S5_SLAB_G * S5_STATE

    @pl.when(pl.program_id(0) == 0)
    def _():
        st_ref[...] = jnp.zeros_like(st_ref)

    for j in range(nslab):
        uj = jnp.swapaxes(u_ref[:, :, j * LANE:(j + 1) * LANE], 0, 1).reshape(tb * bsz, LANE).astype(BF16)
        bu_scr[...] = _dot(uj, bb_ref[j])
        lre = jnp.broadcast_to(lam_ref[j, 0:1, :], (bsz, half))
        lim = jnp.broadcast_to(lam_ref[j, 1:2, :], (bsz, half))

        def step(i, carry):
            xre, xim = carry
            t = (tb - 1 - i) if rev else i
            rows = pl.ds(pl.multiple_of(t * bsz, bsz), bsz)
            nre = lre * xre - lim * xim + bu_scr[rows, 0:half]
            nim = lre * xim + lim * xre + bu_scr[rows, half:2 * half]
            bu_scr[rows, 0:half] = nre
            bu_scr[rows, half:2 * half] = nim
            return nre, nim

        xre, xim = lax.fori_loop(0, tb, step, (st_ref[j, :, 0:half], st_ref[j, :, half:2 * half]), unroll=4)
        st_ref[j, :, 0:half] = xre
        st_ref[j, :, half:2 * half] = xim
        y = _dot(bu_scr[...].astype(BF16), cc_ref[j])
        o_ref[:, :, j * LANE:(j + 1) * LANE] = jnp.swapaxes(y.reshape(tb, bsz, LANE), 0, 1)


def _s5(p, ucol, bbig, lam, cbig, rev, ctx_len):
    bsz, l, _ = p.shape
    nslab = bbig.shape[0]
    dc = nslab * LANE
    tb = 128
    nblk, nctx = l // tb, ctx_len // tb
    blk = functools.partial(_scan_block, rev=rev, nctx=nctx, nblk=nblk)
    half = S5_SLAB_G * S5_STATE
    return pl.pallas_call(
        functools.partial(_s5_kernel, rev=rev, tb=tb, bsz=bsz, nslab=nslab),
        out_shape=jax.ShapeDtypeStruct((bsz, l, dc), F32),
        grid=(nblk,),
        in_specs=[
            pl.BlockSpec((bsz, tb, ucol[1]), lambda p_: (0, blk(p_), ucol[0])),
            _const_spec((nslab, LANE, 2 * half)),
            _const_spec((nslab, 2, half)),
            _const_spec((nslab, 2 * half, LANE)),
        ],
        out_specs=pl.BlockSpec((bsz, tb, dc), lambda p_: (0, blk(p_), 0)),
        scratch_shapes=[pltpu.VMEM((tb * bsz, 2 * half), F32), pltpu.VMEM((nslab, bsz, 2 * half), F32)],
        compiler_params=_cp(("arbitrary",)),
        name="s5_bwd" if rev else "s5_fwd",
    )(p, bbig, lam, cbig)


def _s5_params(lam_re, lam_im, log_step, b_re, b_im, c_re, c_im):
    g_c, p_s = lam_re.shape
    nslab = g_c // S5_SLAB_G
    lam = lax.complex(lam_re.astype(F32), lam_im.astype(F32))
    lam_dt = lam * jnp.exp(log_step.astype(F32))[:, None]
    lam_bar = jnp.exp(lam_dt)
    b_bar = ((lam_bar - 1.0) / lam)[:, :, None] * lax.complex(b_re.astype(F32), b_im.astype(F32))
    eye = jnp.eye(S5_SLAB_G, dtype=F32)

    def in_map(m):
        m = m.reshape(nslab, S5_SLAB_G, p_s, S5_IN)
        return jnp.einsum('jgpd,gh->jgdhp', m, eye).reshape(nslab, S5_SLAB_G * S5_IN, S5_SLAB_G * p_s)

    def out_map(m):
        m = m.reshape(nslab, S5_SLAB_G, S5_IN, p_s)
        return jnp.einsum('jgdp,gh->jgphd', m, eye).reshape(nslab, S5_SLAB_G * p_s, S5_SLAB_G * S5_IN)

    bbig = jnp.concatenate([in_map(jnp.real(b_bar)), in_map(jnp.imag(b_bar))], axis=-1).astype(BF16)
    cbig = jnp.concatenate([out_map(c_re.astype(F32)), out_map(-c_im.astype(F32))], axis=1).astype(BF16)
    lam_s = jnp.stack([jnp.real(lam_bar).reshape(nslab, -1), jnp.imag(lam_bar).reshape(nslab, -1)], axis=1)
    return bbig, lam_s, cbig


def _merge_a_kernel(ya_ref, of_ref, ob_ref, bg_ref, yf_ref, yb_ref, cu_ref, ga_ref, gb_ref, gc_ref,
                    hng_ref, dsk_ref, wglu_ref, wa_ref, wb_ref, wc_ref, o_ref):
    dc = dsk_ref.shape[-1]
    o = of_ref[...] + ob_ref[...]
    parts = []
    for hh in range(H_B):
        oh = o[:, hh * LANE:(hh + 1) * LANE]
        parts.append(oh * lax.rsqrt(jnp.mean(oh * oh, axis=-1, keepdims=True) + EPS))
    bg = bg_ref[...]
    yb = jnp.concatenate(parts, axis=-1) * hng_ref[...] * (bg * _sigmoid(bg))
    y = yf_ref[...] + yb_ref[...] + dsk_ref[...] * cu_ref[:, 0:dc]
    z = _dot(_gelu(y).astype(BF16), wglu_ref[...])
    yc = z[:, 0:dc] * _sigmoid(z[:, dc:2 * dc])
    merged = (_sigmoid(ga_ref[...]) * _dot(ya_ref[...].astype(BF16), wa_ref[...])
              + _sigmoid(gb_ref[...]) * _dot(yb.astype(BF16), wb_ref[...])
              + _sigmoid(gc_ref[...]) * _dot(yc.astype(BF16), wc_ref[...]))
    o_ref[...] = merged.astype(BF16)


def _merge_a(ya, o_f, o_b, p, y_f, y_b, hn_g, d_skip, w_glu, wa, wb, wc, off):
    bsz, l, da = ya.shape
    dc = y_f.shape[-1]
    d = wa.shape[-1]
    tm = 256
    o2 = off * (TM // tm)
    nblk = l // tm - o2
    tok = lambda b, j: (b, j + o2, 0)
    pcol = lambda c: (lambda b, j: (b, j + o2, c))
    return pl.pallas_call(
        _merge_a_kernel,
        out_shape=jax.ShapeDtypeStruct((bsz, nblk * tm, d), BF16),
        grid=(bsz, nblk),
        in_specs=[
            pl.BlockSpec((None, tm, da), tok),
            pl.BlockSpec((None, tm, da), tok),
            pl.BlockSpec((None, tm, da), tok),
            pl.BlockSpec((None, tm, da), pcol(6)),
            pl.BlockSpec((None, tm, dc), tok),
            pl.BlockSpec((None, tm, dc), tok),
            pl.BlockSpec((None, tm, da), pcol(7)),
            pl.BlockSpec((None, tm, d), pcol(4)),
            pl.BlockSpec((None, tm, d), pcol(5)),
            pl.BlockSpec((None, tm, d), pcol(6)),
            _const_spec((1, da)),
            _const_spec((1, dc)),
            _const_spec(w_glu.shape),
            _const_spec(wa.shape),
            _const_spec(wb.shape),
            _const_spec(wc.shape),
        ],
        out_specs=pl.BlockSpec((None, tm, d), lambda b, j: (b, j, 0)),
        compiler_params=_cp(("parallel", "parallel")),
        name="merge_a",
    )(ya, o_f, o_b, p, y_f, y_b, p, p, p, p, hn_g, d_skip, w_glu, wa, wb, wc)


def _merge_b_kernel(m_ref, x_ref, mod_ref, wo_ref, wq_ref, x1_ref, h2_ref, q_ref):
    x1 = x_ref[...] + mod_ref[2:3, :] * _dot(m_ref[...], wo_ref[...])
    x1_ref[...] = x1
    r = lax.rsqrt(jnp.mean(x1 * x1, axis=-1, keepdims=True) + EPS)
    h2 = x1 * r * (1.0 + mod_ref[4:5, :]) + mod_ref[3:4, :]
    h2_ref[...] = h2
    q_ref[...] = _dot(h2.astype(BF16), wq_ref[...])


def _merge_b(merged, xall, mod, wo, wq, off, nctx):
    bsz, lo, d = merged.shape
    nblk = lo // TM
    dq = wq.shape[-1]
    out = lambda b, j: (b, j, 0)
    return pl.pallas_call(
        _merge_b_kernel,
        out_shape=(jax.ShapeDtypeStruct((bsz, lo, d), F32), jax.ShapeDtypeStruct((bsz, lo, d), F32),
                   jax.ShapeDtypeStruct((bsz, lo, dq), F32)),
        grid=(bsz, nblk),
        in_specs=[
            pl.BlockSpec((None, TM, d), out),
            pl.BlockSpec((None, TM, d), lambda b, j: (b, j + off, 0)),
            pl.BlockSpec((None, None, N_MOD, d), lambda b, j: (b, (j + off >= nctx).astype(jnp.int32), 0, 0)),
            _const_spec(wo.shape),
            _const_spec(wq.shape),
        ],
        out_specs=(pl.BlockSpec((None, TM, d), out), pl.BlockSpec((None, TM, d), out),
                   pl.BlockSpec((None, TM, dq), out)),
        compiler_params=_cp(("parallel", "parallel")),
        name="merge_b",
    )(merged, xall, mod, wo, wq)


def _topk_rows(s, row_id, k):
    slot = lax.broadcasted_iota(jnp.int32, (k, s.shape[1]), 0)
    vals = jnp.zeros((k, s.shape[1]), F32)
    ids = jnp.zeros((k, s.shape[1]), F32)
    for r in range(k):
        m = jnp.max(s, axis=0, keepdims=True)
        am = jnp.min(jnp.where(s == m, row_id, 1e9), axis=0, keepdims=True)
        vals = jnp.where(slot == r, m, vals)
        ids = jnp.where(slot == r, am, ids)
        s = jnp.where(row_id == am, -jnp.inf, s)
    return vals, ids


def _pair_candidates():
    k = PEER_TOPK
    pieces = [("a", 0, 0, 16), ("a", 1, 0, 8), ("a", 2, 0, 8), ("a", 3, 0, 8), ("a", 4, 0, 8),
              ("b", 0, 0, 8), ("b", 1, 0, 8), ("b", 0, 8, 16)]
    ids, bias, seen = [], [], set()
    for kind, fixed, lo, hi in pieces:
        for r in range(lo, hi):
            a, b = (fixed, r) if kind == "a" else (r, fixed)
            ok = (a + 1) * (b + 1) <= k and (a, b) not in seen
            if ok:
                seen.add((a, b))
            ids.append(float(a * k + b))
            bias.append(0.0 if ok else -math.inf)
    assert len(seen) == sum(k // (a + 1) for a in range(k))
    return pieces, ids, bias


def _route_kernel(q_ref, keys_ref, cid_ref, cbias_ref, idx_ref, g_ref, it_scr, gt_scr):
    k = PEER_TOPK
    pieces, _, _ = _pair_candidates()
    key_id = lax.broadcasted_iota(jnp.int32, (PEER_NKEYS, q_ref.shape[0]), 0).astype(F32)
    for h in range(PEER_HEADS):
        sv, si = [], []
        for half in range(2):
            c0 = (h * 2 + half) * PEER_NKEYS
            qh = q_ref[:, c0:c0 + PEER_NKEYS].astype(BF16)
            s = _dot_nt(keys_ref[h, half], qh)
            v_, i_ = _topk_rows(s, key_id, k)
            sv.append(v_)
            si.append(i_)
        cand = jnp.concatenate(
            [sv[0][f:f + 1, :] + sv[1][lo:hi, :] if kind == "a" else sv[0][lo:hi, :] + sv[1][f:f + 1, :]
             for kind, f, lo, hi in pieces], axis=0) + cbias_ref[...]
        fv, fi = _topk_rows(cand, cid_ref[...], k)
        fa = jnp.floor(fi * (1.0 / k))
        fb = fi - fa * k
        i1 = jnp.zeros_like(fi)
        i2 = jnp.zeros_like(fi)
        for a in range(k):
            i1 = jnp.where(fa == a, si[0][a:a + 1, :], i1)
            i2 = jnp.where(fb == a, si[1][a:a + 1, :], i2)
        e = jnp.exp(fv - fv[0:1, :])
        gt_scr[h * k:(h + 1) * k, :] = e / jnp.sum(e, axis=0, keepdims=True)
        it_scr[h * k:(h + 1) * k, :] = i1 * PEER_NKEYS + i2
    g_ref[...] = gt_scr[...].T
    idx_ref[...] = it_scr[...].T.astype(jnp.int32)


def _route(q, keys):
    t, dq = q.shape
    tm = LANE
    nsel = PEER_HEADS * PEER_TOPK
    _, ids, bias = _pair_candidates()
    cid = jnp.broadcast_to(jnp.asarray(ids, F32)[:, None], (len(ids), tm))
    cbias = jnp.broadcast_to(jnp.asarray(bias, F32)[:, None], (len(bias), tm))
    return pl.pallas_call(
        _route_kernel,
        out_shape=(jax.ShapeDtypeStruct((t, nsel), jnp.int32), jax.ShapeDtypeStruct((t, nsel), F32)),
        grid=(t // tm,),
        in_specs=[pl.BlockSpec((tm, dq), lambda i: (i, 0)), _const_spec(keys.shape),
                  _const_spec(cid.shape), _const_spec(cbias.shape)],
        out_specs=(pl.BlockSpec((tm, nsel), lambda i: (i, 0)), pl.BlockSpec((tm, nsel), lambda i: (i, 0))),
        scratch_shapes=[pltpu.VMEM((nsel, tm), F32), pltpu.VMEM((nsel, tm), F32)],
        compiler_params=_cp(("parallel",)),
        name="peer_route",
    )(q, keys, cid, cbias)


PEER_TB = 64
PEER_SLOTS = 4
PEER_AHEAD = PEER_SLOTS - 1
HALF_SUB = 8
HI_MASK = -65536


def _pack_expert_table(u_all, v_all, layer):
    _, n_exp, d = u_all.shape
    tb = 256
    assert d == 2 * HALF_SUB * LANE and n_exp % tb == 0

    def pack(x):
        bits = lax.bitcast_convert_type(x.astype(BF16).astype(F32), jnp.int32)
        word = (bits[:, :d // 2] & HI_MASK) | lax.shift_right_logical(bits[:, d // 2:], 16)
        return word.reshape(x.shape[0], HALF_SUB, LANE)

    def body(u_ref, v_ref, o_ref):
        o_ref[:, 0:HALF_SUB, :] = pack(u_ref[...])
        o_ref[:, HALF_SUB:2 * HALF_SUB, :] = pack(v_ref[...])

    return pl.pallas_call(
        body,
        out_shape=jax.ShapeDtypeStruct((n_exp, 2 * HALF_SUB, LANE), jnp.int32),
        grid=(n_exp // tb,),
        in_specs=[pl.BlockSpec((None, tb, d), lambda i: (layer, i, 0)),
                  pl.BlockSpec((None, tb, d), lambda i: (layer, i, 0))],
        out_specs=pl.BlockSpec((tb, 2 * HALF_SUB, LANE), lambda i: (i, 0, 0)),
        compiler_params=_cp(("parallel",)),
        name="pack_experts",
    )(u_all, v_all)


def _unpack_pair(word):
    return lax.bitcast_convert_type(word & HI_MASK, F32), lax.bitcast_convert_type(word << 16, F32)


def _experts_kernel(idx_ref, g_ref, h_ref, x_ref, mod_ref, fin_ref, tab_ref, o_ref, *scratch, final_norm):
    bufs, sem = scratch[:PEER_SLOTS], scratch[PEER_SLOTS]
    nsel = PEER_HEADS * PEER_TOPK
    per = nsel // (2 * HALF_SUB)
    step = pl.program_id(0)

    def issue(t, slot, j0=0, j1=nsel):
        for j in range(j0, j1):
            e = idx_ref[0, t * nsel + j]
            pltpu.make_async_copy(tab_ref.at[e], bufs[slot].at[:, j, :], sem.at[slot]).start(priority=j % 2)

    def wait(slot):
        pltpu.make_async_copy(bufs[slot], bufs[slot], sem.at[slot]).wait()

    @pl.when(step == 0)
    def _():
        for s in range(PEER_AHEAD):
            issue(s, s)

    eye = lax.broadcasted_iota(jnp.int32, (nsel, nsel), 0) == lax.broadcasted_iota(jnp.int32, (nsel, nsel), 1)

    def group(t4, c):
        for s in range(PEER_SLOTS):
            t = t4 * PEER_SLOTS + s
            wait(s)
            nxt = (s + PEER_AHEAD) % PEER_SLOTS
            row = pl.ds(t, 1)
            hrow = h_ref[row, :]
            buf = bufs[s]
            acc = None
            for cc in range(HALF_SUB):
                u_hi, u_lo = _unpack_pair(buf[cc])
                term = (u_hi * hrow[:, cc * LANE:(cc + 1) * LANE]
                        + u_lo * hrow[:, (HALF_SUB + cc) * LANE:(HALF_SUB + cc + 1) * LANE])
                acc = term if acc is None else acc + term
                issue(t + PEER_AHEAD, nxt, cc * per, (cc + 1) * per)
            act = jnp.sum(acc, axis=-1, keepdims=True)
            gcol = jnp.sum(jnp.where(eye, g_ref[row, :], 0.0), axis=-1, keepdims=True)
            w = _gelu(act) * gcol
            outs = [None] * (2 * HALF_SUB)
            for cc in range(HALF_SUB):
                v_hi, v_lo = _unpack_pair(buf[HALF_SUB + cc])
                outs[cc] = jnp.sum(w * v_hi, axis=0, keepdims=True)
                outs[HALF_SUB + cc] = jnp.sum(w * v_lo, axis=0, keepdims=True)
                issue(t + PEER_AHEAD, nxt, (HALF_SUB + cc) * per, (HALF_SUB + cc + 1) * per)
            o_ref[row, :] = x_ref[row, :] + mod_ref[5:6, :] * jnp.concatenate(outs, axis=-1)
        return c

    lax.fori_loop(0, PEER_TB // PEER_SLOTS, group, 0)

    if final_norm:
        y = o_ref[...]
        o_ref[...] = y * lax.rsqrt(jnp.mean(y * y, axis=-1, keepdims=True) + EPS) * fin_ref[...]

    @pl.when(step == pl.num_programs(0) - 1)
    def _():
        for s in range(PEER_AHEAD):
            wait(s)


def _experts(idx, g, h2, x1, mod, fin_g, tab, off, nctx, final_norm):
    t_all, d = h2.shape
    bsz = mod.shape[0]
    nsel = idx.shape[-1]
    tb = PEER_TB
    nb = t_all // tb
    per_b = nb // bsz
    per = TM // tb
    assert d == 2 * HALF_SUB * LANE and tb % PEER_SLOTS == 0
    main = idx.reshape(nb, tb * nsel)
    idx_ext = jnp.concatenate([main, jnp.roll(main, -1, axis=0)[:, :PEER_AHEAD * nsel]], axis=1)
    idx_ext = idx_ext.reshape(nb, 1, (tb + PEER_AHEAD) * nsel)
    tokb = lambda i: (i, 0)

    def mod_map(i):
        return (i // per_b, ((i % per_b) // per + off >= nctx).astype(jnp.int32), 0, 0)

    return pl.pallas_call(
        functools.partial(_experts_kernel, final_norm=final_norm),
        out_shape=jax.ShapeDtypeStruct((t_all, d), F32),
        grid=(nb,),
        in_specs=[
            pl.BlockSpec((None, 1, (tb + PEER_AHEAD) * nsel), lambda i: (i, 0, 0), memory_space=pltpu.SMEM),
            pl.BlockSpec((tb, nsel), tokb),
            pl.BlockSpec((tb, d), tokb),
            pl.BlockSpec((tb, d), tokb),
            pl.BlockSpec((None, None, N_MOD, d), mod_map),
            _const_spec((1, d)),
            pl.BlockSpec(memory_space=pl.ANY),
        ],
        out_specs=pl.BlockSpec((tb, d), tokb),
        scratch_shapes=[pltpu.VMEM((2 * HALF_SUB, nsel, LANE), jnp.int32) for _ in range(PEER_SLOTS)]
        + [pltpu.SemaphoreType.DMA((PEER_SLOTS,))],
        compiler_params=_cp(("arbitrary",)),
        name="peer_experts",
    )(idx_ext, g, h2, x1, mod, fin_g, tab)


def _pos_embed_2d(n, d):
    n_rows = n // GRID_W
    quarter = d // 4
    omega = 1.0 / (POS_THETA ** (jnp.arange(quarter, dtype=F32) / quarter))

    def axis_code(length):
        ang = jnp.arange(length, dtype=F32)[:, None] * omega[None]
        return jnp.concatenate([jnp.sin(ang), jnp.cos(ang)], axis=-1)

    er = jnp.broadcast_to(axis_code(n_rows)[:, None], (n_rows, GRID_W, d // 2))
    ec = jnp.broadcast_to(axis_code(GRID_W)[None], (n_rows, GRID_W, d // 2))
    return jnp.concatenate([er, ec], axis=-1).reshape(n, d)


def kernel(x, c, ctx, c_ctx, w_mod, b_mod, w_in, gmlp_ln_g, gmlp_ln_b, gmlp_w_s, gmlp_b_s, hgrn_lb_logits, hgrn_norm_g, s5_lam_re, s5_lam_im, s5_log_step, s5_b_re, s5_b_im, s5_c_re, s5_c_im, s5_d, s5_w_glu, w_branch_a, w_branch_b, w_branch_c, w_out, peer_w_q, peer_sub_keys, peer_u, peer_v, final_norm_g):
    bsz, n, d = x.shape
    ctx_len = ctx.shape[1]
    depth = w_mod.shape[0]
    nctx = ctx_len // TM
    da = gmlp_ln_g.shape[-1]
    db = hgrn_norm_g.shape[-1]
    dc = s5_d.shape[-1]
    assert n % TM == 0 and ctx_len % TM == 0 and bsz % 8 == 0

    xall = _embed(x, ctx, _pos_embed_2d(n, d).astype(x.dtype))

    rows = -(-(bsz + 1) // 8) * 8
    cc = jnp.zeros((rows, d), F32).at[0].set(c_ctx).at[1:bsz + 1].set(c)
    mod_all = _modulation(cc, w_mod, b_mod)

    gamma = jax.nn.softmax(hgrn_lb_logits.astype(F32), axis=0)
    lower_bounds = jnp.cumsum(gamma, axis=0) - gamma[0:1]

    for l in range(depth):
        last = l == depth - 1
        off = nctx if last else 0
        mod_c = jnp.broadcast_to(mod_all[l, 0][None], (bsz, N_MOD * d))
        mod = jnp.stack([mod_c, mod_all[l, 1:bsz + 1]], axis=1).reshape(bsz, 2, N_MOD, d)

        cu0 = 2 * da + 5 * db
        w_pad = jnp.concatenate([w_in[l][:, :cu0 + dc], jnp.zeros((d, da - dc), F32), w_in[l][:, cu0 + dc:]],
                                axis=1).astype(BF16)
        p = _in_proj(xall, mod, w_pad, ctx_len)

        bias = jnp.repeat(gmlp_b_s[l].T, da // G_A, axis=1)
        ya = _gmlp(p, gmlp_ln_g[l].reshape(1, da), gmlp_ln_b[l].reshape(1, da), gmlp_w_s[l].astype(BF16), bias)

        o_f = _hgrn(p, lower_bounds[l, 0].reshape(1, db), False, nctx)
        o_b = _hgrn(p, lower_bounds[l, 1].reshape(1, db), True, nctx)

        ys = []
        for r in range(2):
            prm = _s5_params(s5_lam_re[l, r], s5_lam_im[l, r], s5_log_step[l, r], s5_b_re[l, r], s5_b_im[l, r],
                             s5_c_re[l, r], s5_c_im[l, r])
            ys.append(_s5(p, (cu0 // da, da), *prm, rev=bool(r), ctx_len=ctx_len))

        merged = _merge_a(ya, o_f, o_b, p, ys[0], ys[1], hgrn_norm_g[l].reshape(1, db), s5_d[l].reshape(1, dc),
                          s5_w_glu[l].astype(BF16), w_branch_a[l].astype(BF16), w_branch_b[l].astype(BF16),
                          w_branch_c[l].astype(BF16), off)
        x1, h2, q = _merge_b(merged, xall, mod, w_out[l].astype(BF16), peer_w_q[l].astype(BF16), off, nctx)

        lo = x1.shape[1]
        idx, g = _route(q.reshape(bsz * lo, -1), peer_sub_keys[l].astype(BF16))
        tab = _pack_expert_table(peer_u, peer_v, l)
        xall = _experts(idx, g, h2.reshape(bsz * lo, d), x1.reshape(bsz * lo, d), mod,
                        final_norm_g.astype(F32).reshape(1, d), tab, off, nctx, final_norm=last)
        xall = xall.reshape(bsz, lo, d)

    return xall
```

```python
import functools
import math

import jax
import jax.numpy as jnp
from jax import lax
from jax.experimental import pallas as pl
from jax.experimental.pallas import tpu as pltpu

F32 = jnp.float32
BF16 = jnp.bfloat16

EPS = 1e-6
GRID_W = 64
POS_THETA = 10000.0
N_MOD = 6
G_A = 8
GMLP_CHUNK = 128
H_B = 8
HGRN_CHUNK = 32
HGRN_HEADS_PER_BODY = 8
S5_IN = 16
S5_STATE = 64
S5_SLAB_G = 8
PEER_HEADS = 8
PEER_NKEYS = 128
PEER_TOPK = 16

TM = 256
LANE = 128
SUB = 8
VMEM_LIMIT = 56 * 1024 * 1024


def _cp(sem, vmem=VMEM_LIMIT):
    return pltpu.CompilerParams(dimension_semantics=sem, vmem_limit_bytes=vmem)


def _const_spec(shape):
    nd = len(shape)
    return pl.BlockSpec(shape, lambda *_: (0,) * nd, pipeline_mode=pl.Buffered(1))


def _gelu(x):
    return x * (lax.erf(x * (1.0 / math.sqrt(2.0))) + 1.0) * 0.5


def _sigmoid(x):
    return jax.nn.sigmoid(x)


def _dot(a, b):
    return jnp.dot(a, b, preferred_element_type=F32)


def _dot_nt(a, b):
    return lax.dot_general(a, b, (((1,), (1,)), ((), ())), preferred_element_type=F32)


def _dot_tn(a, b):
    return lax.dot_general(a, b, (((0,), (0,)), ((), ())), preferred_element_type=F32)


def _embed_kernel(ctx_ref, x_ref, pos_ref, o_ref, *, nctx):
    j = pl.program_id(1)

    @pl.when(j < nctx)
    def _():
        o_ref[...] = ctx_ref[...]

    @pl.when(j >= nctx)
    def _():
        o_ref[...] = x_ref[...] + pos_ref[...]


def _embed(x, ctx, pos):
    bsz, n, d = x.shape
    nctx = ctx.shape[1] // TM
    nblk = nctx + n // TM
    return pl.pallas_call(
        functools.partial(_embed_kernel, nctx=nctx),
        out_shape=jax.ShapeDtypeStruct((bsz, nblk * TM, d), x.dtype),
        grid=(bsz, nblk),
        in_specs=[
            pl.BlockSpec((None, TM, d), lambda b, j: (b, jnp.minimum(j, nctx - 1), 0)),
            pl.BlockSpec((None, TM, d), lambda b, j: (b, jnp.maximum(j - nctx, 0), 0)),
            pl.BlockSpec((TM, d), lambda b, j: (jnp.maximum(j - nctx, 0), 0)),
        ],
        out_specs=pl.BlockSpec((None, TM, d), lambda b, j: (b, j, 0)),
        compiler_params=_cp(("parallel", "arbitrary")),
        name="embed",
    )(ctx, x, pos)


def _mod_kernel(c_ref, w_ref, b_ref, o_ref):
    c = c_ref[...]
    s = c * _sigmoid(c)
    o_ref[...] = _dot(s.astype(BF16), w_ref[...].astype(BF16)) + b_ref[...]


def _modulation(cc, w_mod, b_mod):
    depth, d, nd = w_mod.shape
    rows = cc.shape[0]
    tn = 1024
    return pl.pallas_call(
        _mod_kernel,
        out_shape=jax.ShapeDtypeStruct((depth, rows, nd), F32),
        grid=(depth, nd // tn),
        in_specs=[
            pl.BlockSpec((rows, d), lambda l, n: (0, 0)),
            pl.BlockSpec((None, d, tn), lambda l, n: (l, 0, n)),
            pl.BlockSpec((None, 1, tn), lambda l, n: (l, 0, n)),
        ],
        out_specs=pl.BlockSpec((None, rows, tn), lambda l, n: (l, 0, n)),
        compiler_params=_cp(("parallel", "arbitrary")),
        name="modulation",
    )(cc, w_mod, b_mod.reshape(depth, 1, nd))


def _inproj_kernel(x_ref, mod_ref, w_ref, o_ref, h_scr, *, tm, ctx_rows):
    j = pl.program_id(1)

    @pl.when(pl.program_id(2) == 0)
    def _():
        x = x_ref[...]
        r = lax.rsqrt(jnp.mean(x * x, axis=-1, keepdims=True) + EPS)
        row = j * tm + lax.broadcasted_iota(jnp.int32, (tm, 1), 0)
        is_ctx = row < ctx_rows
        sh = jnp.where(is_ctx, mod_ref[0, 0:1, :], mod_ref[1, 0:1, :])
        sc = jnp.where(is_ctx, mod_ref[0, 1:2, :], mod_ref[1, 1:2, :])
        h_scr[...] = (x * r * (1.0 + sc) + sh).astype(BF16)

    o_ref[...] = _dot(h_scr[...], w_ref[...])


def _in_proj(xall, mod, w_pad, ctx_rows):
    bsz, l, d = xall.shape
    ncol = w_pad.shape[1]
    tm = 768 if l % 768 == 0 else TM
    tn = 1024
    return pl.pallas_call(
        functools.partial(_inproj_kernel, tm=tm, ctx_rows=ctx_rows),
        out_shape=jax.ShapeDtypeStruct((bsz, l, ncol), F32),
        grid=(bsz, l // tm, ncol // tn),
        in_specs=[
            pl.BlockSpec((None, tm, d), lambda b, j, n: (b, j, 0)),
            pl.BlockSpec((None, 2, N_MOD, d), lambda b, j, n: (b, 0, 0, 0)),
            pl.BlockSpec((d, tn), lambda b, j, n: (0, n)),
        ],
        out_specs=pl.BlockSpec((None, tm, tn), lambda b, j, n: (b, j, n)),
        scratch_shapes=[pltpu.VMEM((tm, d), BF16)],
        compiler_params=_cp(("parallel", "parallel", "arbitrary")),
        name="in_proj",
    )(xall, mod, w_pad)


def _gmlp_kernel(u_ref, v_ref, lng_ref, lnb_ref, ws_ref, bias_ref, o_ref):
    v = _gelu(v_ref[...])
    mu = jnp.mean(v, axis=-1, keepdims=True)
    vc = v - mu
    var = jnp.mean(vc * vc, axis=-1, keepdims=True)
    v = (vc * lax.rsqrt(var + EPS) * lng_ref[...] + lnb_ref[...]).astype(BF16)
    for c in range(TM // GMLP_CHUNK):
        rows = slice(c * GMLP_CHUNK, (c + 1) * GMLP_CHUNK)
        for g in range(G_A):
            cols = slice(g * LANE, (g + 1) * LANE)
            mixed = _dot(ws_ref[g], v[rows, cols]) + bias_ref[:, cols]
            o_ref[rows, cols] = _gelu(u_ref[rows, cols]) * mixed


def _gmlp(p, ln_g, ln_b, w_s, bias):
    bsz, l, _ = p.shape
    da = ln_g.shape[-1]
    return pl.pallas_call(
        _gmlp_kernel,
        out_shape=jax.ShapeDtypeStruct((bsz, l, da), F32),
        grid=(bsz, l // TM),
        in_specs=[
            pl.BlockSpec((None, TM, da), lambda b, j: (b, j, 0)),
            pl.BlockSpec((None, TM, da), lambda b, j: (b, j, 1)),
            _const_spec((1, da)),
            _const_spec((1, da)),
            _const_spec((G_A, GMLP_CHUNK, GMLP_CHUNK)),
            _const_spec((GMLP_CHUNK, da)),
        ],
        out_specs=pl.BlockSpec((None, TM, da), lambda b, j: (b, j, 0)),
        compiler_params=_cp(("parallel", "parallel")),
        name="gmlp",
    )(p, p, ln_g, ln_b, w_s, bias)


def _scan_block(p, rev, nctx, nblk):
    if not rev:
        return p
    return jnp.where(p < nctx, nctx - 1 - p, nblk - 1 - (p - nctx))


def _hgrn_kernel(q_ref, f_ref, i_ref, lb_ref, o_ref, st_ref, *, rev):
    c_sz = HGRN_CHUNK
    nchunk = TM // c_sz

    @pl.when(pl.program_id(1) == 0)
    def _():
        st_ref[...] = jnp.zeros_like(st_ref)

    nsb = c_sz // SUB
    rowi = lax.broadcasted_iota(jnp.int32, (c_sz, LANE), 0)
    row8 = lax.broadcasted_iota(jnp.int32, (SUB, LANE), 0)
    rowa = lax.broadcasted_iota(jnp.int32, (c_sz, c_sz), 0)

    hpb = HGRN_HEADS_PER_BODY
    ngrp = H_B // hpb

    def chunk_body(it, carry):
        ci, hg = it // ngrp, it % ngrp
        c = (nchunk - 1 - ci) if rev else ci
        rows = pl.ds(pl.multiple_of(c * c_sz, c_sz), c_sz)
        for hl in range(hpb):
            if ngrp == 1:
                hh, cols = hl, slice(hl * LANE, (hl + 1) * LANE)
            else:
                hh = hg * hpb + hl
                cols = pl.ds(pl.multiple_of(hh * LANE, LANE), LANE)
            lbh = lb_ref[:, cols]
            st = st_ref[hh]
            qr = q_ref[rows, cols]
            q = qr * _sigmoid(qr)
            fg = lbh + (1.0 - lbh) * _sigmoid(f_ref[rows, cols])
            k = 1.0 - fg
            v = i_ref[rows, cols]
            b = jnp.log(fg)
            for s in (1, 2, 4, 8, 16):
                if rev:
                    b = b + jnp.where(rowi < c_sz - s, pltpu.roll(b, c_sz - s, 0), 0.0)
                else:
                    b = b + jnp.where(rowi >= s, pltpu.roll(b, s, 0), 0.0)
            b_last = b[0:1, :] if rev else b[c_sz - 1:c_sz, :]
            o = _dot_nt((q * jnp.exp(b)).astype(BF16), st.astype(BF16))
            diag = []
            for jb in range(nsb):
                r8 = slice(jb * SUB, (jb + 1) * SUB)
                bj, qj, kj, vj = b[r8], q[r8], k[r8], v[r8]
                oj = jnp.zeros((SUB, LANE), F32)
                for s in range(SUB):
                    valid = (row8 <= s) if rev else (row8 >= s)
                    dec = jnp.exp(bj - bj[s:s + 1, :])
                    pr = jnp.where(valid, qj * (kj[s:s + 1, :] * dec), 0.0)
                    oj = oj + jnp.sum(pr, axis=-1, keepdims=True) * vj[s:s + 1, :]
                diag.append(oj)
            o = o + jnp.concatenate(diag, axis=0)
            att = jnp.zeros((c_sz, c_sz), F32)
            for jb in (range(1, nsb) if rev else range(nsb - 1)):
                edge = jb * SUB if rev else jb * SUB + SUB - 1
                b_ref_row = b[edge:edge + 1, :]
                in_blk = (rowi >= jb * SUB) & (rowi < (jb + 1) * SUB)
                kt = jnp.where(in_blk, k * jnp.exp(jnp.minimum(b_ref_row - b, 0.0)), 0.0)
                qq = q * jnp.exp(jnp.minimum(b - b_ref_row, 0.0))
                a = _dot_nt(qq.astype(BF16), kt.astype(BF16))
                t_ok = (rowa < jb * SUB) if rev else (rowa >= (jb + 1) * SUB)
                att = att + jnp.where(t_ok, a, 0.0)
            o = o + _dot(att.astype(BF16), v.astype(BF16))
            o_ref[rows, cols] = o
            kk = k * jnp.exp(b_last - b)
            st_ref[hh] = st * jnp.exp(b_last) + _dot_tn(v.astype(BF16), kk.astype(BF16))
        return carry

    lax.fori_loop(0, nchunk * ngrp, chunk_body, 0)


def _hgrn(p, lb, rev, nctx):
    bsz, l, _ = p.shape
    db = lb.shape[-1]
    nblk = l // TM
    blk = functools.partial(_scan_block, rev=rev, nctx=nctx, nblk=nblk)
    fcol = 4 if rev else 3
    return pl.pallas_call(
        functools.partial(_hgrn_kernel, rev=rev),
        out_shape=jax.ShapeDtypeStruct((bsz, l, db), F32),
        grid=(bsz, nblk),
        in_specs=[
            pl.BlockSpec((None, TM, db), lambda b, p_: (b, blk(p_), 2)),
            pl.BlockSpec((None, TM, db), lambda b, p_: (b, blk(p_), fcol)),
            pl.BlockSpec((None, TM, db), lambda b, p_: (b, blk(p_), 5)),
            _const_spec((1, db)),
        ],
        out_specs=pl.BlockSpec((None, TM, db), lambda b, p_: (b, blk(p_), 0)),
        scratch_shapes=[pltpu.VMEM((H_B, LANE, LANE), F32)],
        compiler_params=_cp(("parallel", "arbitrary")),
        name="hgrn_bwd" if rev else "hgrn_fwd",
    )(p, p, p, lb)


def _s5_kernel(u_ref, bb_ref, lam_ref, cc_ref, o_ref, bu_scr, st_ref, *, rev, tb, bsz, nslab):
    half = S5_SLAB_G * S5_STATE

    @pl.when(pl.program_id(0) == 0)
    def _():
        st_ref[...] = jnp.zeros_like(st_ref)

    for j in range(nslab):
        uj = jnp.swapaxes(u_ref[:, :, j * LANE:(j + 1) * LANE], 0, 1).reshape(tb * bsz, LANE).astype(BF16)
        bu_scr[...] = _dot(uj, bb_ref[j])
        lre = jnp.broadcast_to(lam_ref[j, 0:1, :], (bsz, half))
        lim = jnp.broadcast_to(lam_ref[j, 1:2, :], (bsz, half))

        def step(i, carry):
            xre, xim = carry
            t = (tb - 1 - i) if rev else i
            rows = pl.ds(pl.multiple_of(t * bsz, bsz), bsz)
            nre = lre * xre - lim * xim + bu_scr[rows, 0:half]
            nim = lre * xim + lim * xre + bu_scr[rows, half:2 * half]
            bu_scr[rows, 0:half] = nre
            bu_scr[rows, half:2 * half] = nim
            return nre, nim

        xre, xim = lax.fori_loop(0, tb, step, (st_ref[j, :, 0:half], st_ref[j, :, half:2 * half]), unroll=4)
        st_ref[j, :, 0:half] = xre
        st_ref[j, :, half:2 * half] = xim
        y = _dot(bu_scr[...].astype(BF16), cc_ref[j])
        o_ref[:, :, j * LANE:(j + 1) * LANE] = jnp.swapaxes(y.reshape(tb, bsz, LANE), 0, 1)


def _s5(p, ucol, bbig, lam, cbig, rev, ctx_len):
    bsz, l, _ = p.shape
    nslab = bbig.shape[0]
    dc = nslab * LANE
    tb = 128
    nblk, nctx = l // tb, ctx_len // tb
    blk = functools.partial(_scan_block, rev=rev, nctx=nctx, nblk=nblk)
    half = S5_SLAB_G * S5_STATE
    return pl.pallas_call(
        functools.partial(_s5_kernel, rev=rev, tb=tb, bsz=bsz, nslab=nslab),
        out_shape=jax.ShapeDtypeStruct((bsz, l, dc), F32),
        grid=(nblk,),
        in_specs=[
            pl.BlockSpec((bsz, tb, ucol[1]), lambda p_: (0, blk(p_), ucol[0])),
            _const_spec((nslab, LANE, 2 * half)),
            _const_spec((nslab, 2, half)),
            _const_spec((nslab, 2 * half, LANE)),
        ],
        out_specs=pl.BlockSpec((bsz, tb, dc), lambda p_: (0, blk(p_), 0)),
        scratch_shapes=[pltpu.VMEM((tb * bsz, 2 * half), F32), pltpu.VMEM((nslab, bsz, 2 * half), F32)],
        compiler_params=_cp(("arbitrary",)),
        name="s5_bwd" if rev else "s5_fwd",
    )(p, bbig, lam, cbig)


def _s5_params(lam_re, lam_im, log_step, b_re, b_im, c_re, c_im):
    g_c, p_s = lam_re.shape
    nslab = g_c // S5_SLAB_G
    lam = lax.complex(lam_re.astype(F32), lam_im.astype(F32))
    lam_dt = lam * jnp.exp(log_step.astype(F32))[:, None]
    lam_bar = jnp.exp(lam_dt)
    b_bar = ((lam_bar - 1.0) / lam)[:, :, None] * lax.complex(b_re.astype(F32), b_im.astype(F32))
    eye = jnp.eye(S5_SLAB_G, dtype=F32)

    def in_map(m):
        m = m.reshape(nslab, S5_SLAB_G, p_s, S5_IN)
        return jnp.einsum('jgpd,gh->jgdhp', m, eye).reshape(nslab, S5_SLAB_G * S5_IN, S5_SLAB_G * p_s)

    def out_map(m):
        m = m.reshape(nslab, S5_SLAB_G, S5_IN, p_s)
        return jnp.einsum('jgdp,gh->jgphd', m, eye).reshape(nslab, S5_SLAB_G * p_s, S5_SLAB_G * S5_IN)

    bbig = jnp.concatenate([in_map(jnp.real(b_bar)), in_map(jnp.imag(b_bar))], axis=-1).astype(BF16)
    cbig = jnp.concatenate([out_map(c_re.astype(F32)), out_map(-c_im.astype(F32))], axis=1).astype(BF16)
    lam_s = jnp.stack([jnp.real(lam_bar).reshape(nslab, -1), jnp.imag(lam_bar).reshape(nslab, -1)], axis=1)
    return bbig, lam_s, cbig


def _merge_a_kernel(ya_ref, of_ref, ob_ref, bg_ref, yf_ref, yb_ref, cu_ref, ga_ref, gb_ref, gc_ref,
                    hng_ref, dsk_ref, wglu_ref, wa_ref, wb_ref, wc_ref, o_ref):
    dc = dsk_ref.shape[-1]
    o = of_ref[...] + ob_ref[...]
    parts = []
    for hh in range(H_B):
        oh = o[:, hh * LANE:(hh + 1) * LANE]
        parts.append(oh * lax.rsqrt(jnp.mean(oh * oh, axis=-1, keepdims=True) + EPS))
    bg = bg_ref[...]
    yb = jnp.concatenate(parts, axis=-1) * hng_ref[...] * (bg * _sigmoid(bg))
    y = yf_ref[...] + yb_ref[...] + dsk_ref[...] * cu_ref[:, 0:dc]
    z = _dot(_gelu(y).astype(BF16), wglu_ref[...])
    yc = z[:, 0:dc] * _sigmoid(z[:, dc:2 * dc])
    merged = (_sigmoid(ga_ref[...]) * _dot(ya_ref[...].astype(BF16), wa_ref[...])
              + _sigmoid(gb_ref[...]) * _dot(yb.astype(BF16), wb_ref[...])
              + _sigmoid(gc_ref[...]) * _dot(yc.astype(BF16), wc_ref[...]))
    o_ref[...] = merged.astype(BF16)


def _merge_a(ya, o_f, o_b, p, y_f, y_b, hn_g, d_skip, w_glu, wa, wb, wc, off):
    bsz, l, da = ya.shape
    dc = y_f.shape[-1]
    d = wa.shape[-1]
    tm = 256
    o2 = off * (TM // tm)
    nblk = l // tm - o2
    tok = lambda b, j: (b, j + o2, 0)
    pcol = lambda c: (lambda b, j: (b, j + o2, c))
    return pl.pallas_call(
        _merge_a_kernel,
        out_shape=jax.ShapeDtypeStruct((bsz, nblk * tm, d), BF16),
        grid=(bsz, nblk),
        in_specs=[
            pl.BlockSpec((None, tm, da), tok),
            pl.BlockSpec((None, tm, da), tok),
            pl.BlockSpec((None, tm, da), tok),
            pl.BlockSpec((None, tm, da), pcol(6)),
            pl.BlockSpec((None, tm, dc), tok),
            pl.BlockSpec((None, tm, dc), tok),
            pl.BlockSpec((None, tm, da), pcol(7)),
            pl.BlockSpec((None, tm, d), pcol(4)),
            pl.BlockSpec((None, tm, d), pcol(5)),
            pl.BlockSpec((None, tm, d), pcol(6)),
            _const_spec((1, da)),
            _const_spec((1, dc)),
            _const_spec(w_glu.shape),
            _const_spec(wa.shape),
            _const_spec(wb.shape),
            _const_spec(wc.shape),
        ],
        out_specs=pl.BlockSpec((None, tm, d), lambda b, j: (b, j, 0)),
        compiler_params=_cp(("parallel", "parallel")),
        name="merge_a",
    )(ya, o_f, o_b, p, y_f, y_b, p, p, p, p, hn_g, d_skip, w_glu, wa, wb, wc)


def _merge_b_kernel(m_ref, x_ref, mod_ref, wo_ref, wq_ref, x1_ref, h2_ref, q_ref):
    x1 = x_ref[...] + mod_ref[2:3, :] * _dot(m_ref[...], wo_ref[...])
    x1_ref[...] = x1
    r = lax.rsqrt(jnp.mean(x1 * x1, axis=-1, keepdims=True) + EPS)
    h2 = x1 * r * (1.0 + mod_ref[4:5, :]) + mod_ref[3:4, :]
    h2_ref[...] = h2
    q_ref[...] = _dot(h2.astype(BF16), wq_ref[...])


def _merge_b(merged, xall, mod, wo, wq, off, nctx):
    bsz, lo, d = merged.shape
    nblk = lo // TM
    dq = wq.shape[-1]
    out = lambda b, j: (b, j, 0)
    return pl.pallas_call(
        _merge_b_kernel,
        out_shape=(jax.ShapeDtypeStruct((bsz, lo, d), F32), jax.ShapeDtypeStruct((bsz, lo, d), F32),
                   jax.ShapeDtypeStruct((bsz, lo, dq), F32)),
        grid=(bsz, nblk),
        in_specs=[
            pl.BlockSpec((None, TM, d), out),
            pl.BlockSpec((None, TM, d), lambda b, j: (b, j + off, 0)),
            pl.BlockSpec((None, None, N_MOD, d), lambda b, j: (b, (j + off >= nctx).astype(jnp.int32), 0, 0)),
            _const_spec(wo.shape),
            _const_spec(wq.shape),
        ],
        out_specs=(pl.BlockSpec((None, TM, d), out), pl.BlockSpec((None, TM, d), out),
                   pl.BlockSpec((None, TM, dq), out)),
        compiler_params=_cp(("parallel", "parallel")),
        name="merge_b",
    )(merged, xall, mod, wo, wq)


def _topk_rows(s, row_id, k):
    slot = lax.broadcasted_iota(jnp.int32, (k, s.shape[1]), 0)
    vals = jnp.zeros((k, s.shape[1]), F32)
    ids = jnp.zeros((k, s.shape[1]), F32)
    for r in range(k):
        m = jnp.max(s, axis=0, keepdims=True)
        am = jnp.min(jnp.where(s == m, row_id, 1e9), axis=0, keepdims=True)
        vals = jnp.where(slot == r, m, vals)
        ids = jnp.where(slot == r, am, ids)
        s = jnp.where(row_id == am, -jnp.inf, s)
    return vals, ids


def _pair_candidates():
    k = PEER_TOPK
    pieces = [("a", 0, 0, 16), ("a", 1, 0, 8), ("a", 2, 0, 8), ("a", 3, 0, 8), ("a", 4, 0, 8),
              ("b", 0, 0, 8), ("b", 1, 0, 8), ("b", 0, 8, 16)]
    ids, bias, seen = [], [], set()
    for kind, fixed, lo, hi in pieces:
        for r in range(lo, hi):
            a, b = (fixed, r) if kind == "a" else (r, fixed)
            ok = (a + 1) * (b + 1) <= k and (a, b) not in seen
            if ok:
                seen.add((a, b))
            ids.append(float(a * k + b))
            bias.append(0.0 if ok else -math.inf)
    assert len(seen) == sum(k // (a + 1) for a in range(k))
    return pieces, ids, bias


def _route_kernel(q_ref, keys_ref, cid_ref, cbias_ref, idx_ref, g_ref, it_scr, gt_scr):
    k = PEER_TOPK
    pieces, _, _ = _pair_candidates()
    key_id = lax.broadcasted_iota(jnp.int32, (PEER_NKEYS, q_ref.shape[0]), 0).astype(F32)
    for h in range(PEER_HEADS):
        sv, si = [], []
        for half in range(2):
            c0 = (h * 2 + half) * PEER_NKEYS
            qh = q_ref[:, c0:c0 + PEER_NKEYS].astype(BF16)
            s = _dot_nt(keys_ref[h, half], qh)
            v_, i_ = _topk_rows(s, key_id, k)
            sv.append(v_)
            si.append(i_)
        cand = jnp.concatenate(
            [sv[0][f:f + 1, :] + sv[1][lo:hi, :] if kind == "a" else sv[0][lo:hi, :] + sv[1][f:f + 1, :]
             for kind, f, lo, hi in pieces], axis=0) + cbias_ref[...]
        fv, fi = _topk_rows(cand, cid_ref[...], k)
        fa = jnp.floor(fi * (1.0 / k))
        fb = fi - fa * k
        i1 = jnp.zeros_like(fi)
        i2 = jnp.zeros_like(fi)
        for a in range(k):
            i1 = jnp.where(fa == a, si[0][a:a + 1, :], i1)
            i2 = jnp.where(fb == a, si[1][a:a + 1, :], i2)
        e = jnp.exp(fv - fv[0:1, :])
        gt_scr[h * k:(h + 1) * k, :] = e / jnp.sum(e, axis=0, keepdims=True)
        it_scr[h * k:(h + 1) * k, :] = i1 * PEER_NKEYS + i2
    g_ref[...] = gt_scr[...].T
    idx_ref[...] = it_scr[...].T.astype(jnp.int32)


def _route(q, keys):
    t, dq = q.shape
    tm = LANE
    nsel = PEER_HEADS * PEER_TOPK
    _, ids, bias = _pair_candidates()
    cid = jnp.broadcast_to(jnp.asarray(ids, F32)[:, None], (len(ids), tm))
    cbias = jnp.broadcast_to(jnp.asarray(bias, F32)[:, None], (len(bias), tm))
    return pl.pallas_call(
        _route_kernel,
        out_shape=(jax.ShapeDtypeStruct((t, nsel), jnp.int32), jax.ShapeDtypeStruct((t, nsel), F32)),
        grid=(t // tm,),
        in_specs=[pl.BlockSpec((tm, dq), lambda i: (i, 0)), _const_spec(keys.shape),
                  _const_spec(cid.shape), _const_spec(cbias.shape)],
        out_specs=(pl.BlockSpec((tm, nsel), lambda i: (i, 0)), pl.BlockSpec((tm, nsel), lambda i: (i, 0))),
        scratch_shapes=[pltpu.VMEM((nsel, tm), F32), pltpu.VMEM((nsel, tm), F32)],
        compiler_params=_cp(("parallel",)),
        name="peer_route",
    )(q, keys, cid, cbias)


PEER_TB = 64
PEER_SLOTS = 4
PEER_AHEAD = PEER_SLOTS - 1
HALF_SUB = 8
HI_MASK = -65536


def _pack_expert_table(u_all, v_all, layer):
    _, n_exp, d = u_all.shape
    tb = 256
    assert d == 2 * HALF_SUB * LANE and n_exp % tb == 0

    def pack(x):
        bits = lax.bitcast_convert_type(x.astype(BF16).astype(F32), jnp.int32)
        word = (bits[:, :d // 2] & HI_MASK) | lax.shift_right_logical(bits[:, d // 2:], 16)
        return word.reshape(x.shape[0], HALF_SUB, LANE)

    def body(u_ref, v_ref, o_ref):
        o_ref[:, 0:HALF_SUB, :] = pack(u_ref[...])
        o_ref[:, HALF_SUB:2 * HALF_SUB, :] = pack(v_ref[...])

    return pl.pallas_call(
        body,
        out_shape=jax.ShapeDtypeStruct((n_exp, 2 * HALF_SUB, LANE), jnp.int32),
        grid=(n_exp // tb,),
        in_specs=[pl.BlockSpec((None, tb, d), lambda i: (layer, i, 0)),
                  pl.BlockSpec((None, tb, d), lambda i: (layer, i, 0))],
        out_specs=pl.BlockSpec((tb, 2 * HALF_SUB, LANE), lambda i: (i, 0, 0)),
        compiler_params=_cp(("parallel",)),
        name="pack_experts",
    )(u_all, v_all)


def _unpack_pair(word):
    return lax.bitcast_convert_type(word & HI_MASK, F32), lax.bitcast_convert_type(word << 16, F32)


def _experts_kernel(idx_ref, g_ref, h_ref, x_ref, mod_ref, fin_ref, tab_ref, o_ref, *scratch, final_norm):
    bufs, sem = scratch[:PEER_SLOTS], scratch[PEER_SLOTS]
    nsel = PEER_HEADS * PEER_TOPK
    per = nsel // (2 * HALF_SUB)
    step = pl.program_id(0)

    def issue(t, slot, j0=0, j1=nsel):
        for j in range(j0, j1):
            e = idx_ref[0, t * nsel + j]
            pltpu.make_async_copy(tab_ref.at[e], bufs[slot].at[:, j, :], sem.at[slot]).start(priority=j % 2)

    def wait(slot):
        pltpu.make_async_copy(bufs[slot], bufs[slot], sem.at[slot]).wait()

    @pl.when(step == 0)
    def _():
        for s in range(PEER_AHEAD):
            issue(s, s)

    eye = lax.broadcasted_iota(jnp.int32, (nsel, nsel), 0) == lax.broadcasted_iota(jnp.int32, (nsel, nsel), 1)

    def group(t4, c):
        for s in range(PEER_SLOTS):
            t = t4 * PEER_SLOTS + s
            wait(s)
            nxt = (s + PEER_AHEAD) % PEER_SLOTS
            row = pl.ds(t, 1)
            hrow = h_ref[row, :]
            buf = bufs[s]
            acc = None
            for cc in range(HALF_SUB):
                u_hi, u_lo = _unpack_pair(buf[cc])
                term = (u_hi * hrow[:, cc * LANE:(cc + 1) * LANE]
                        + u_lo * hrow[:, (HALF_SUB + cc) * LANE:(HALF_SUB + cc + 1) * LANE])
                acc = term if acc is None else acc + term
                issue(t + PEER_AHEAD, nxt, cc * per, (cc + 1) * per)
            act = jnp.sum(acc, axis=-1, keepdims=True)
            gcol = jnp.sum(jnp.where(eye, g_ref[row, :], 0.0), axis=-1, keepdims=True)
            w = _gelu(act) * gcol
            outs = [None] * (2 * HALF_SUB)
            for cc in range(HALF_SUB):
                v_hi, v_lo = _unpack_pair(buf[HALF_SUB + cc])
                outs[cc] = jnp.sum(w * v_hi, axis=0, keepdims=True)
                outs[HALF_SUB + cc] = jnp.sum(w * v_lo, axis=0, keepdims=True)
                issue(t + PEER_AHEAD, nxt, (HALF_SUB + cc) * per, (HALF_SUB + cc + 1) * per)
            o_ref[row, :] = x_ref[row, :] + mod_ref[5:6, :] * jnp.concatenate(outs, axis=-1)
        return c

    lax.fori_loop(0, PEER_TB // PEER_SLOTS, group, 0)

    if final_norm:
        y = o_ref[...]
        o_ref[...] = y * lax.rsqrt(jnp.mean(y * y, axis=-1, keepdims=True) + EPS) * fin_ref[...]

    @pl.when(step == pl.num_programs(0) - 1)
    def _():
        for s in range(PEER_AHEAD):
            wait(s)


def _experts(idx, g, h2, x1, mod, fin_g, tab, off, nctx, final_norm):
    t_all, d = h2.shape
    bsz = mod.shape[0]
    nsel = idx.shape[-1]
    tb = PEER_TB
    nb = t_all // tb
    per_b = nb // bsz
    per = TM // tb
    assert d == 2 * HALF_SUB * LANE and tb % PEER_SLOTS == 0
    main = idx.reshape(nb, tb * nsel)
    idx_ext = jnp.concatenate([main, jnp.roll(main, -1, axis=0)[:, :PEER_AHEAD * nsel]], axis=1)
    idx_ext = idx_ext.reshape(nb, 1, (tb + PEER_AHEAD) * nsel)
    tokb = lambda i: (i, 0)

    def mod_map(i):
        return (i // per_b, ((i % per_b) // per + off >= nctx).astype(jnp.int32), 0, 0)

    return pl.pallas_call(
        functools.partial(_experts_kernel, final_norm=final_norm),
        out_shape=jax.ShapeDtypeStruct((t_all, d), F32),
        grid=(nb,),
        in_specs=[
            pl.BlockSpec((None, 1, (tb + PEER_AHEAD) * nsel), lambda i: (i, 0, 0), memory_space=pltpu.SMEM),
            pl.BlockSpec((tb, nsel), tokb),
            pl.BlockSpec((tb, d), tokb),
            pl.BlockSpec((tb, d), tokb),
            pl.BlockSpec((None, None, N_MOD, d), mod_map),
            _const_spec((1, d)),
            pl.BlockSpec(memory_space=pl.ANY),
        ],
        out_specs=pl.BlockSpec((tb, d), tokb),
        scratch_shapes=[pltpu.VMEM((2 * HALF_SUB, nsel, LANE), jnp.int32) for _ in range(PEER_SLOTS)]
        + [pltpu.SemaphoreType.DMA((PEER_SLOTS,))],
        compiler_params=_cp(("arbitrary",)),
        name="peer_experts",
    )(idx_ext, g, h2, x1, mod, fin_g, tab)


def _pos_embed_2d(n, d):
    n_rows = n // GRID_W
    quarter = d // 4
    omega = 1.0 / (POS_THETA ** (jnp.arange(quarter, dtype=F32) / quarter))

    def axis_code(length):
        ang = jnp.arange(length, dtype=F32)[:, None] * omega[None]
        return jnp.concatenate([jnp.sin(ang), jnp.cos(ang)], axis=-1)

    er = jnp.broadcast_to(axis_code(n_rows)[:, None], (n_rows, GRID_W, d // 2))
    ec = jnp.broadcast_to(axis_code(GRID_W)[None], (n_rows, GRID_W, d // 2))
    return jnp.concatenate([er, ec], axis=-1).reshape(n, d)


def kernel(x, c, ctx, c_ctx, w_mod, b_mod, w_in, gmlp_ln_g, gmlp_ln_b, gmlp_w_s, gmlp_b_s, hgrn_lb_logits, hgrn_norm_g, s5_lam_re, s5_lam_im, s5_log_step, s5_b_re, s5_b_im, s5_c_re, s5_c_im, s5_d, s5_w_glu, w_branch_a, w_branch_b, w_branch_c, w_out, peer_w_q, peer_sub_keys, peer_u, peer_v, final_norm_g):
    bsz, n, d = x.shape
    ctx_len = ctx.shape[1]
    depth = w_mod.shape[0]
    nctx = ctx_len // TM
    da = gmlp_ln_g.shape[-1]
    db = hgrn_norm_g.shape[-1]
    dc = s5_d.shape[-1]
    assert n % TM == 0 and ctx_len % TM == 0 and bsz % SUB == 0
    assert da == db and d == 2 * da and dc % LANE == 0 and dc <= da
    assert w_in.shape[-1] == 2 * da + 5 * db + dc + 3 * d

    xall = _embed(x, ctx, _pos_embed_2d(n, d).astype(x.dtype))

    rows = -(-(bsz + 1) // 8) * 8
    cc = jnp.zeros((rows, d), F32).at[0].set(c_ctx).at[1:bsz + 1].set(c)
    mod_all = _modulation(cc, w_mod, b_mod)

    gamma = jax.nn.softmax(hgrn_lb_logits.astype(F32), axis=0)
    lower_bounds = jnp.cumsum(gamma, axis=0) - gamma[0:1]

    for l in range(depth):
        last = l == depth - 1
        off = nctx if last else 0
        mod_c = jnp.broadcast_to(mod_all[l, 0][None], (bsz, N_MOD * d))
        mod = jnp.stack([mod_c, mod_all[l, 1:bsz + 1]], axis=1).reshape(bsz, 2, N_MOD, d)

        cu0 = 2 * da + 5 * db
        w_pad = jnp.concatenate([w_in[l][:, :cu0 + dc], jnp.zeros((d, da - dc), F32), w_in[l][:, cu0 + dc:]],
                                axis=1).astype(BF16)
        p = _in_proj(xall, mod, w_pad, ctx_len)

        bias = jnp.repeat(gmlp_b_s[l].T, da // G_A, axis=1)
        ya = _gmlp(p, gmlp_ln_g[l].reshape(1, da), gmlp_ln_b[l].reshape(1, da), gmlp_w_s[l].astype(BF16), bias)

        o_f = _hgrn(p, lower_bounds[l, 0].reshape(1, db), False, nctx)
        o_b = _hgrn(p, lower_bounds[l, 1].reshape(1, db), True, nctx)

        ys = []
        for r in range(2):
            prm = _s5_params(s5_lam_re[l, r], s5_lam_im[l, r], s5_log_step[l, r], s5_b_re[l, r], s5_b_im[l, r],
                             s5_c_re[l, r], s5_c_im[l, r])
            ys.append(_s5(p, (cu0 // da, da), *prm, rev=bool(r), ctx_len=ctx_len))

        merged = _merge_a(ya, o_f, o_b, p, ys[0], ys[1], hgrn_norm_g[l].reshape(1, db), s5_d[l].reshape(1, dc),
                          s5_w_glu[l].astype(BF16), w_branch_a[l].astype(BF16), w_branch_b[l].astype(BF16),
                          w_branch_c[l].astype(BF16), off)
        x1, h2, q = _merge_b(merged, xall, mod, w_out[l].astype(BF16), peer_w_q[l].astype(BF16), off, nctx)

        lo = x1.shape[1]
        idx, g = _route(q.reshape(bsz * lo, -1), peer_sub_keys[l].astype(BF16))
        tab = _pack_expert_table(peer_u, peer_v, l)
        xall = _experts(idx, g, h2.reshape(bsz * lo, d), x1.reshape(bsz * lo, d), mod,
                        final_norm_g.astype(F32).reshape(1, d), tab, off, nctx, final_norm=last)
        xall = xall.reshape(bsz, lo, d)

    return xall
```

```python
import functools
import math

import jax
import jax.numpy as jnp
from jax import lax
from jax.experimental import pallas as pl
from jax.experimental.pallas import tpu as pltpu
from jax.experimental.pallas import tpu_sc as plsc

F32 = jnp.float32
BF16 = jnp.bfloat16

EPS = 1e-6
GRID_W = 64
POS_THETA = 10000.0
N_MOD = 6
G_A = 8
GMLP_CHUNK = 128
H_B = 8
HGRN_CHUNK = 32
HGRN_HEADS_PER_BODY = 8
S5_IN = 16
S5_STATE = 64
S5_SLAB_G = 8
PEER_HEADS = 8
PEER_NKEYS = 128
PEER_TOPK = 16

TM = 256
LANE = 128
SUB = 8
VMEM_LIMIT = 56 * 1024 * 1024


def _cp(sem, vmem=VMEM_LIMIT):
    return pltpu.CompilerParams(dimension_semantics=sem, vmem_limit_bytes=vmem)


def _const_spec(shape):
    nd = len(shape)
    return pl.BlockSpec(shape, lambda *_: (0,) * nd, pipeline_mode=pl.Buffered(1))


def _gelu(x):
    return x * (lax.erf(x * (1.0 / math.sqrt(2.0))) + 1.0) * 0.5


def _sigmoid(x):
    return jax.nn.sigmoid(x)


def _dot(a, b):
    return jnp.dot(a, b, preferred_element_type=F32)


def _dot_nt(a, b):
    return lax.dot_general(a, b, (((1,), (1,)), ((), ())), preferred_element_type=F32)


def _dot_tn(a, b):
    return lax.dot_general(a, b, (((0,), (0,)), ((), ())), preferred_element_type=F32)


def _embed_kernel(ctx_ref, x_ref, pos_ref, o_ref, *, nctx):
    j = pl.program_id(1)

    @pl.when(j < nctx)
    def _():
        o_ref[...] = ctx_ref[...]

    @pl.when(j >= nctx)
    def _():
        o_ref[...] = x_ref[...] + pos_ref[...]


def _embed(x, ctx, pos):
    bsz, n, d = x.shape
    nctx = ctx.shape[1] // TM
    nblk = nctx + n // TM
    return pl.pallas_call(
        functools.partial(_embed_kernel, nctx=nctx),
        out_shape=jax.ShapeDtypeStruct((bsz, nblk * TM, d), x.dtype),
        grid=(bsz, nblk),
        in_specs=[
            pl.BlockSpec((None, TM, d), lambda b, j: (b, jnp.minimum(j, nctx - 1), 0)),
            pl.BlockSpec((None, TM, d), lambda b, j: (b, jnp.maximum(j - nctx, 0), 0)),
            pl.BlockSpec((TM, d), lambda b, j: (jnp.maximum(j - nctx, 0), 0)),
        ],
        out_specs=pl.BlockSpec((None, TM, d), lambda b, j: (b, j, 0)),
        compiler_params=_cp(("parallel", "arbitrary")),
        name="embed",
    )(ctx, x, pos)


def _mod_kernel(c_ref, w_ref, b_ref, o_ref):
    c = c_ref[...]
    s = c * _sigmoid(c)
    o_ref[...] = _dot(s.astype(BF16), w_ref[...].astype(BF16)) + b_ref[...]


def _modulation(cc, w_mod, b_mod):
    depth, d, nd = w_mod.shape
    rows = cc.shape[0]
    tn = 1024
    return pl.pallas_call(
        _mod_kernel,
        out_shape=jax.ShapeDtypeStruct((depth, rows, nd), F32),
        grid=(depth, nd // tn),
        in_specs=[
            pl.BlockSpec((rows, d), lambda l, n: (0, 0)),
            pl.BlockSpec((None, d, tn), lambda l, n: (l, 0, n)),
            pl.BlockSpec((None, 1, tn), lambda l, n: (l, 0, n)),
        ],
        out_specs=pl.BlockSpec((None, rows, tn), lambda l, n: (l, 0, n)),
        compiler_params=_cp(("parallel", "arbitrary")),
        name="modulation",
    )(cc, w_mod, b_mod.reshape(depth, 1, nd))


def _inproj_kernel(x_ref, mod_ref, w_ref, o_ref, h_scr, *, tm, ctx_rows):
    j = pl.program_id(1)

    @pl.when(pl.program_id(2) == 0)
    def _():
        x = x_ref[...]
        r = lax.rsqrt(jnp.mean(x * x, axis=-1, keepdims=True) + EPS)
        row = j * tm + lax.broadcasted_iota(jnp.int32, (tm, 1), 0)
        is_ctx = row < ctx_rows
        sh = jnp.where(is_ctx, mod_ref[0, 0:1, :], mod_ref[1, 0:1, :])
        sc = jnp.where(is_ctx, mod_ref[0, 1:2, :], mod_ref[1, 1:2, :])
        h_scr[...] = (x * r * (1.0 + sc) + sh).astype(BF16)

    o_ref[...] = _dot(h_scr[...], w_ref[...])


def _in_proj(xall, mod, w_pad, ctx_rows):
    bsz, l, d = xall.shape
    ncol = w_pad.shape[1]
    tm = 768 if l % 768 == 0 else TM
    tn = 1024
    return pl.pallas_call(
        functools.partial(_inproj_kernel, tm=tm, ctx_rows=ctx_rows),
        out_shape=jax.ShapeDtypeStruct((bsz, l, ncol), F32),
        grid=(bsz, l // tm, ncol // tn),
        in_specs=[
            pl.BlockSpec((None, tm, d), lambda b, j, n: (b, j, 0)),
            pl.BlockSpec((None, 2, N_MOD, d), lambda b, j, n: (b, 0, 0, 0)),
            pl.BlockSpec((d, tn), lambda b, j, n: (0, n)),
        ],
        out_specs=pl.BlockSpec((None, tm, tn), lambda b, j, n: (b, j, n)),
        scratch_shapes=[pltpu.VMEM((tm, d), BF16)],
        compiler_params=_cp(("parallel", "parallel", "arbitrary")),
        name="in_proj",
    )(xall, mod, w_pad)


def _gmlp_kernel(u_ref, v_ref, lng_ref, lnb_ref, ws_ref, bias_ref, o_ref):
    v = _gelu(v_ref[...])
    mu = jnp.mean(v, axis=-1, keepdims=True)
    vc = v - mu
    var = jnp.mean(vc * vc, axis=-1, keepdims=True)
    v = (vc * lax.rsqrt(var + EPS) * lng_ref[...] + lnb_ref[...]).astype(BF16)
    for c in range(TM // GMLP_CHUNK):
        rows = slice(c * GMLP_CHUNK, (c + 1) * GMLP_CHUNK)
        for g in range(G_A):
            cols = slice(g * LANE, (g + 1) * LANE)
            mixed = _dot(ws_ref[g], v[rows, cols]) + bias_ref[:, cols]
            o_ref[rows, cols] = _gelu(u_ref[rows, cols]) * mixed


def _gmlp(p, ln_g, ln_b, w_s, bias):
    bsz, l, _ = p.shape
    da = ln_g.shape[-1]
    return pl.pallas_call(
        _gmlp_kernel,
        out_shape=jax.ShapeDtypeStruct((bsz, l, da), F32),
        grid=(bsz, l // TM),
        in_specs=[
            pl.BlockSpec((None, TM, da), lambda b, j: (b, j, 0)),
            pl.BlockSpec((None, TM, da), lambda b, j: (b, j, 1)),
            _const_spec((1, da)),
            _const_spec((1, da)),
            _const_spec((G_A, GMLP_CHUNK, GMLP_CHUNK)),
            _const_spec((GMLP_CHUNK, da)),
        ],
        out_specs=pl.BlockSpec((None, TM, da), lambda b, j: (b, j, 0)),
        compiler_params=_cp(("parallel", "parallel")),
        name="gmlp",
    )(p, p, ln_g, ln_b, w_s, bias)


def _scan_block(p, rev, nctx, nblk):
    if not rev:
        return p
    return jnp.where(p < nctx, nctx - 1 - p, nblk - 1 - (p - nctx))


def _hgrn_kernel(q_ref, f_ref, i_ref, lb_ref, o_ref, st_ref, *, rev):
    c_sz = HGRN_CHUNK
    nchunk = TM // c_sz

    @pl.when(pl.program_id(1) == 0)
    def _():
        st_ref[...] = jnp.zeros_like(st_ref)

    nsb = c_sz // SUB
    rowi = lax.broadcasted_iota(jnp.int32, (c_sz, LANE), 0)
    row8 = lax.broadcasted_iota(jnp.int32, (SUB, LANE), 0)
    rowa = lax.broadcasted_iota(jnp.int32, (c_sz, c_sz), 0)

    hpb = HGRN_HEADS_PER_BODY
    ngrp = H_B // hpb

    def chunk_body(it, carry):
        ci, hg = it // ngrp, it % ngrp
        c = (nchunk - 1 - ci) if rev else ci
        rows = pl.ds(pl.multiple_of(c * c_sz, c_sz), c_sz)
        for hl in range(hpb):
            if ngrp == 1:
                hh, cols = hl, slice(hl * LANE, (hl + 1) * LANE)
            else:
                hh = hg * hpb + hl
                cols = pl.ds(pl.multiple_of(hh * LANE, LANE), LANE)
            lbh = lb_ref[:, cols]
            st = st_ref[hh]
            qr = q_ref[rows, cols]
            q = qr * _sigmoid(qr)
            fg = lbh + (1.0 - lbh) * _sigmoid(f_ref[rows, cols])
            k = 1.0 - fg
            v = i_ref[rows, cols]
            b = jnp.log(fg)
            for s in (1, 2, 4, 8, 16):
                if rev:
                    b = b + jnp.where(rowi < c_sz - s, pltpu.roll(b, c_sz - s, 0), 0.0)
                else:
                    b = b + jnp.where(rowi >= s, pltpu.roll(b, s, 0), 0.0)
            b_last = b[0:1, :] if rev else b[c_sz - 1:c_sz, :]
            o = _dot_nt((q * jnp.exp(b)).astype(BF16), st.astype(BF16))
            diag = []
            for jb in range(nsb):
                r8 = slice(jb * SUB, (jb + 1) * SUB)
                bj, qj, kj, vj = b[r8], q[r8], k[r8], v[r8]
                oj = jnp.zeros((SUB, LANE), F32)
                for s in range(SUB):
                    valid = (row8 <= s) if rev else (row8 >= s)
                    dec = jnp.exp(bj - bj[s:s + 1, :])
                    pr = jnp.where(valid, qj * (kj[s:s + 1, :] * dec), 0.0)
                    oj = oj + jnp.sum(pr, axis=-1, keepdims=True) * vj[s:s + 1, :]
                diag.append(oj)
            o = o + jnp.concatenate(diag, axis=0)
            att = jnp.zeros((c_sz, c_sz), F32)
            for jb in (range(1, nsb) if rev else range(nsb - 1)):
                edge = jb * SUB if rev else jb * SUB + SUB - 1
                b_ref_row = b[edge:edge + 1, :]
                in_blk = (rowi >= jb * SUB) & (rowi < (jb + 1) * SUB)
                kt = jnp.where(in_blk, k * jnp.exp(jnp.minimum(b_ref_row - b, 0.0)), 0.0)
                qq = q * jnp.exp(jnp.minimum(b - b_ref_row, 0.0))
                a = _dot_nt(qq.astype(BF16), kt.astype(BF16))
                t_ok = (rowa < jb * SUB) if rev else (rowa >= (jb + 1) * SUB)
                att = att + jnp.where(t_ok, a, 0.0)
            o = o + _dot(att.astype(BF16), v.astype(BF16))
            o_ref[rows, cols] = o
            kk = k * jnp.exp(b_last - b)
            st_ref[hh] = st * jnp.exp(b_last) + _dot_tn(v.astype(BF16), kk.astype(BF16))
        return carry

    lax.fori_loop(0, nchunk * ngrp, chunk_body, 0)


def _hgrn(p, lb, rev, nctx):
    bsz, l, _ = p.shape
    db = lb.shape[-1]
    nblk = l // TM
    blk = functools.partial(_scan_block, rev=rev, nctx=nctx, nblk=nblk)
    fcol = 4 if rev else 3
    return pl.pallas_call(
        functools.partial(_hgrn_kernel, rev=rev),
        out_shape=jax.ShapeDtypeStruct((bsz, l, db), F32),
        grid=(bsz, nblk),
        in_specs=[
            pl.BlockSpec((None, TM, db), lambda b, p_: (b, blk(p_), 2)),
            pl.BlockSpec((None, TM, db), lambda b, p_: (b, blk(p_), fcol)),
            pl.BlockSpec((None, TM, db), lambda b, p_: (b, blk(p_), 5)),
            _const_spec((1, db)),
        ],
        out_specs=pl.BlockSpec((None, TM, db), lambda b, p_: (b, blk(p_), 0)),
        scratch_shapes=[pltpu.VMEM((H_B, LANE, LANE), F32)],
        compiler_params=_cp(("parallel", "arbitrary")),
        name="hgrn_bwd" if rev else "hgrn_fwd",
    )(p, p, p, lb)


def _s5_kernel(u_ref, bb_ref, lam_ref, cc_ref, o_ref, bu_scr, st_ref, *, rev, tb, bsz, nslab):
    half = S5_SLAB_G * S5_STATE

    @pl.when(pl.program_id(0) == 0)
    def _():
        st_ref[...] = jnp.zeros_like(st_ref)

    for j in range(nslab):
        uj = jnp.swapaxes(u_ref[:, :, j * LANE:(j + 1) * LANE], 0, 1).reshape(tb * bsz, LANE).astype(BF16)
        bu_scr[...] = _dot(uj, bb_ref[j])
        lre = jnp.broadcast_to(lam_ref[j, 0:1, :], (bsz, half))
        lim = jnp.broadcast_to(lam_ref[j, 1:2, :], (bsz, half))

        def step(i, carry):
            xre, xim = carry
            t = (tb - 1 - i) if rev else i
            rows = pl.ds(pl.multiple_of(t * bsz, bsz), bsz)
            nre = lre * xre - lim * xim + bu_scr[rows, 0:half]
            nim = lre * xim + lim * xre + bu_scr[rows, half:2 * half]
            bu_scr[rows, 0:half] = nre
            bu_scr[rows, half:2 * half] = nim
            return nre, nim

        xre, xim = lax.fori_loop(0, tb, step, (st_ref[j, :, 0:half], st_ref[j, :, half:2 * half]), unroll=4)
        st_ref[j, :, 0:half] = xre
        st_ref[j, :, half:2 * half] = xim
        y = _dot(bu_scr[...].astype(BF16), cc_ref[j])
        o_ref[:, :, j * LANE:(j + 1) * LANE] = jnp.swapaxes(y.reshape(tb, bsz, LANE), 0, 1)


def _s5(p, ucol, bbig, lam, cbig, rev, ctx_len):
    bsz, l, _ = p.shape
    nslab = bbig.shape[0]
    dc = nslab * LANE
    tb = 128
    nblk, nctx = l // tb, ctx_len // tb
    blk = functools.partial(_scan_block, rev=rev, nctx=nctx, nblk=nblk)
    half = S5_SLAB_G * S5_STATE
    return pl.pallas_call(
        functools.partial(_s5_kernel, rev=rev, tb=tb, bsz=bsz, nslab=nslab),
        out_shape=jax.ShapeDtypeStruct((bsz, l, dc), F32),
        grid=(nblk,),
        in_specs=[
            pl.BlockSpec((bsz, tb, ucol[1]), lambda p_: (0, blk(p_), ucol[0])),
            _const_spec((nslab, LANE, 2 * half)),
            _const_spec((nslab, 2, half)),
            _const_spec((nslab, 2 * half, LANE)),
        ],
        out_specs=pl.BlockSpec((bsz, tb, dc), lambda p_: (0, blk(p_), 0)),
        scratch_shapes=[pltpu.VMEM((tb * bsz, 2 * half), F32), pltpu.VMEM((nslab, bsz, 2 * half), F32)],
        compiler_params=_cp(("arbitrary",)),
        name="s5_bwd" if rev else "s5_fwd",
    )(p, bbig, lam, cbig)


def _s5_params(lam_re, lam_im, log_step, b_re, b_im, c_re, c_im):
    g_c, p_s = lam_re.shape
    nslab = g_c // S5_SLAB_G
    lam = lax.complex(lam_re.astype(F32), lam_im.astype(F32))
    lam_dt = lam * jnp.exp(log_step.astype(F32))[:, None]
    lam_bar = jnp.exp(lam_dt)
    b_bar = ((lam_bar - 1.0) / lam)[:, :, None] * lax.complex(b_re.astype(F32), b_im.astype(F32))
    eye = jnp.eye(S5_SLAB_G, dtype=F32)

    def in_map(m):
        m = m.reshape(nslab, S5_SLAB_G, p_s, S5_IN)
        return jnp.einsum('jgpd,gh->jgdhp', m, eye).reshape(nslab, S5_SLAB_G * S5_IN, S5_SLAB_G * p_s)

    def out_map(m):
        m = m.reshape(nslab, S5_SLAB_G, S5_IN, p_s)
        return jnp.einsum('jgdp,gh->jgphd', m, eye).reshape(nslab, S5_SLAB_G * p_s, S5_SLAB_G * S5_IN)

    bbig = jnp.concatenate([in_map(jnp.real(b_bar)), in_map(jnp.imag(b_bar))], axis=-1).astype(BF16)
    cbig = jnp.concatenate([out_map(c_re.astype(F32)), out_map(-c_im.astype(F32))], axis=1).astype(BF16)
    lam_s = jnp.stack([jnp.real(lam_bar).reshape(nslab, -1), jnp.imag(lam_bar).reshape(nslab, -1)], axis=1)
    return bbig, lam_s, cbig


def _merge_a_kernel(ya_ref, of_ref, ob_ref, bg_ref, yf_ref, yb_ref, cu_ref, ga_ref, gb_ref, gc_ref,
                    hng_ref, dsk_ref, wglu_ref, wa_ref, wb_ref, wc_ref, o_ref):
    dc = dsk_ref.shape[-1]
    o = of_ref[...] + ob_ref[...]
    parts = []
    for hh in range(H_B):
        oh = o[:, hh * LANE:(hh + 1) * LANE]
        parts.append(oh * lax.rsqrt(jnp.mean(oh * oh, axis=-1, keepdims=True) + EPS))
    bg = bg_ref[...]
    yb = jnp.concatenate(parts, axis=-1) * hng_ref[...] * (bg * _sigmoid(bg))
    y = yf_ref[...] + yb_ref[...] + dsk_ref[...] * cu_ref[:, 0:dc]
    z = _dot(_gelu(y).astype(BF16), wglu_ref[...])
    yc = z[:, 0:dc] * _sigmoid(z[:, dc:2 * dc])
    merged = (_sigmoid(ga_ref[...]) * _dot(ya_ref[...].astype(BF16), wa_ref[...])
              + _sigmoid(gb_ref[...]) * _dot(yb.astype(BF16), wb_ref[...])
              + _sigmoid(gc_ref[...]) * _dot(yc.astype(BF16), wc_ref[...]))
    o_ref[...] = merged.astype(BF16)


def _merge_a(ya, o_f, o_b, p, y_f, y_b, hn_g, d_skip, w_glu, wa, wb, wc, off):
    bsz, l, da = ya.shape
    dc = y_f.shape[-1]
    d = wa.shape[-1]
    tm = 256
    o2 = off * (TM // tm)
    nblk = l // tm - o2
    tok = lambda b, j: (b, j + o2, 0)
    pcol = lambda c: (lambda b, j: (b, j + o2, c))
    return pl.pallas_call(
        _merge_a_kernel,
        out_shape=jax.ShapeDtypeStruct((bsz, nblk * tm, d), BF16),
        grid=(bsz, nblk),
        in_specs=[
            pl.BlockSpec((None, tm, da), tok),
            pl.BlockSpec((None, tm, da), tok),
            pl.BlockSpec((None, tm, da), tok),
            pl.BlockSpec((None, tm, da), pcol(6)),
            pl.BlockSpec((None, tm, dc), tok),
            pl.BlockSpec((None, tm, dc), tok),
            pl.BlockSpec((None, tm, da), pcol(7)),
            pl.BlockSpec((None, tm, d), pcol(4)),
            pl.BlockSpec((None, tm, d), pcol(5)),
            pl.BlockSpec((None, tm, d), pcol(6)),
            _const_spec((1, da)),
            _const_spec((1, dc)),
            _const_spec(w_glu.shape),
            _const_spec(wa.shape),
            _const_spec(wb.shape),
            _const_spec(wc.shape),
        ],
        out_specs=pl.BlockSpec((None, tm, d), lambda b, j: (b, j, 0)),
        compiler_params=_cp(("parallel", "parallel")),
        name="merge_a",
    )(ya, o_f, o_b, p, y_f, y_b, p, p, p, p, hn_g, d_skip, w_glu, wa, wb, wc)


def _merge_b_kernel(m_ref, x_ref, mod_ref, wo_ref, wq_ref, x1_ref, h2_ref, q_ref):
    x1 = x_ref[...] + mod_ref[2:3, :] * _dot(m_ref[...], wo_ref[...])
    x1_ref[...] = x1
    r = lax.rsqrt(jnp.mean(x1 * x1, axis=-1, keepdims=True) + EPS)
    h2 = x1 * r * (1.0 + mod_ref[4:5, :]) + mod_ref[3:4, :]
    h2_ref[...] = h2
    q_ref[...] = _dot(h2.astype(BF16), wq_ref[...])


def _merge_b(merged, xall, mod, wo, wq, off, nctx):
    bsz, lo, d = merged.shape
    nblk = lo // TM
    dq = wq.shape[-1]
    out = lambda b, j: (b, j, 0)
    return pl.pallas_call(
        _merge_b_kernel,
        out_shape=(jax.ShapeDtypeStruct((bsz, lo, d), F32), jax.ShapeDtypeStruct((bsz, lo, d), F32),
                   jax.ShapeDtypeStruct((bsz, lo, dq), F32)),
        grid=(bsz, nblk),
        in_specs=[
            pl.BlockSpec((None, TM, d), out),
            pl.BlockSpec((None, TM, d), lambda b, j: (b, j + off, 0)),
            pl.BlockSpec((None, None, N_MOD, d), lambda b, j: (b, (j + off >= nctx).astype(jnp.int32), 0, 0)),
            _const_spec(wo.shape),
            _const_spec(wq.shape),
        ],
        out_specs=(pl.BlockSpec((None, TM, d), out), pl.BlockSpec((None, TM, d), out),
                   pl.BlockSpec((None, TM, dq), out)),
        compiler_params=_cp(("parallel", "parallel")),
        name="merge_b",
    )(merged, xall, mod, wo, wq)


def _topk_rows(s, row_id, k):
    slot = lax.broadcasted_iota(jnp.int32, (k, s.shape[1]), 0)
    vals = jnp.zeros((k, s.shape[1]), F32)
    ids = jnp.zeros((k, s.shape[1]), F32)
    for r in range(k):
        m = jnp.max(s, axis=0, keepdims=True)
        am = jnp.min(jnp.where(s == m, row_id, 1e9), axis=0, keepdims=True)
        vals = jnp.where(slot == r, m, vals)
        ids = jnp.where(slot == r, am, ids)
        s = jnp.where(row_id == am, -jnp.inf, s)
    return vals, ids


def _pair_candidates():
    k = PEER_TOPK
    pieces = [("a", 0, 0, 16), ("a", 1, 0, 8), ("a", 2, 0, 8), ("a", 3, 0, 8), ("a", 4, 0, 8),
              ("b", 0, 0, 8), ("b", 1, 0, 8), ("b", 0, 8, 16)]
    ids, bias, seen = [], [], set()
    for kind, fixed, lo, hi in pieces:
        for r in range(lo, hi):
            a, b = (fixed, r) if kind == "a" else (r, fixed)
            ok = (a + 1) * (b + 1) <= k and (a, b) not in seen
            if ok:
                seen.add((a, b))
            ids.append(float(a * k + b))
            bias.append(0.0 if ok else -math.inf)
    assert len(seen) == sum(k // (a + 1) for a in range(k))
    return pieces, ids, bias


def _route_kernel(q_ref, keys_ref, cid_ref, cbias_ref, idx_ref, g_ref, it_scr, gt_scr):
    k = PEER_TOPK
    pieces, _, _ = _pair_candidates()
    key_id = lax.broadcasted_iota(jnp.int32, (PEER_NKEYS, q_ref.shape[0]), 0).astype(F32)
    for h in range(PEER_HEADS):
        sv, si = [], []
        for half in range(2):
            c0 = (h * 2 + half) * PEER_NKEYS
            qh = q_ref[:, c0:c0 + PEER_NKEYS].astype(BF16)
            s = _dot_nt(keys_ref[h, half], qh)
            v_, i_ = _topk_rows(s, key_id, k)
            sv.append(v_)
            si.append(i_)
        cand = jnp.concatenate(
            [sv[0][f:f + 1, :] + sv[1][lo:hi, :] if kind == "a" else sv[0][lo:hi, :] + sv[1][f:f + 1, :]
             for kind, f, lo, hi in pieces], axis=0) + cbias_ref[...]
        fv, fi = _topk_rows(cand, cid_ref[...], k)
        fa = jnp.floor(fi * (1.0 / k))
        fb = fi - fa * k
        i1 = jnp.zeros_like(fi)
        i2 = jnp.zeros_like(fi)
        for a in range(k):
            i1 = jnp.where(fa == a, si[0][a:a + 1, :], i1)
            i2 = jnp.where(fb == a, si[1][a:a + 1, :], i2)
        e = jnp.exp(fv - fv[0:1, :])
        gt_scr[h * k:(h + 1) * k, :] = e / jnp.sum(e, axis=0, keepdims=True)
        it_scr[h * k:(h + 1) * k, :] = i1 * PEER_NKEYS + i2
    g_ref[...] = gt_scr[...].T
    idx_ref[...] = it_scr[...].T.astype(jnp.int32)


def _route(q, keys):
    t, dq = q.shape
    tm = LANE
    nsel = PEER_HEADS * PEER_TOPK
    _, ids, bias = _pair_candidates()
    cid = jnp.broadcast_to(jnp.asarray(ids, F32)[:, None], (len(ids), tm))
    cbias = jnp.broadcast_to(jnp.asarray(bias, F32)[:, None], (len(bias), tm))
    return pl.pallas_call(
        _route_kernel,
        out_shape=(jax.ShapeDtypeStruct((t, nsel), jnp.int32), jax.ShapeDtypeStruct((t, nsel), F32)),
        grid=(t // tm,),
        in_specs=[pl.BlockSpec((tm, dq), lambda i: (i, 0)), _const_spec(keys.shape),
                  _const_spec(cid.shape), _const_spec(cbias.shape)],
        out_specs=(pl.BlockSpec((tm, nsel), lambda i: (i, 0)), pl.BlockSpec((tm, nsel), lambda i: (i, 0))),
        scratch_shapes=[pltpu.VMEM((nsel, tm), F32), pltpu.VMEM((nsel, tm), F32)],
        compiler_params=_cp(("parallel",)),
        name="peer_route",
    )(q, keys, cid, cbias)


PEER_TB = 64
PEER_SLOTS = 4
PEER_AHEAD = PEER_SLOTS - 1
HALF_SUB = 8
HI_MASK = -65536


def _pack_expert_table(u_all, v_all, layer):
    _, n_exp, d = u_all.shape
    tb = 256
    assert d == 2 * HALF_SUB * LANE and n_exp % tb == 0

    def pack(x):
        bits = lax.bitcast_convert_type(x.astype(BF16).astype(F32), jnp.int32)
        word = (bits[:, :d // 2] & HI_MASK) | lax.shift_right_logical(bits[:, d // 2:], 16)
        return word.reshape(x.shape[0], HALF_SUB, LANE)

    def body(u_ref, v_ref, o_ref):
        o_ref[:, 0:HALF_SUB, :] = pack(u_ref[...])
        o_ref[:, HALF_SUB:2 * HALF_SUB, :] = pack(v_ref[...])

    return pl.pallas_call(
        body,
        out_shape=jax.ShapeDtypeStruct((n_exp, 2 * HALF_SUB, LANE), jnp.int32),
        grid=(n_exp // tb,),
        in_specs=[pl.BlockSpec((None, tb, d), lambda i: (layer, i, 0)),
                  pl.BlockSpec((None, tb, d), lambda i: (layer, i, 0))],
        out_specs=pl.BlockSpec((tb, 2 * HALF_SUB, LANE), lambda i: (i, 0, 0)),
        compiler_params=_cp(("parallel",)),
        name="pack_experts",
    )(u_all, v_all)


def _unpack_pair(word):
    return lax.bitcast_convert_type(word & HI_MASK, F32), lax.bitcast_convert_type(word << 16, F32)


def _expert_token(words, hrow, grow, eye, after_step):
    acc = None
    for cc in range(HALF_SUB):
        u_hi, u_lo = _unpack_pair(words(cc))
        term = (u_hi * hrow[:, cc * LANE:(cc + 1) * LANE]
                + u_lo * hrow[:, (HALF_SUB + cc) * LANE:(HALF_SUB + cc + 1) * LANE])
        acc = term if acc is None else acc + term
        after_step(cc)
    act = jnp.sum(acc, axis=-1, keepdims=True)
    gcol = jnp.sum(jnp.where(eye, grow, 0.0), axis=-1, keepdims=True)
    w = _gelu(act) * gcol
    outs = [None] * (2 * HALF_SUB)
    for cc in range(HALF_SUB):
        v_hi, v_lo = _unpack_pair(words(HALF_SUB + cc))
        outs[cc] = jnp.sum(w * v_hi, axis=0, keepdims=True)
        outs[HALF_SUB + cc] = jnp.sum(w * v_lo, axis=0, keepdims=True)
        after_step(HALF_SUB + cc)
    return jnp.concatenate(outs, axis=-1)


def _experts_kernel(idx_ref, g_ref, h_ref, x_ref, mod_ref, fin_ref, tab_ref, o_ref, *scratch, final_norm):
    bufs, sem = scratch[:PEER_SLOTS], scratch[PEER_SLOTS]
    nsel = PEER_HEADS * PEER_TOPK
    per = nsel // (2 * HALF_SUB)
    step = pl.program_id(0)

    def issue(t, slot, j0=0, j1=nsel):
        for j in range(j0, j1):
            e = idx_ref[0, t * nsel + j]
            pltpu.make_async_copy(tab_ref.at[e], bufs[slot].at[:, j, :], sem.at[slot]).start(priority=j % 2)

    def wait(slot):
        pltpu.make_async_copy(bufs[slot], bufs[slot], sem.at[slot]).wait()

    @pl.when(step == 0)
    def _():
        for s in range(PEER_AHEAD):
            issue(s, s)

    eye = lax.broadcasted_iota(jnp.int32, (nsel, nsel), 0) == lax.broadcasted_iota(jnp.int32, (nsel, nsel), 1)

    def group(t4, c):
        for s in range(PEER_SLOTS):
            t = t4 * PEER_SLOTS + s
            wait(s)
            nxt = (s + PEER_AHEAD) % PEER_SLOTS
            row = pl.ds(t, 1)
            buf = bufs[s]
            out = _expert_token(lambda r: buf[r], h_ref[row, :], g_ref[row, :], eye,
                                lambda i: issue(t + PEER_AHEAD, nxt, i * per, (i + 1) * per))
            o_ref[row, :] = x_ref[row, :] + mod_ref[5:6, :] * out
        return c

    lax.fori_loop(0, PEER_TB // PEER_SLOTS, group, 0)

    if final_norm:
        y = o_ref[...]
        o_ref[...] = y * lax.rsqrt(jnp.mean(y * y, axis=-1, keepdims=True) + EPS) * fin_ref[...]

    @pl.when(step == pl.num_programs(0) - 1)
    def _():
        for s in range(PEER_AHEAD):
            wait(s)


def _experts(idx, g, h2, x1, mod, fin_g, tab, off, nctx, final_norm, n_direct):
    t_all, d = h2.shape
    bsz = mod.shape[0]
    nsel = idx.shape[-1]
    tb = PEER_TB
    nb = n_direct // tb
    per_b = t_all // bsz // tb
    per = TM // tb
    assert d == 2 * HALF_SUB * LANE and tb % PEER_SLOTS == 0 and n_direct % tb == 0
    main = idx[:n_direct].reshape(nb, tb * nsel)
    idx_ext = jnp.concatenate([main, jnp.roll(main, -1, axis=0)[:, :PEER_AHEAD * nsel]], axis=1)
    idx_ext = idx_ext.reshape(nb, 1, (tb + PEER_AHEAD) * nsel)
    tokb = lambda i: (i, 0)

    def mod_map(i):
        return (i // per_b, ((i % per_b) // per + off >= nctx).astype(jnp.int32), 0, 0)

    return pl.pallas_call(
        functools.partial(_experts_kernel, final_norm=final_norm),
        out_shape=jax.ShapeDtypeStruct((t_all, d), F32),
        grid=(nb,),
        in_specs=[
            pl.BlockSpec((None, 1, (tb + PEER_AHEAD) * nsel), lambda i: (i, 0, 0), memory_space=pltpu.SMEM),
            pl.BlockSpec((tb, nsel), tokb),
            pl.BlockSpec((tb, d), tokb),
            pl.BlockSpec((tb, d), tokb),
            pl.BlockSpec((None, None, N_MOD, d), mod_map),
            _const_spec((1, d)),
            pl.BlockSpec(memory_space=pl.ANY),
        ],
        out_specs=pl.BlockSpec((tb, d), tokb),
        scratch_shapes=[pltpu.VMEM((2 * HALF_SUB, nsel, LANE), jnp.int32) for _ in range(PEER_SLOTS)]
        + [pltpu.SemaphoreType.DMA((PEER_SLOTS,))],
        compiler_params=_cp(("arbitrary",)),
        name="peer_experts",
    )(idx_ext, g, h2, x1, mod, fin_g, tab)


SC_CHUNK = 128
STAGED_TB = 8
STAGED_NUM, STAGED_DEN = 3, 16


def _sc_gather_rows(tab, idx):
    info = plsc.get_sparse_core_info()
    nc, nw = info.num_cores, info.num_cores * info.num_subcores
    n_exp, nsub, lane = tab.shape
    ts, nsel = idx.shape
    idx_flat = (idx[:, None, :] * nsub + jnp.arange(nsub, dtype=idx.dtype)[None, :, None]).reshape(-1)
    nrows = idx_flat.shape[0]
    rpw = nrows // nw
    assert nrows % (nw * SC_CHUNK) == 0
    mesh = plsc.VectorSubcoreMesh(core_axis_name="c", subcore_axis_name="s")

    @functools.partial(
        pl.kernel, mesh=mesh,
        out_type=jax.ShapeDtypeStruct((nrows, lane), tab.dtype),
        scratch_types=[pltpu.VMEM((SC_CHUNK,), jnp.int32), pltpu.VMEM((SC_CHUNK, lane), tab.dtype),
                       pltpu.SemaphoreType.DMA],
    )
    def gather(tab_hbm, idx_hbm, out_hbm, idx_v, rows_v, sem):
        base = (lax.axis_index("s") * nc + lax.axis_index("c")) * rpw

        @pl.loop(0, rpw // SC_CHUNK)
        def _(ci):
            r0 = pl.multiple_of(base + ci * SC_CHUNK, SC_CHUNK)
            pltpu.sync_copy(idx_hbm.at[pl.ds(r0, SC_CHUNK)], idx_v)
            pltpu.async_copy(tab_hbm.at[idx_v], rows_v, sem).wait()
            pltpu.sync_copy(rows_v, out_hbm.at[pl.ds(r0, SC_CHUNK)])

    return gather(tab.reshape(n_exp * nsub, lane), idx_flat).reshape(ts, nsub, nsel, lane)


def _experts_staged_kernel(rows_ref, g_ref, h_ref, x_ref, mod_ref, fin_ref, y_in_ref, o_ref, *, final_norm):
    del y_in_ref
    nsel = PEER_HEADS * PEER_TOPK
    eye = lax.broadcasted_iota(jnp.int32, (nsel, nsel), 0) == lax.broadcasted_iota(jnp.int32, (nsel, nsel), 1)

    def tok(t, c):
        row = pl.ds(t, 1)
        out = _expert_token(lambda r: rows_ref[t, r], h_ref[row, :], g_ref[row, :], eye, lambda i: None)
        o_ref[row, :] = x_ref[row, :] + mod_ref[5:6, :] * out
        return c

    lax.fori_loop(0, STAGED_TB, tok, 0)
    if final_norm:
        y = o_ref[...]
        o_ref[...] = y * lax.rsqrt(jnp.mean(y * y, axis=-1, keepdims=True) + EPS) * fin_ref[...]


def _experts_staged(rows, g, h2, x1, mod, fin_g, y_direct, off, nctx, final_norm, n_direct):
    t_all, d = h2.shape
    bsz = mod.shape[0]
    nsel = g.shape[-1]
    tb = STAGED_TB
    b0 = n_direct // tb
    nb = (t_all - n_direct) // tb
    per_b = t_all // bsz // tb
    per = TM // tb
    tokb = lambda i: (i + b0, 0)

    def mod_map(i):
        gi = i + b0
        return (gi // per_b, ((gi % per_b) // per + off >= nctx).astype(jnp.int32), 0, 0)

    return pl.pallas_call(
        functools.partial(_experts_staged_kernel, final_norm=final_norm),
        out_shape=jax.ShapeDtypeStruct((t_all, d), F32),
        grid=(nb,),
        in_specs=[
            pl.BlockSpec((tb, 2 * HALF_SUB, nsel, LANE), lambda i: (i, 0, 0, 0)),
            pl.BlockSpec((tb, nsel), tokb),
            pl.BlockSpec((tb, d), tokb),
            pl.BlockSpec((tb, d), tokb),
            pl.BlockSpec((None, None, N_MOD, d), mod_map),
            _const_spec((1, d)),
            pl.BlockSpec(memory_space=pl.ANY),
        ],
        out_specs=pl.BlockSpec((tb, d), tokb),
        input_output_aliases={6: 0},
        compiler_params=_cp(("parallel",)),
        name="peer_experts_staged",
    )(rows, g, h2, x1, mod, fin_g, y_direct)


def _pos_embed_2d(n, d):
    n_rows = n // GRID_W
    quarter = d // 4
    omega = 1.0 / (POS_THETA ** (jnp.arange(quarter, dtype=F32) / quarter))

    def axis_code(length):
        ang = jnp.arange(length, dtype=F32)[:, None] * omega[None]
        return jnp.concatenate([jnp.sin(ang), jnp.cos(ang)], axis=-1)

    er = jnp.broadcast_to(axis_code(n_rows)[:, None], (n_rows, GRID_W, d // 2))
    ec = jnp.broadcast_to(axis_code(GRID_W)[None], (n_rows, GRID_W, d // 2))
    return jnp.concatenate([er, ec], axis=-1).reshape(n, d)


def kernel(x, c, ctx, c_ctx, w_mod, b_mod, w_in, gmlp_ln_g, gmlp_ln_b, gmlp_w_s, gmlp_b_s, hgrn_lb_logits, hgrn_norm_g, s5_lam_re, s5_lam_im, s5_log_step, s5_b_re, s5_b_im, s5_c_re, s5_c_im, s5_d, s5_w_glu, w_branch_a, w_branch_b, w_branch_c, w_out, peer_w_q, peer_sub_keys, peer_u, peer_v, final_norm_g):
    bsz, n, d = x.shape
    ctx_len = ctx.shape[1]
    depth = w_mod.shape[0]
    nctx = ctx_len // TM
    da = gmlp_ln_g.shape[-1]
    db = hgrn_norm_g.shape[-1]
    dc = s5_d.shape[-1]
    assert n % TM == 0 and ctx_len % TM == 0 and bsz % SUB == 0
    assert da == db and d == 2 * da and dc % LANE == 0 and dc <= da
    assert w_in.shape[-1] == 2 * da + 5 * db + dc + 3 * d

    xall = _embed(x, ctx, _pos_embed_2d(n, d).astype(x.dtype))

    rows = -(-(bsz + 1) // 8) * 8
    cc = jnp.zeros((rows, d), F32).at[0].set(c_ctx).at[1:bsz + 1].set(c)
    mod_all = _modulation(cc, w_mod, b_mod)

    gamma = jax.nn.softmax(hgrn_lb_logits.astype(F32), axis=0)
    lower_bounds = jnp.cumsum(gamma, axis=0) - gamma[0:1]

    for l in range(depth):
        last = l == depth - 1
        off = nctx if last else 0
        mod_c = jnp.broadcast_to(mod_all[l, 0][None], (bsz, N_MOD * d))
        mod = jnp.stack([mod_c, mod_all[l, 1:bsz + 1]], axis=1).reshape(bsz, 2, N_MOD, d)

        cu0 = 2 * da + 5 * db
        w_pad = jnp.concatenate([w_in[l][:, :cu0 + dc], jnp.zeros((d, da - dc), F32), w_in[l][:, cu0 + dc:]],
                                axis=1).astype(BF16)
        p = _in_proj(xall, mod, w_pad, ctx_len)

        bias = jnp.repeat(gmlp_b_s[l].T, da // G_A, axis=1)
        ya = _gmlp(p, gmlp_ln_g[l].reshape(1, da), gmlp_ln_b[l].reshape(1, da), gmlp_w_s[l].astype(BF16), bias)

        o_f = _hgrn(p, lower_bounds[l, 0].reshape(1, db), False, nctx)
        o_b = _hgrn(p, lower_bounds[l, 1].reshape(1, db), True, nctx)

        ys = []
        for r in range(2):
            prm = _s5_params(s5_lam_re[l, r], s5_lam_im[l, r], s5_log_step[l, r], s5_b_re[l, r], s5_b_im[l, r],
                             s5_c_re[l, r], s5_c_im[l, r])
            ys.append(_s5(p, (cu0 // da, da), *prm, rev=bool(r), ctx_len=ctx_len))

        merged = _merge_a(ya, o_f, o_b, p, ys[0], ys[1], hgrn_norm_g[l].reshape(1, db), s5_d[l].reshape(1, dc),
                          s5_w_glu[l].astype(BF16), w_branch_a[l].astype(BF16), w_branch_b[l].astype(BF16),
                          w_branch_c[l].astype(BF16), off)
        x1, h2, q = _merge_b(merged, xall, mod, w_out[l].astype(BF16), peer_w_q[l].astype(BF16), off, nctx)

        lo = x1.shape[1]
        idx, g = _route(q.reshape(bsz * lo, -1), peer_sub_keys[l].astype(BF16))
        tab = _pack_expert_table(peer_u, peer_v, l)
        t_all = bsz * lo
        n_direct = t_all - t_all * STAGED_NUM // STAGED_DEN
        rows = _sc_gather_rows(tab, idx[n_direct:])
        ex_args = (g, h2.reshape(t_all, d), x1.reshape(t_all, d), mod, final_norm_g.astype(F32).reshape(1, d))
        y = _experts(idx, *ex_args, tab, off, nctx, final_norm=last, n_direct=n_direct)
        y = _experts_staged(rows, *ex_args, y, off, nctx, final_norm=last, n_direct=n_direct)
        xall = y.reshape(bsz, lo, d)

    return xall
```

```python
import functools
import math

import jax
import jax.numpy as jnp
from jax import lax
from jax.experimental import pallas as pl
from jax.experimental.pallas import tpu as pltpu

F32 = jnp.float32
BF16 = jnp.bfloat16

EPS = 1e-6
GRID_W = 64
POS_THETA = 10000.0
N_MOD = 6
G_A = 8
GMLP_CHUNK = 128
H_B = 8
HGRN_CHUNK = 32
HGRN_HEADS_PER_BODY = 8
S5_IN = 16
S5_STATE = 64
S5_SLAB_G = 8
PEER_HEADS = 8
PEER_NKEYS = 128
PEER_TOPK = 16

TM = 256
LANE = 128
SUB = 8
VMEM_LIMIT = 56 * 1024 * 1024


def _cp(sem, vmem=VMEM_LIMIT):
    return pltpu.CompilerParams(dimension_semantics=sem, vmem_limit_bytes=vmem)


def _const_spec(shape):
    nd = len(shape)
    return pl.BlockSpec(shape, lambda *_: (0,) * nd, pipeline_mode=pl.Buffered(1))


def _gelu(x):
    return x * (lax.erf(x * (1.0 / math.sqrt(2.0))) + 1.0) * 0.5


def _sigmoid(x):
    return jax.nn.sigmoid(x)


def _dot(a, b):
    return jnp.dot(a, b, preferred_element_type=F32)


def _dot_nt(a, b):
    return lax.dot_general(a, b, (((1,), (1,)), ((), ())), preferred_element_type=F32)


def _dot_tn(a, b):
    return lax.dot_general(a, b, (((0,), (0,)), ((), ())), preferred_element_type=F32)


def _embed_kernel(ctx_ref, x_ref, pos_ref, o_ref, *, nctx):
    j = pl.program_id(1)

    @pl.when(j < nctx)
    def _():
        o_ref[...] = ctx_ref[...]

    @pl.when(j >= nctx)
    def _():
        o_ref[...] = x_ref[...] + pos_ref[...]


def _embed(x, ctx, pos):
    bsz, n, d = x.shape
    nctx = ctx.shape[1] // TM
    nblk = nctx + n // TM
    return pl.pallas_call(
        functools.partial(_embed_kernel, nctx=nctx),
        out_shape=jax.ShapeDtypeStruct((bsz, nblk * TM, d), x.dtype),
        grid=(bsz, nblk),
        in_specs=[
            pl.BlockSpec((None, TM, d), lambda b, j: (b, jnp.minimum(j, nctx - 1), 0)),
            pl.BlockSpec((None, TM, d), lambda b, j: (b, jnp.maximum(j - nctx, 0), 0)),
            pl.BlockSpec((TM, d), lambda b, j: (jnp.maximum(j - nctx, 0), 0)),
        ],
        out_specs=pl.BlockSpec((None, TM, d), lambda b, j: (b, j, 0)),
        compiler_params=_cp(("parallel", "arbitrary")),
        name="embed",
    )(ctx, x, pos)


def _mod_kernel(c_ref, w_ref, b_ref, o_ref):
    c = c_ref[...]
    s = c * _sigmoid(c)
    o_ref[...] = _dot(s.astype(BF16), w_ref[...].astype(BF16)) + b_ref[...]


def _modulation(cc, w_mod, b_mod):
    depth, d, nd = w_mod.shape
    rows = cc.shape[0]
    tn = 1024
    return pl.pallas_call(
        _mod_kernel,
        out_shape=jax.ShapeDtypeStruct((depth, rows, nd), F32),
        grid=(depth, nd // tn),
        in_specs=[
            pl.BlockSpec((rows, d), lambda l, n: (0, 0)),
            pl.BlockSpec((None, d, tn), lambda l, n: (l, 0, n)),
            pl.BlockSpec((None, 1, tn), lambda l, n: (l, 0, n)),
        ],
        out_specs=pl.BlockSpec((None, rows, tn), lambda l, n: (l, 0, n)),
        compiler_params=_cp(("parallel", "arbitrary")),
        name="modulation",
    )(cc, w_mod, b_mod.reshape(depth, 1, nd))


def _inproj_kernel(x_ref, mod_ref, w_ref, o_ref, h_scr, *, tm, ctx_rows):
    j = pl.program_id(1)

    @pl.when(pl.program_id(2) == 0)
    def _():
        x = x_ref[...]
        r = lax.rsqrt(jnp.mean(x * x, axis=-1, keepdims=True) + EPS)
        row = j * tm + lax.broadcasted_iota(jnp.int32, (tm, 1), 0)
        is_ctx = row < ctx_rows
        sh = jnp.where(is_ctx, mod_ref[0, 0:1, :], mod_ref[1, 0:1, :])
        sc = jnp.where(is_ctx, mod_ref[0, 1:2, :], mod_ref[1, 1:2, :])
        h_scr[...] = (x * r * (1.0 + sc) + sh).astype(BF16)

    o_ref[...] = _dot(h_scr[...], w_ref[...])


def _in_proj(xall, mod, w_pad, ctx_rows):
    bsz, l, d = xall.shape
    ncol = w_pad.shape[1]
    tm = 768 if l % 768 == 0 else TM
    tn = 1024
    return pl.pallas_call(
        functools.partial(_inproj_kernel, tm=tm, ctx_rows=ctx_rows),
        out_shape=jax.ShapeDtypeStruct((bsz, l, ncol), F32),
        grid=(bsz, l // tm, ncol // tn),
        in_specs=[
            pl.BlockSpec((None, tm, d), lambda b, j, n: (b, j, 0)),
            pl.BlockSpec((None, 2, N_MOD, d), lambda b, j, n: (b, 0, 0, 0)),
            pl.BlockSpec((d, tn), lambda b, j, n: (0, n)),
        ],
        out_specs=pl.BlockSpec((None, tm, tn), lambda b, j, n: (b, j, n)),
        scratch_shapes=[pltpu.VMEM((tm, d), BF16)],
        compiler_params=_cp(("parallel", "parallel", "arbitrary")),
        name="in_proj",
    )(xall, mod, w_pad)


def _gmlp_kernel(u_ref, v_ref, lng_ref, lnb_ref, ws_ref, bias_ref, o_ref):
    v = _gelu(v_ref[...])
    mu = jnp.mean(v, axis=-1, keepdims=True)
    vc = v - mu
    var = jnp.mean(vc * vc, axis=-1, keepdims=True)
    v = (vc * lax.rsqrt(var + EPS) * lng_ref[...] + lnb_ref[...]).astype(BF16)
    for c in range(TM // GMLP_CHUNK):
        rows = slice(c * GMLP_CHUNK, (c + 1) * GMLP_CHUNK)
        for g in range(G_A):
            cols = slice(g * LANE, (g + 1) * LANE)
            mixed = _dot(ws_ref[g], v[rows, cols]) + bias_ref[:, cols]
            o_ref[rows, cols] = _gelu(u_ref[rows, cols]) * mixed


def _gmlp(p, ln_g, ln_b, w_s, bias):
    bsz, l, _ = p.shape
    da = ln_g.shape[-1]
    return pl.pallas_call(
        _gmlp_kernel,
        out_shape=jax.ShapeDtypeStruct((bsz, l, da), F32),
        grid=(bsz, l // TM),
        in_specs=[
            pl.BlockSpec((None, TM, da), lambda b, j: (b, j, 0)),
            pl.BlockSpec((None, TM, da), lambda b, j: (b, j, 1)),
            _const_spec((1, da)),
            _const_spec((1, da)),
            _const_spec((G_A, GMLP_CHUNK, GMLP_CHUNK)),
            _const_spec((GMLP_CHUNK, da)),
        ],
        out_specs=pl.BlockSpec((None, TM, da), lambda b, j: (b, j, 0)),
        compiler_params=_cp(("parallel", "parallel")),
        name="gmlp",
    )(p, p, ln_g, ln_b, w_s, bias)


def _scan_block(p, rev, nctx, nblk):
    if not rev:
        return p
    return jnp.where(p < nctx, nctx - 1 - p, nblk - 1 - (p - nctx))


def _hgrn_kernel(q_ref, f_ref, i_ref, lb_ref, o_ref, st_ref, *, rev):
    c_sz = HGRN_CHUNK
    nchunk = TM // c_sz

    @pl.when(pl.program_id(1) == 0)
    def _():
        st_ref[...] = jnp.zeros_like(st_ref)

    nsb = c_sz // SUB
    rowi = lax.broadcasted_iota(jnp.int32, (c_sz, LANE), 0)
    row8 = lax.broadcasted_iota(jnp.int32, (SUB, LANE), 0)

    hpb = HGRN_HEADS_PER_BODY
    ngrp = H_B // hpb

    def chunk_body(it, carry):
        ci, hg = it // ngrp, it % ngrp
        c = (nchunk - 1 - ci) if rev else ci
        rows = pl.ds(pl.multiple_of(c * c_sz, c_sz), c_sz)
        for hl in range(hpb):
            if ngrp == 1:
                hh, cols = hl, slice(hl * LANE, (hl + 1) * LANE)
            else:
                hh = hg * hpb + hl
                cols = pl.ds(pl.multiple_of(hh * LANE, LANE), LANE)
            lbh = lb_ref[:, cols]
            st = st_ref[hh]
            qr = q_ref[rows, cols]
            q = qr * _sigmoid(qr)
            fg = lbh + (1.0 - lbh) * _sigmoid(f_ref[rows, cols])
            k = 1.0 - fg
            v = i_ref[rows, cols]
            b = jnp.log(fg)
            for s in (1, 2, 4, 8, 16):
                if rev:
                    b = b + jnp.where(rowi < c_sz - s, pltpu.roll(b, c_sz - s, 0), 0.0)
                else:
                    b = b + jnp.where(rowi >= s, pltpu.roll(b, s, 0), 0.0)
            b_last = b[0:1, :] if rev else b[c_sz - 1:c_sz, :]
            o = _dot_nt((q * jnp.exp(b)).astype(BF16), st.astype(BF16))
            diag = []
            for jb in range(nsb):
                r8 = slice(jb * SUB, (jb + 1) * SUB)
                bj, qj, kj, vj = b[r8], q[r8], k[r8], v[r8]
                oj = jnp.zeros((SUB, LANE), F32)
                for s in range(SUB):
                    valid = (row8 <= s) if rev else (row8 >= s)
                    dec = jnp.exp(bj - bj[s:s + 1, :])
                    pr = jnp.where(valid, qj * (kj[s:s + 1, :] * dec), 0.0)
                    oj = oj + jnp.sum(pr, axis=-1, keepdims=True) * vj[s:s + 1, :]
                diag.append(oj)
            o = o + jnp.concatenate(diag, axis=0)
            qqs, kts = [], []
            for jb in (range(1, nsb) if rev else range(nsb - 1)):
                edge = jb * SUB if rev else jb * SUB + SUB - 1
                b_ref_row = b[edge:edge + 1, :]
                src = slice(jb * SUB, (jb + 1) * SUB)
                dst = slice(0, jb * SUB) if rev else slice((jb + 1) * SUB, c_sz)
                kt8 = k[src] * jnp.exp(jnp.minimum(b_ref_row - b[src], 0.0))
                qd = q[dst] * jnp.exp(jnp.minimum(b[dst] - b_ref_row, 0.0))
                zk_lo = jnp.zeros((jb * SUB, LANE), F32)
                zk_hi = jnp.zeros((c_sz - (jb + 1) * SUB, LANE), F32)
                kts.append(jnp.concatenate([z for z in (zk_lo, kt8, zk_hi) if z.shape[0]], axis=0))
                zq = jnp.zeros((c_sz - qd.shape[0], LANE), F32)
                qqs.append(jnp.concatenate([qd, zq] if rev else [zq, qd], axis=0))
            att = _dot_nt(jnp.concatenate(qqs, axis=1).astype(BF16), jnp.concatenate(kts, axis=1).astype(BF16))
            o = o + _dot(att.astype(BF16), v.astype(BF16))
            o_ref[rows, cols] = o
            kk = k * jnp.exp(b_last - b)
            st_ref[hh] = st * jnp.exp(b_last) + _dot_tn(v.astype(BF16), kk.astype(BF16))
        return carry

    lax.fori_loop(0, nchunk * ngrp, chunk_body, 0)


def _hgrn(p, lb, rev, nctx):
    bsz, l, _ = p.shape
    db = lb.shape[-1]
    nblk = l // TM
    blk = functools.partial(_scan_block, rev=rev, nctx=nctx, nblk=nblk)
    fcol = 4 if rev else 3
    return pl.pallas_call(
        functools.partial(_hgrn_kernel, rev=rev),
        out_shape=jax.ShapeDtypeStruct((bsz, l, db), F32),
        grid=(bsz, nblk),
        in_specs=[
            pl.BlockSpec((None, TM, db), lambda b, p_: (b, blk(p_), 2)),
            pl.BlockSpec((None, TM, db), lambda b, p_: (b, blk(p_), fcol)),
            pl.BlockSpec((None, TM, db), lambda b, p_: (b, blk(p_), 5)),
            _const_spec((1, db)),
        ],
        out_specs=pl.BlockSpec((None, TM, db), lambda b, p_: (b, blk(p_), 0)),
        scratch_shapes=[pltpu.VMEM((H_B, LANE, LANE), F32)],
        compiler_params=_cp(("parallel", "arbitrary")),
        name="hgrn_bwd" if rev else "hgrn_fwd",
    )(p, p, p, lb)


def _s5_kernel(u_ref, bb_ref, lam_ref, cc_ref, o_ref, bu_scr, st_ref, *, rev, tb, bsz, nslab):
    half = S5_SLAB_G * S5_STATE

    @pl.when(pl.program_id(0) == 0)
    def _():
        st_ref[...] = jnp.zeros_like(st_ref)

    for j in range(nslab):
        uj = jnp.swapaxes(u_ref[:, :, j * LANE:(j + 1) * LANE], 0, 1).reshape(tb * bsz, LANE).astype(BF16)
        bu_scr[...] = _dot(uj, bb_ref[j])
        lre = jnp.broadcast_to(lam_ref[j, 0:1, :], (bsz, half))
        lim = jnp.broadcast_to(lam_ref[j, 1:2, :], (bsz, half))

        def step(i, carry):
            xre, xim = carry
            t = (tb - 1 - i) if rev else i
            rows = pl.ds(pl.multiple_of(t * bsz, bsz), bsz)
            nre = lre * xre - lim * xim + bu_scr[rows, 0:half]
            nim = lre * xim + lim * xre + bu_scr[rows, half:2 * half]
            bu_scr[rows, 0:half] = nre
            bu_scr[rows, half:2 * half] = nim
            return nre, nim

        xre, xim = lax.fori_loop(0, tb, step, (st_ref[j, :, 0:half], st_ref[j, :, half:2 * half]), unroll=4)
        st_ref[j, :, 0:half] = xre
        st_ref[j, :, half:2 * half] = xim
        y = _dot(bu_scr[...].astype(BF16), cc_ref[j])
        o_ref[:, :, j * LANE:(j + 1) * LANE] = jnp.swapaxes(y.reshape(tb, bsz, LANE), 0, 1)


def _s5(p, ucol, bbig, lam, cbig, rev, ctx_len):
    bsz, l, _ = p.shape
    nslab = bbig.shape[0]
    dc = nslab * LANE
    tb = 128
    nblk, nctx = l // tb, ctx_len // tb
    blk = functools.partial(_scan_block, rev=rev, nctx=nctx, nblk=nblk)
    half = S5_SLAB_G * S5_STATE
    return pl.pallas_call(
        functools.partial(_s5_kernel, rev=rev, tb=tb, bsz=bsz, nslab=nslab),
        out_shape=jax.ShapeDtypeStruct((bsz, l, dc), F32),
        grid=(nblk,),
        in_specs=[
            pl.BlockSpec((bsz, tb, ucol[1]), lambda p_: (0, blk(p_), ucol[0])),
            _const_spec((nslab, LANE, 2 * half)),
            _const_spec((nslab, 2, half)),
            _const_spec((nslab, 2 * half, LANE)),
        ],
        out_specs=pl.BlockSpec((bsz, tb, dc), lambda p_: (0, blk(p_), 0)),
        scratch_shapes=[pltpu.VMEM((tb * bsz, 2 * half), F32), pltpu.VMEM((nslab, bsz, 2 * half), F32)],
        compiler_params=_cp(("arbitrary",)),
        name="s5_bwd" if rev else "s5_fwd",
    )(p, bbig, lam, cbig)


def _s5_params(lam_re, lam_im, log_step, b_re, b_im, c_re, c_im):
    g_c, p_s = lam_re.shape
    nslab = g_c // S5_SLAB_G
    lam = lax.complex(lam_re.astype(F32), lam_im.astype(F32))
    lam_dt = lam * jnp.exp(log_step.astype(F32))[:, None]
    lam_bar = jnp.exp(lam_dt)
    b_bar = ((lam_bar - 1.0) / lam)[:, :, None] * lax.complex(b_re.astype(F32), b_im.astype(F32))
    eye = jnp.eye(S5_SLAB_G, dtype=F32)

    def in_map(m):
        m = m.reshape(nslab, S5_SLAB_G, p_s, S5_IN)
        return jnp.einsum('jgpd,gh->jgdhp', m, eye).reshape(nslab, S5_SLAB_G * S5_IN, S5_SLAB_G * p_s)

    def out_map(m):
        m = m.reshape(nslab, S5_SLAB_G, S5_IN, p_s)
        return jnp.einsum('jgdp,gh->jgphd', m, eye).reshape(nslab, S5_SLAB_G * p_s, S5_SLAB_G * S5_IN)

    bbig = jnp.concatenate([in_map(jnp.real(b_bar)), in_map(jnp.imag(b_bar))], axis=-1).astype(BF16)
    cbig = jnp.concatenate([out_map(c_re.astype(F32)), out_map(-c_im.astype(F32))], axis=1).astype(BF16)
    lam_s = jnp.stack([jnp.real(lam_bar).reshape(nslab, -1), jnp.imag(lam_bar).reshape(nslab, -1)], axis=1)
    return bbig, lam_s, cbig


def _merge_a_kernel(ya_ref, of_ref, ob_ref, bg_ref, yf_ref, yb_ref, cu_ref, ga_ref, gb_ref, gc_ref,
                    hng_ref, dsk_ref, wglu_ref, wa_ref, wb_ref, wc_ref, o_ref):
    dc = dsk_ref.shape[-1]
    o = of_ref[...] + ob_ref[...]
    parts = []
    for hh in range(H_B):
        oh = o[:, hh * LANE:(hh + 1) * LANE]
        parts.append(oh * lax.rsqrt(jnp.mean(oh * oh, axis=-1, keepdims=True) + EPS))
    bg = bg_ref[...]
    yb = jnp.concatenate(parts, axis=-1) * hng_ref[...] * (bg * _sigmoid(bg))
    y = yf_ref[...] + yb_ref[...] + dsk_ref[...] * cu_ref[:, 0:dc]
    z = _dot(_gelu(y).astype(BF16), wglu_ref[...])
    yc = z[:, 0:dc] * _sigmoid(z[:, dc:2 * dc])
    merged = (_sigmoid(ga_ref[...]) * _dot(ya_ref[...].astype(BF16), wa_ref[...])
              + _sigmoid(gb_ref[...]) * _dot(yb.astype(BF16), wb_ref[...])
              + _sigmoid(gc_ref[...]) * _dot(yc.astype(BF16), wc_ref[...]))
    o_ref[...] = merged.astype(BF16)


def _merge_a(ya, o_f, o_b, p, y_f, y_b, hn_g, d_skip, w_glu, wa, wb, wc, off):
    bsz, l, da = ya.shape
    dc = y_f.shape[-1]
    d = wa.shape[-1]
    tm = 256
    o2 = off * (TM // tm)
    nblk = l // tm - o2
    tok = lambda b, j: (b, j + o2, 0)
    pcol = lambda c: (lambda b, j: (b, j + o2, c))
    return pl.pallas_call(
        _merge_a_kernel,
        out_shape=jax.ShapeDtypeStruct((bsz, nblk * tm, d), BF16),
        grid=(bsz, nblk),
        in_specs=[
            pl.BlockSpec((None, tm, da), tok),
            pl.BlockSpec((None, tm, da), tok),
            pl.BlockSpec((None, tm, da), tok),
            pl.BlockSpec((None, tm, da), pcol(6)),
            pl.BlockSpec((None, tm, dc), tok),
            pl.BlockSpec((None, tm, dc), tok),
            pl.BlockSpec((None, tm, da), pcol(7)),
            pl.BlockSpec((None, tm, d), pcol(4)),
            pl.BlockSpec((None, tm, d), pcol(5)),
            pl.BlockSpec((None, tm, d), pcol(6)),
            _const_spec((1, da)),
            _const_spec((1, dc)),
            _const_spec(w_glu.shape),
            _const_spec(wa.shape),
            _const_spec(wb.shape),
            _const_spec(wc.shape),
        ],
        out_specs=pl.BlockSpec((None, tm, d), lambda b, j: (b, j, 0)),
        compiler_params=_cp(("parallel", "parallel")),
        name="merge_a",
    )(ya, o_f, o_b, p, y_f, y_b, p, p, p, p, hn_g, d_skip, w_glu, wa, wb, wc)


def _merge_b_kernel(m_ref, x_ref, mod_ref, wo_ref, wq_ref, x1_ref, h2_ref, q_ref):
    x1 = x_ref[...] + mod_ref[2:3, :] * _dot(m_ref[...], wo_ref[...])
    x1_ref[...] = x1
    r = lax.rsqrt(jnp.mean(x1 * x1, axis=-1, keepdims=True) + EPS)
    h2 = x1 * r * (1.0 + mod_ref[4:5, :]) + mod_ref[3:4, :]
    h2_ref[...] = h2
    q_ref[...] = _dot(h2.astype(BF16), wq_ref[...])


def _merge_b(merged, xall, mod, wo, wq, off, nctx):
    bsz, lo, d = merged.shape
    nblk = lo // TM
    dq = wq.shape[-1]
    out = lambda b, j: (b, j, 0)
    return pl.pallas_call(
        _merge_b_kernel,
        out_shape=(jax.ShapeDtypeStruct((bsz, lo, d), F32), jax.ShapeDtypeStruct((bsz, lo, d), F32),
                   jax.ShapeDtypeStruct((bsz, lo, dq), F32)),
        grid=(bsz, nblk),
        in_specs=[
            pl.BlockSpec((None, TM, d), out),
            pl.BlockSpec((None, TM, d), lambda b, j: (b, j + off, 0)),
            pl.BlockSpec((None, None, N_MOD, d), lambda b, j: (b, (j + off >= nctx).astype(jnp.int32), 0, 0)),
            _const_spec(wo.shape),
            _const_spec(wq.shape),
        ],
        out_specs=(pl.BlockSpec((None, TM, d), out), pl.BlockSpec((None, TM, d), out),
                   pl.BlockSpec((None, TM, dq), out)),
        compiler_params=_cp(("parallel", "parallel")),
        name="merge_b",
    )(merged, xall, mod, wo, wq)


def _topk_rows(s, row_id, k):
    slot = lax.broadcasted_iota(jnp.int32, (k, s.shape[1]), 0)
    vals = jnp.zeros((k, s.shape[1]), F32)
    ids = jnp.zeros((k, s.shape[1]), F32)
    for r in range(k):
        m = jnp.max(s, axis=0, keepdims=True)
        am = jnp.min(jnp.where(s == m, row_id, 1e9), axis=0, keepdims=True)
        vals = jnp.where(slot == r, m, vals)
        ids = jnp.where(slot == r, am, ids)
        s = jnp.where(row_id == am, -jnp.inf, s)
    return vals, ids


def _pair_candidates():
    k = PEER_TOPK
    pieces = [("a", 0, 0, 16), ("a", 1, 0, 8), ("a", 2, 0, 8), ("a", 3, 0, 8), ("a", 4, 0, 8),
              ("b", 0, 0, 8), ("b", 1, 0, 8), ("b", 0, 8, 16)]
    ids, bias, seen = [], [], set()
    for kind, fixed, lo, hi in pieces:
        for r in range(lo, hi):
            a, b = (fixed, r) if kind == "a" else (r, fixed)
            ok = (a + 1) * (b + 1) <= k and (a, b) not in seen
            if ok:
                seen.add((a, b))
            ids.append(float(a * k + b))
            bias.append(0.0 if ok else -math.inf)
    assert len(seen) == sum(k // (a + 1) for a in range(k))
    return pieces, ids, bias


def _route_kernel(q_ref, keys_ref, cid_ref, cbias_ref, idx_ref, g_ref, it_scr, gt_scr):
    k = PEER_TOPK
    pieces, _, _ = _pair_candidates()
    key_id = lax.broadcasted_iota(jnp.int32, (PEER_NKEYS, q_ref.shape[0]), 0).astype(F32)
    for h in range(PEER_HEADS):
        sv, si = [], []
        for half in range(2):
            c0 = (h * 2 + half) * PEER_NKEYS
            qh = q_ref[:, c0:c0 + PEER_NKEYS].astype(BF16)
            s = _dot_nt(keys_ref[h, half], qh)
            v_, i_ = _topk_rows(s, key_id, k)
            sv.append(v_)
            si.append(i_)
        cand = jnp.concatenate(
            [sv[0][f:f + 1, :] + sv[1][lo:hi, :] if kind == "a" else sv[0][lo:hi, :] + sv[1][f:f + 1, :]
             for kind, f, lo, hi in pieces], axis=0) + cbias_ref[...]
        fv, fi = _topk_rows(cand, cid_ref[...], k)
        fa = jnp.floor(fi * (1.0 / k))
        fb = fi - fa * k
        i1 = jnp.zeros_like(fi)
        i2 = jnp.zeros_like(fi)
        for a in range(k):
            i1 = jnp.where(fa == a, si[0][a:a + 1, :], i1)
            i2 = jnp.where(fb == a, si[1][a:a + 1, :], i2)
        e = jnp.exp(fv - fv[0:1, :])
        gt_scr[h * k:(h + 1) * k, :] = e / jnp.sum(e, axis=0, keepdims=True)
        it_scr[h * k:(h + 1) * k, :] = i1 * PEER_NKEYS + i2
    g_ref[...] = gt_scr[...].T
    idx_ref[...] = it_scr[...].T.astype(jnp.int32)


def _route(q, keys):
    t, dq = q.shape
    tm = LANE
    nsel = PEER_HEADS * PEER_TOPK
    _, ids, bias = _pair_candidates()
    cid = jnp.broadcast_to(jnp.asarray(ids, F32)[:, None], (len(ids), tm))
    cbias = jnp.broadcast_to(jnp.asarray(bias, F32)[:, None], (len(bias), tm))
    return pl.pallas_call(
        _route_kernel,
        out_shape=(jax.ShapeDtypeStruct((t, nsel), jnp.int32), jax.ShapeDtypeStruct((t, nsel), F32)),
        grid=(t // tm,),
        in_specs=[pl.BlockSpec((tm, dq), lambda i: (i, 0)), _const_spec(keys.shape),
                  _const_spec(cid.shape), _const_spec(cbias.shape)],
        out_specs=(pl.BlockSpec((tm, nsel), lambda i: (i, 0)), pl.BlockSpec((tm, nsel), lambda i: (i, 0))),
        scratch_shapes=[pltpu.VMEM((nsel, tm), F32), pltpu.VMEM((nsel, tm), F32)],
        compiler_params=_cp(("parallel",)),
        name="peer_route",
    )(q, keys, cid, cbias)


PEER_TB = 64
PEER_SLOTS = 4
PEER_AHEAD = PEER_SLOTS - 1
HALF_SUB = 8
HI_MASK = -65536


def _pack_expert_table(u_all, v_all, layer):
    _, n_exp, d = u_all.shape
    tb = 256
    assert d == 2 * HALF_SUB * LANE and n_exp % tb == 0

    def pack(x):
        bits = lax.bitcast_convert_type(x.astype(BF16).astype(F32), jnp.int32)
        word = (bits[:, :d // 2] & HI_MASK) | lax.shift_right_logical(bits[:, d // 2:], 16)
        return word.reshape(x.shape[0], HALF_SUB, LANE)

    def body(u_ref, v_ref, o_ref):
        o_ref[:, 0:HALF_SUB, :] = pack(u_ref[...])
        o_ref[:, HALF_SUB:2 * HALF_SUB, :] = pack(v_ref[...])

    return pl.pallas_call(
        body,
        out_shape=jax.ShapeDtypeStruct((n_exp, 2 * HALF_SUB, LANE), jnp.int32),
        grid=(n_exp // tb,),
        in_specs=[pl.BlockSpec((None, tb, d), lambda i: (layer, i, 0)),
                  pl.BlockSpec((None, tb, d), lambda i: (layer, i, 0))],
        out_specs=pl.BlockSpec((tb, 2 * HALF_SUB, LANE), lambda i: (i, 0, 0)),
        compiler_params=_cp(("parallel",)),
        name="pack_experts",
    )(u_all, v_all)


def _unpack_pair(word):
    return lax.bitcast_convert_type(word & HI_MASK, F32), lax.bitcast_convert_type(word << 16, F32)


def _experts_kernel(idx_ref, g_ref, h_ref, x_ref, mod_ref, fin_ref, tab_ref, o_ref, *scratch, final_norm):
    bufs, sem = scratch[:PEER_SLOTS], scratch[PEER_SLOTS]
    nsel = PEER_HEADS * PEER_TOPK
    per = nsel // (2 * HALF_SUB)
    step = pl.program_id(0)

    def issue(t, slot, j0=0, j1=nsel):
        for j in range(j0, j1):
            e = idx_ref[0, t * nsel + j]
            pltpu.make_async_copy(tab_ref.at[e], bufs[slot].at[:, j, :], sem.at[slot]).start(priority=j % 2)

    def wait(slot):
        pltpu.make_async_copy(bufs[slot], bufs[slot], sem.at[slot]).wait()

    @pl.when(step == 0)
    def _():
        for s in range(PEER_AHEAD):
            issue(s, s)

    eye = lax.broadcasted_iota(jnp.int32, (nsel, nsel), 0) == lax.broadcasted_iota(jnp.int32, (nsel, nsel), 1)

    def group(t4, c):
        for s in range(PEER_SLOTS):
            t = t4 * PEER_SLOTS + s
            wait(s)
            nxt = (s + PEER_AHEAD) % PEER_SLOTS
            row = pl.ds(t, 1)
            hrow = h_ref[row, :]
            buf = bufs[s]
            acc = None
            for cc in range(HALF_SUB):
                u_hi, u_lo = _unpack_pair(buf[cc])
                term = (u_hi * hrow[:, cc * LANE:(cc + 1) * LANE]
                        + u_lo * hrow[:, (HALF_SUB + cc) * LANE:(HALF_SUB + cc + 1) * LANE])
                acc = term if acc is None else acc + term
                issue(t + PEER_AHEAD, nxt, cc * per, (cc + 1) * per)
            act = jnp.sum(acc, axis=-1, keepdims=True)
            gcol = jnp.sum(jnp.where(eye, g_ref[row, :], 0.0), axis=-1, keepdims=True)
            w = _gelu(act) * gcol
            outs = [None] * (2 * HALF_SUB)
            for cc in range(HALF_SUB):
                v_hi, v_lo = _unpack_pair(buf[HALF_SUB + cc])
                outs[cc] = jnp.sum(w * v_hi, axis=0, keepdims=True)
                outs[HALF_SUB + cc] = jnp.sum(w * v_lo, axis=0, keepdims=True)
                issue(t + PEER_AHEAD, nxt, (HALF_SUB + cc) * per, (HALF_SUB + cc + 1) * per)
            o_ref[row, :] = x_ref[row, :] + mod_ref[5:6, :] * jnp.concatenate(outs, axis=-1)
        return c

    lax.fori_loop(0, PEER_TB // PEER_SLOTS, group, 0)

    if final_norm:
        y = o_ref[...]
        o_ref[...] = y * lax.rsqrt(jnp.mean(y * y, axis=-1, keepdims=True) + EPS) * fin_ref[...]

    @pl.when(step == pl.num_programs(0) - 1)
    def _():
        for s in range(PEER_AHEAD):
            wait(s)


def _experts(idx, g, h2, x1, mod, fin_g, tab, off, nctx, final_norm):
    t_all, d = h2.shape
    bsz = mod.shape[0]
    nsel = idx.shape[-1]
    tb = PEER_TB
    nb = t_all // tb
    per_b = nb // bsz
    per = TM // tb
    assert d == 2 * HALF_SUB * LANE and tb % PEER_SLOTS == 0
    main = idx.reshape(nb, tb * nsel)
    idx_ext = jnp.concatenate([main, jnp.roll(main, -1, axis=0)[:, :PEER_AHEAD * nsel]], axis=1)
    idx_ext = idx_ext.reshape(nb, 1, (tb + PEER_AHEAD) * nsel)
    tokb = lambda i: (i, 0)

    def mod_map(i):
        return (i // per_b, ((i % per_b) // per + off >= nctx).astype(jnp.int32), 0, 0)

    return pl.pallas_call(
        functools.partial(_experts_kernel, final_norm=final_norm),
        out_shape=jax.ShapeDtypeStruct((t_all, d), F32),
        grid=(nb,),
        in_specs=[
            pl.BlockSpec((None, 1, (tb + PEER_AHEAD) * nsel), lambda i: (i, 0, 0), memory_space=pltpu.SMEM),
            pl.BlockSpec((tb, nsel), tokb),
            pl.BlockSpec((tb, d), tokb),
            pl.BlockSpec((tb, d), tokb),
            pl.BlockSpec((None, None, N_MOD, d), mod_map),
            _const_spec((1, d)),
            pl.BlockSpec(memory_space=pl.ANY),
        ],
        out_specs=pl.BlockSpec((tb, d), tokb),
        scratch_shapes=[pltpu.VMEM((2 * HALF_SUB, nsel, LANE), jnp.int32) for _ in range(PEER_SLOTS)]
        + [pltpu.SemaphoreType.DMA((PEER_SLOTS,))],
        compiler_params=_cp(("arbitrary",)),
        name="peer_experts",
    )(idx_ext, g, h2, x1, mod, fin_g, tab)


def _pos_embed_2d(n, d):
    n_rows = n // GRID_W
    quarter = d // 4
    omega = 1.0 / (POS_THETA ** (jnp.arange(quarter, dtype=F32) / quarter))

    def axis_code(length):
        ang = jnp.arange(length, dtype=F32)[:, None] * omega[None]
        return jnp.concatenate([jnp.sin(ang), jnp.cos(ang)], axis=-1)

    er = jnp.broadcast_to(axis_code(n_rows)[:, None], (n_rows, GRID_W, d // 2))
    ec = jnp.broadcast_to(axis_code(GRID_W)[None], (n_rows, GRID_W, d // 2))
    return jnp.concatenate([er, ec], axis=-1).reshape(n, d)


def kernel(x, c, ctx, c_ctx, w_mod, b_mod, w_in, gmlp_ln_g, gmlp_ln_b, gmlp_w_s, gmlp_b_s, hgrn_lb_logits, hgrn_norm_g, s5_lam_re, s5_lam_im, s5_log_step, s5_b_re, s5_b_im, s5_c_re, s5_c_im, s5_d, s5_w_glu, w_branch_a, w_branch_b, w_branch_c, w_out, peer_w_q, peer_sub_keys, peer_u, peer_v, final_norm_g):
    bsz, n, d = x.shape
    ctx_len = ctx.shape[1]
    depth = w_mod.shape[0]
    nctx = ctx_len // TM
    da = gmlp_ln_g.shape[-1]
    db = hgrn_norm_g.shape[-1]
    dc = s5_d.shape[-1]
    assert n % TM == 0 and ctx_len % TM == 0 and bsz % SUB == 0
    assert da == db and d == 2 * da and dc % LANE == 0 and dc <= da
    assert w_in.shape[-1] == 2 * da + 5 * db + dc + 3 * d

    xall = _embed(x, ctx, _pos_embed_2d(n, d).astype(x.dtype))

    rows = -(-(bsz + 1) // 8) * 8
    cc = jnp.zeros((rows, d), F32).at[0].set(c_ctx).at[1:bsz + 1].set(c)
    mod_all = _modulation(cc, w_mod, b_mod)

    gamma = jax.nn.softmax(hgrn_lb_logits.astype(F32), axis=0)
    lower_bounds = jnp.cumsum(gamma, axis=0) - gamma[0:1]

    for l in range(depth):
        last = l == depth - 1
        off = nctx if last else 0
        mod_c = jnp.broadcast_to(mod_all[l, 0][None], (bsz, N_MOD * d))
        mod = jnp.stack([mod_c, mod_all[l, 1:bsz + 1]], axis=1).reshape(bsz, 2, N_MOD, d)

        cu0 = 2 * da + 5 * db
        w_pad = jnp.concatenate([w_in[l][:, :cu0 + dc], jnp.zeros((d, da - dc), F32), w_in[l][:, cu0 + dc:]],
                                axis=1).astype(BF16)
        p = _in_proj(xall, mod, w_pad, ctx_len)

        bias = jnp.repeat(gmlp_b_s[l].T, da // G_A, axis=1)
        ya = _gmlp(p, gmlp_ln_g[l].reshape(1, da), gmlp_ln_b[l].reshape(1, da), gmlp_w_s[l].astype(BF16), bias)

        o_f = _hgrn(p, lower_bounds[l, 0].reshape(1, db), False, nctx)
        o_b = _hgrn(p, lower_bounds[l, 1].reshape(1, db), True, nctx)

        ys = []
        for r in range(2):
            prm = _s5_params(s5_lam_re[l, r], s5_lam_im[l, r], s5_log_step[l, r], s5_b_re[l, r], s5_b_im[l, r],
                             s5_c_re[l, r], s5_c_im[l, r])
            ys.append(_s5(p, (cu0 // da, da), *prm, rev=bool(r), ctx_len=ctx_len))

        merged = _merge_a(ya, o_f, o_b, p, ys[0], ys[1], hgrn_norm_g[l].reshape(1, db), s5_d[l].reshape(1, dc),
                          s5_w_glu[l].astype(BF16), w_branch_a[l].astype(BF16), w_branch_b[l].astype(BF16),
                          w_branch_c[l].astype(BF16), off)
        x1, h2, q = _merge_b(merged, xall, mod, w_out[l].astype(BF16), peer_w_q[l].astype(BF16), off, nctx)

        lo = x1.shape[1]
        idx, g = _route(q.reshape(bsz * lo, -1), peer_sub_keys[l].astype(BF16))
        tab = _pack_expert_table(peer_u, peer_v, l)
        xall = _experts(idx, g, h2.reshape(bsz * lo, d), x1.reshape(bsz * lo, d), mod,
                        final_norm_g.astype(F32).reshape(1, d), tab, off, nctx, final_norm=last)
        xall = xall.reshape(bsz, lo, d)

    return xall
```

```python
import functools
import math

import jax
import jax.numpy as jnp
from jax import lax
from jax.experimental import pallas as pl
from jax.experimental.pallas import tpu as pltpu

F32 = jnp.float32
BF16 = jnp.bfloat16

EPS = 1e-6
GRID_W = 64
POS_THETA = 10000.0
N_MOD = 6
G_A = 8
GMLP_CHUNK = 128
H_B = 8
HGRN_CHUNK = 32
HGRN_HEADS_PER_BODY = 8
S5_IN = 16
S5_STATE = 64
S5_SLAB_G = 8
PEER_HEADS = 8
PEER_NKEYS = 128
PEER_TOPK = 16

TM = 256
LANE = 128
SUB = 8
VMEM_LIMIT = 56 * 1024 * 1024


def _cp(sem, vmem=VMEM_LIMIT):
    return pltpu.CompilerParams(dimension_semantics=sem, vmem_limit_bytes=vmem)


def _const_spec(shape):
    nd = len(shape)
    return pl.BlockSpec(shape, lambda *_: (0,) * nd, pipeline_mode=pl.Buffered(1))


def _gelu(x):
    return x * (lax.erf(x * (1.0 / math.sqrt(2.0))) + 1.0) * 0.5


def _sigmoid(x):
    return jax.nn.sigmoid(x)


def _dot(a, b):
    return jnp.dot(a, b, preferred_element_type=F32)


def _dot_nt(a, b):
    return lax.dot_general(a, b, (((1,), (1,)), ((), ())), preferred_element_type=F32)


def _dot_tn(a, b):
    return lax.dot_general(a, b, (((0,), (0,)), ((), ())), preferred_element_type=F32)


def _embed_kernel(ctx_ref, x_ref, pos_ref, o_ref, *, nctx):
    j = pl.program_id(1)

    @pl.when(j < nctx)
    def _():
        o_ref[...] = ctx_ref[...]

    @pl.when(j >= nctx)
    def _():
        o_ref[...] = x_ref[...] + pos_ref[...]


def _embed(x, ctx, pos):
    bsz, n, d = x.shape
    nctx = ctx.shape[1] // TM
    nblk = nctx + n // TM
    return pl.pallas_call(
        functools.partial(_embed_kernel, nctx=nctx),
        out_shape=jax.ShapeDtypeStruct((bsz, nblk * TM, d), x.dtype),
        grid=(bsz, nblk),
        in_specs=[
            pl.BlockSpec((None, TM, d), lambda b, j: (b, jnp.minimum(j, nctx - 1), 0)),
            pl.BlockSpec((None, TM, d), lambda b, j: (b, jnp.maximum(j - nctx, 0), 0)),
            pl.BlockSpec((TM, d), lambda b, j: (jnp.maximum(j - nctx, 0), 0)),
        ],
        out_specs=pl.BlockSpec((None, TM, d), lambda b, j: (b, j, 0)),
        compiler_params=_cp(("parallel", "arbitrary")),
        name="embed",
    )(ctx, x, pos)


def _mod_kernel(c_ref, w_ref, b_ref, o_ref):
    c = c_ref[...]
    s = c * _sigmoid(c)
    o_ref[...] = _dot(s.astype(BF16), w_ref[...].astype(BF16)) + b_ref[...]


def _modulation(cc, w_mod, b_mod):
    depth, d, nd = w_mod.shape
    rows = cc.shape[0]
    tn = 1024
    return pl.pallas_call(
        _mod_kernel,
        out_shape=jax.ShapeDtypeStruct((depth, rows, nd), F32),
        grid=(depth, nd // tn),
        in_specs=[
            pl.BlockSpec((rows, d), lambda l, n: (0, 0)),
            pl.BlockSpec((None, d, tn), lambda l, n: (l, 0, n)),
            pl.BlockSpec((None, 1, tn), lambda l, n: (l, 0, n)),
        ],
        out_specs=pl.BlockSpec((None, rows, tn), lambda l, n: (l, 0, n)),
        compiler_params=_cp(("parallel", "arbitrary")),
        name="modulation",
    )(cc, w_mod, b_mod.reshape(depth, 1, nd))


def _inproj_kernel(x_ref, mod_ref, w_ref, o_ref, h_scr, *, tm, ctx_rows):
    j = pl.program_id(1)

    @pl.when(pl.program_id(2) == 0)
    def _():
        x = x_ref[...]
        r = lax.rsqrt(jnp.mean(x * x, axis=-1, keepdims=True) + EPS)
        row = j * tm + lax.broadcasted_iota(jnp.int32, (tm, 1), 0)
        is_ctx = row < ctx_rows
        sh = jnp.where(is_ctx, mod_ref[0, 0:1, :], mod_ref[1, 0:1, :])
        sc = jnp.where(is_ctx, mod_ref[0, 1:2, :], mod_ref[1, 1:2, :])
        h_scr[...] = (x * r * (1.0 + sc) + sh).astype(BF16)

    o_ref[...] = _dot(h_scr[...], w_ref[...])


def _in_proj(xall, mod, w_pad, ctx_rows):
    bsz, l, d = xall.shape
    ncol = w_pad.shape[1]
    tm = 768 if l % 768 == 0 else TM
    tn = 1024
    return pl.pallas_call(
        functools.partial(_inproj_kernel, tm=tm, ctx_rows=ctx_rows),
        out_shape=jax.ShapeDtypeStruct((bsz, l, ncol), F32),
        grid=(bsz, l // tm, ncol // tn),
        in_specs=[
            pl.BlockSpec((None, tm, d), lambda b, j, n: (b, j, 0)),
            pl.BlockSpec((None, 2, N_MOD, d), lambda b, j, n: (b, 0, 0, 0)),
            pl.BlockSpec((d, tn), lambda b, j, n: (0, n)),
        ],
        out_specs=pl.BlockSpec((None, tm, tn), lambda b, j, n: (b, j, n)),
        scratch_shapes=[pltpu.VMEM((tm, d), BF16)],
        compiler_params=_cp(("parallel", "parallel", "arbitrary")),
        name="in_proj",
    )(xall, mod, w_pad)


def _gmlp_kernel(u_ref, v_ref, lng_ref, lnb_ref, ws_ref, bias_ref, o_ref):
    v = _gelu(v_ref[...])
    mu = jnp.mean(v, axis=-1, keepdims=True)
    vc = v - mu
    var = jnp.mean(vc * vc, axis=-1, keepdims=True)
    v = (vc * lax.rsqrt(var + EPS) * lng_ref[...] + lnb_ref[...]).astype(BF16)
    for c in range(TM // GMLP_CHUNK):
        rows = slice(c * GMLP_CHUNK, (c + 1) * GMLP_CHUNK)
        for g in range(G_A):
            cols = slice(g * LANE, (g + 1) * LANE)
            mixed = _dot(ws_ref[g], v[rows, cols]) + bias_ref[:, cols]
            o_ref[rows, cols] = _gelu(u_ref[rows, cols]) * mixed


def _gmlp(p, ln_g, ln_b, w_s, bias):
    bsz, l, _ = p.shape
    da = ln_g.shape[-1]
    return pl.pallas_call(
        _gmlp_kernel,
        out_shape=jax.ShapeDtypeStruct((bsz, l, da), F32),
        grid=(bsz, l // TM),
        in_specs=[
            pl.BlockSpec((None, TM, da), lambda b, j: (b, j, 0)),
            pl.BlockSpec((None, TM, da), lambda b, j: (b, j, 1)),
            _const_spec((1, da)),
            _const_spec((1, da)),
            _const_spec((G_A, GMLP_CHUNK, GMLP_CHUNK)),
            _const_spec((GMLP_CHUNK, da)),
        ],
        out_specs=pl.BlockSpec((None, TM, da), lambda b, j: (b, j, 0)),
        compiler_params=_cp(("parallel", "parallel")),
        name="gmlp",
    )(p, p, ln_g, ln_b, w_s, bias)


def _scan_block(p, rev, nctx, nblk):
    if not rev:
        return p
    return jnp.where(p < nctx, nctx - 1 - p, nblk - 1 - (p - nctx))


def _hgrn_kernel(q_ref, f_ref, i_ref, lb_ref, o_ref, st_ref, *, rev):
    c_sz = HGRN_CHUNK
    nchunk = TM // c_sz

    @pl.when(pl.program_id(1) == 0)
    def _():
        st_ref[...] = jnp.zeros_like(st_ref)

    nsb = c_sz // SUB
    rowi = lax.broadcasted_iota(jnp.int32, (c_sz, LANE), 0)
    row8 = lax.broadcasted_iota(jnp.int32, (SUB, LANE), 0)

    hpb = HGRN_HEADS_PER_BODY
    ngrp = H_B // hpb

    def chunk_body(it, carry):
        ci, hg = it // ngrp, it % ngrp
        c = (nchunk - 1 - ci) if rev else ci
        rows = pl.ds(pl.multiple_of(c * c_sz, c_sz), c_sz)
        for hl in range(hpb):
            if ngrp == 1:
                hh, cols = hl, slice(hl * LANE, (hl + 1) * LANE)
            else:
                hh = hg * hpb + hl
                cols = pl.ds(pl.multiple_of(hh * LANE, LANE), LANE)
            lbh = lb_ref[:, cols]
            st = st_ref[hh]
            qr = q_ref[rows, cols]
            q = qr * _sigmoid(qr)
            fg = lbh + (1.0 - lbh) * _sigmoid(f_ref[rows, cols])
            k = 1.0 - fg
            v = i_ref[rows, cols]
            b = jnp.log(fg)
            for s in (1, 2, 4, 8, 16):
                if rev:
                    b = b + jnp.where(rowi < c_sz - s, pltpu.roll(b, c_sz - s, 0), 0.0)
                else:
                    b = b + jnp.where(rowi >= s, pltpu.roll(b, s, 0), 0.0)
            b_last = b[0:1, :] if rev else b[c_sz - 1:c_sz, :]
            o = _dot_nt((q * jnp.exp(b)).astype(BF16), st.astype(BF16))
            diag = []
            for jb in range(nsb):
                r8 = slice(jb * SUB, (jb + 1) * SUB)
                bj, qj, kj, vj = b[r8], q[r8], k[r8], v[r8]
                oj = jnp.zeros((SUB, LANE), F32)
                for s in range(SUB):
                    valid = (row8 <= s) if rev else (row8 >= s)
                    dec = jnp.exp(bj - bj[s:s + 1, :])
                    pr = jnp.where(valid, qj * (kj[s:s + 1, :] * dec), 0.0)
                    oj = oj + jnp.sum(pr, axis=-1, keepdims=True) * vj[s:s + 1, :]
                diag.append(oj)
            o = o + jnp.concatenate(diag, axis=0)
            qqs, kts = [], []
            for jb in (range(1, nsb) if rev else range(nsb - 1)):
                edge = jb * SUB if rev else jb * SUB + SUB - 1
                b_ref_row = b[edge:edge + 1, :]
                src = slice(jb * SUB, (jb + 1) * SUB)
                dst = slice(0, jb * SUB) if rev else slice((jb + 1) * SUB, c_sz)
                kt8 = k[src] * jnp.exp(jnp.minimum(b_ref_row - b[src], 0.0))
                qd = q[dst] * jnp.exp(jnp.minimum(b[dst] - b_ref_row, 0.0))
                zk_lo = jnp.zeros((jb * SUB, LANE), F32)
                zk_hi = jnp.zeros((c_sz - (jb + 1) * SUB, LANE), F32)
                kts.append(jnp.concatenate([z for z in (zk_lo, kt8, zk_hi) if z.shape[0]], axis=0))
                zq = jnp.zeros((c_sz - qd.shape[0], LANE), F32)
                qqs.append(jnp.concatenate([qd, zq] if rev else [zq, qd], axis=0))
            att = _dot_nt(jnp.concatenate(qqs, axis=1).astype(BF16), jnp.concatenate(kts, axis=1).astype(BF16))
            o = o + _dot(att.astype(BF16), v.astype(BF16))
            o_ref[rows, cols] = o
            kk = k * jnp.exp(b_last - b)
            st_ref[hh] = st * jnp.exp(b_last) + _dot_tn(v.astype(BF16), kk.astype(BF16))
        return carry

    lax.fori_loop(0, nchunk * ngrp, chunk_body, 0)


def _hgrn(p, lb, rev, nctx):
    bsz, l, _ = p.shape
    db = lb.shape[-1]
    nblk = l // TM
    blk = functools.partial(_scan_block, rev=rev, nctx=nctx, nblk=nblk)
    fcol = 4 if rev else 3
    return pl.pallas_call(
        functools.partial(_hgrn_kernel, rev=rev),
        out_shape=jax.ShapeDtypeStruct((bsz, l, db), F32),
        grid=(bsz, nblk),
        in_specs=[
            pl.BlockSpec((None, TM, db), lambda b, p_: (b, blk(p_), 2)),
            pl.BlockSpec((None, TM, db), lambda b, p_: (b, blk(p_), fcol)),
            pl.BlockSpec((None, TM, db), lambda b, p_: (b, blk(p_), 5)),
            _const_spec((1, db)),
        ],
        out_specs=pl.BlockSpec((None, TM, db), lambda b, p_: (b, blk(p_), 0)),
        scratch_shapes=[pltpu.VMEM((H_B, LANE, LANE), F32)],
        compiler_params=_cp(("parallel", "arbitrary")),
        name="hgrn_bwd" if rev else "hgrn_fwd",
    )(p, p, p, lb)


def _s5_kernel(u_ref, bb_ref, lam_ref, cc_ref, o_ref, bu_scr, st_ref, *, rev, tb, bsz, nslab):
    half = S5_SLAB_G * S5_STATE

    @pl.when(pl.program_id(0) == 0)
    def _():
        st_ref[...] = jnp.zeros_like(st_ref)

    for j in range(nslab):
        uj = jnp.swapaxes(u_ref[:, :, j * LANE:(j + 1) * LANE], 0, 1).reshape(tb * bsz, LANE).astype(BF16)
        bu_scr[...] = _dot(uj, bb_ref[j])
        lre = jnp.broadcast_to(lam_ref[j, 0:1, :], (bsz, half))
        lim = jnp.broadcast_to(lam_ref[j, 1:2, :], (bsz, half))

        def step(i, carry):
            xre, xim = carry
            t = (tb - 1 - i) if rev else i
            rows = pl.ds(pl.multiple_of(t * bsz, bsz), bsz)
            nre = lre * xre - lim * xim + bu_scr[rows, 0:half]
            nim = lre * xim + lim * xre + bu_scr[rows, half:2 * half]
            bu_scr[rows, 0:half] = nre
            bu_scr[rows, half:2 * half] = nim
            return nre, nim

        xre, xim = lax.fori_loop(0, tb, step, (st_ref[j, :, 0:half], st_ref[j, :, half:2 * half]), unroll=4)
        st_ref[j, :, 0:half] = xre
        st_ref[j, :, half:2 * half] = xim
        y = _dot(bu_scr[...].astype(BF16), cc_ref[j])
        o_ref[:, :, j * LANE:(j + 1) * LANE] = jnp.swapaxes(y.reshape(tb, bsz, LANE), 0, 1)


def _s5(p, ucol, bbig, lam, cbig, rev, ctx_len):
    bsz, l, _ = p.shape
    nslab = bbig.shape[0]
    dc = nslab * LANE
    tb = 128
    nblk, nctx = l // tb, ctx_len // tb
    blk = functools.partial(_scan_block, rev=rev, nctx=nctx, nblk=nblk)
    half = S5_SLAB_G * S5_STATE
    return pl.pallas_call(
        functools.partial(_s5_kernel, rev=rev, tb=tb, bsz=bsz, nslab=nslab),
        out_shape=jax.ShapeDtypeStruct((bsz, l, dc), F32),
        grid=(nblk,),
        in_specs=[
            pl.BlockSpec((bsz, tb, ucol[1]), lambda p_: (0, blk(p_), ucol[0])),
            _const_spec((nslab, LANE, 2 * half)),
            _const_spec((nslab, 2, half)),
            _const_spec((nslab, 2 * half, LANE)),
        ],
        out_specs=pl.BlockSpec((bsz, tb, dc), lambda p_: (0, blk(p_), 0)),
        scratch_shapes=[pltpu.VMEM((tb * bsz, 2 * half), F32), pltpu.VMEM((nslab, bsz, 2 * half), F32)],
        compiler_params=_cp(("arbitrary",)),
        name="s5_bwd" if rev else "s5_fwd",
    )(p, bbig, lam, cbig)


def _s5_params(lam_re, lam_im, log_step, b_re, b_im, c_re, c_im):
    g_c, p_s = lam_re.shape
    nslab = g_c // S5_SLAB_G
    lam = lax.complex(lam_re.astype(F32), lam_im.astype(F32))
    lam_dt = lam * jnp.exp(log_step.astype(F32))[:, None]
    lam_bar = jnp.exp(lam_dt)
    b_bar = ((lam_bar - 1.0) / lam)[:, :, None] * lax.complex(b_re.astype(F32), b_im.astype(F32))
    eye = jnp.eye(S5_SLAB_G, dtype=F32)

    def in_map(m):
        m = m.reshape(nslab, S5_SLAB_G, p_s, S5_IN)
        return jnp.einsum('jgpd,gh->jgdhp', m, eye).reshape(nslab, S5_SLAB_G * S5_IN, S5_SLAB_G * p_s)

    def out_map(m):
        m = m.reshape(nslab, S5_SLAB_G, S5_IN, p_s)
        return jnp.einsum('jgdp,gh->jgphd', m, eye).reshape(nslab, S5_SLAB_G * p_s, S5_SLAB_G * S5_IN)

    bbig = jnp.concatenate([in_map(jnp.real(b_bar)), in_map(jnp.imag(b_bar))], axis=-1).astype(BF16)
    cbig = jnp.concatenate([out_map(c_re.astype(F32)), out_map(-c_im.astype(F32))], axis=1).astype(BF16)
    lam_s = jnp.stack([jnp.real(lam_bar).reshape(nslab, -1), jnp.imag(lam_bar).reshape(nslab, -1)], axis=1)
    return bbig, lam_s, cbig


def _merge_a_kernel(ya_ref, of_ref, ob_ref, bg_ref, yf_ref, yb_ref, cu_ref, ga_ref, gb_ref, gc_ref,
                    hng_ref, dsk_ref, wglu_ref, wa_ref, wb_ref, wc_ref, o_ref):
    dc = dsk_ref.shape[-1]
    o = of_ref[...] + ob_ref[...]
    parts = []
    for hh in range(H_B):
        oh = o[:, hh * LANE:(hh + 1) * LANE]
        parts.append(oh * lax.rsqrt(jnp.mean(oh * oh, axis=-1, keepdims=True) + EPS))
    bg = bg_ref[...]
    yb = jnp.concatenate(parts, axis=-1) * hng_ref[...] * (bg * _sigmoid(bg))
    y = yf_ref[...] + yb_ref[...] + dsk_ref[...] * cu_ref[:, 0:dc]
    z = _dot(_gelu(y).astype(BF16), wglu_ref[...])
    yc = z[:, 0:dc] * _sigmoid(z[:, dc:2 * dc])
    merged = (_sigmoid(ga_ref[...]) * _dot(ya_ref[...].astype(BF16), wa_ref[...])
              + _sigmoid(gb_ref[...]) * _dot(yb.astype(BF16), wb_ref[...])
              + _sigmoid(gc_ref[...]) * _dot(yc.astype(BF16), wc_ref[...]))
    o_ref[...] = merged.astype(BF16)


def _merge_a(ya, o_f, o_b, p, y_f, y_b, hn_g, d_skip, w_glu, wa, wb, wc, off):
    bsz, l, da = ya.shape
    dc = y_f.shape[-1]
    d = wa.shape[-1]
    tm = 256
    o2 = off * (TM // tm)
    nblk = l // tm - o2
    tok = lambda b, j: (b, j + o2, 0)
    pcol = lambda c: (lambda b, j: (b, j + o2, c))
    return pl.pallas_call(
        _merge_a_kernel,
        out_shape=jax.ShapeDtypeStruct((bsz, nblk * tm, d), BF16),
        grid=(bsz, nblk),
        in_specs=[
            pl.BlockSpec((None, tm, da), tok),
            pl.BlockSpec((None, tm, da), tok),
            pl.BlockSpec((None, tm, da), tok),
            pl.BlockSpec((None, tm, da), pcol(6)),
            pl.BlockSpec((None, tm, dc), tok),
            pl.BlockSpec((None, tm, dc), tok),
            pl.BlockSpec((None, tm, da), pcol(7)),
            pl.BlockSpec((None, tm, d), pcol(4)),
            pl.BlockSpec((None, tm, d), pcol(5)),
            pl.BlockSpec((None, tm, d), pcol(6)),
            _const_spec((1, da)),
            _const_spec((1, dc)),
            _const_spec(w_glu.shape),
            _const_spec(wa.shape),
            _const_spec(wb.shape),
            _const_spec(wc.shape),
        ],
        out_specs=pl.BlockSpec((None, tm, d), lambda b, j: (b, j, 0)),
        compiler_params=_cp(("parallel", "parallel")),
        name="merge_a",
    )(ya, o_f, o_b, p, y_f, y_b, p, p, p, p, hn_g, d_skip, w_glu, wa, wb, wc)


def _merge_b_kernel(m_ref, x_ref, mod_ref, wo_ref, wq_ref, x1_ref, h2_ref, q_ref):
    x1 = x_ref[...] + mod_ref[2:3, :] * _dot(m_ref[...], wo_ref[...])
    x1_ref[...] = x1
    r = lax.rsqrt(jnp.mean(x1 * x1, axis=-1, keepdims=True) + EPS)
    h2 = x1 * r * (1.0 + mod_ref[4:5, :]) + mod_ref[3:4, :]
    h2_ref[...] = h2
    q_ref[...] = _dot(h2.astype(BF16), wq_ref[...])


def _merge_b(merged, xall, mod, wo, wq, off, nctx):
    bsz, lo, d = merged.shape
    nblk = lo // TM
    dq = wq.shape[-1]
    out = lambda b, j: (b, j, 0)
    return pl.pallas_call(
        _merge_b_kernel,
        out_shape=(jax.ShapeDtypeStruct((bsz, lo, d), F32), jax.ShapeDtypeStruct((bsz, lo, d), F32),
                   jax.ShapeDtypeStruct((bsz, lo, dq), F32)),
        grid=(bsz, nblk),
        in_specs=[
            pl.BlockSpec((None, TM, d), out),
            pl.BlockSpec((None, TM, d), lambda b, j: (b, j + off, 0)),
            pl.BlockSpec((None, None, N_MOD, d), lambda b, j: (b, (j + off >= nctx).astype(jnp.int32), 0, 0)),
            _const_spec(wo.shape),
            _const_spec(wq.shape),
        ],
        out_specs=(pl.BlockSpec((None, TM, d), out), pl.BlockSpec((None, TM, d), out),
                   pl.BlockSpec((None, TM, dq), out)),
        compiler_params=_cp(("parallel", "parallel")),
        name="merge_b",
    )(merged, xall, mod, wo, wq)


def _topk_rows(s, row_id, k):
    slot = lax.broadcasted_iota(jnp.int32, (k, s.shape[1]), 0)
    vals = jnp.zeros((k, s.shape[1]), F32)
    ids = jnp.zeros((k, s.shape[1]), F32)
    for r in range(k):
        m = jnp.max(s, axis=0, keepdims=True)
        am = jnp.min(jnp.where(s == m, row_id, 1e9), axis=0, keepdims=True)
        vals = jnp.where(slot == r, m, vals)
        ids = jnp.where(slot == r, am, ids)
        s = jnp.where(row_id == am, -jnp.inf, s)
    return vals, ids


def _pair_candidates():
    k = PEER_TOPK
    pieces = [("a", 0, 0, 16), ("a", 1, 0, 8), ("a", 2, 0, 8), ("a", 3, 0, 8), ("a", 4, 0, 8),
              ("b", 0, 0, 8), ("b", 1, 0, 8), ("b", 0, 8, 16)]
    ids, bias, seen = [], [], set()
    for kind, fixed, lo, hi in pieces:
        for r in range(lo, hi):
            a, b = (fixed, r) if kind == "a" else (r, fixed)
            ok = (a + 1) * (b + 1) <= k and (a, b) not in seen
            if ok:
                seen.add((a, b))
            ids.append(float(a * k + b))
            bias.append(0.0 if ok else -math.inf)
    assert len(seen) == sum(k // (a + 1) for a in range(k))
    return pieces, ids, bias


def _route_kernel(q_ref, keys_ref, cid_ref, cbias_ref, idx_ref, g_ref, it_scr, gt_scr):
    k = PEER_TOPK
    pieces, _, _ = _pair_candidates()
    key_id = lax.broadcasted_iota(jnp.int32, (PEER_NKEYS, q_ref.shape[0]), 0).astype(F32)
    for h in range(PEER_HEADS):
        sv, si = [], []
        for half in range(2):
            c0 = (h * 2 + half) * PEER_NKEYS
            qh = q_ref[:, c0:c0 + PEER_NKEYS].astype(BF16)
            s = _dot_nt(keys_ref[h, half], qh)
            v_, i_ = _topk_rows(s, key_id, k)
            sv.append(v_)
            si.append(i_)
        cand = jnp.concatenate(
            [sv[0][f:f + 1, :] + sv[1][lo:hi, :] if kind == "a" else sv[0][lo:hi, :] + sv[1][f:f + 1, :]
             for kind, f, lo, hi in pieces], axis=0) + cbias_ref[...]
        fv, fi = _topk_rows(cand, cid_ref[...], k)
        fa = jnp.floor(fi * (1.0 / k))
        fb = fi - fa * k
        i1 = jnp.zeros_like(fi)
        i2 = jnp.zeros_like(fi)
        for a in range(k):
            i1 = jnp.where(fa == a, si[0][a:a + 1, :], i1)
            i2 = jnp.where(fb == a, si[1][a:a + 1, :], i2)
        e = jnp.exp(fv - fv[0:1, :])
        gt_scr[h * k:(h + 1) * k, :] = e / jnp.sum(e, axis=0, keepdims=True)
        it_scr[h * k:(h + 1) * k, :] = i1 * PEER_NKEYS + i2
    g_ref[...] = gt_scr[...].T
    idx_ref[...] = it_scr[...].T.astype(jnp.int32)


def _route(q, keys):
    t, dq = q.shape
    tm = LANE
    nsel = PEER_HEADS * PEER_TOPK
    _, ids, bias = _pair_candidates()
    cid = jnp.broadcast_to(jnp.asarray(ids, F32)[:, None], (len(ids), tm))
    cbias = jnp.broadcast_to(jnp.asarray(bias, F32)[:, None], (len(bias), tm))
    return pl.pallas_call(
        _route_kernel,
        out_shape=(jax.ShapeDtypeStruct((t, nsel), jnp.int32), jax.ShapeDtypeStruct((t, nsel), F32)),
        grid=(t // tm,),
        in_specs=[pl.BlockSpec((tm, dq), lambda i: (i, 0)), _const_spec(keys.shape),
                  _const_spec(cid.shape), _const_spec(cbias.shape)],
        out_specs=(pl.BlockSpec((tm, nsel), lambda i: (i, 0)), pl.BlockSpec((tm, nsel), lambda i: (i, 0))),
        scratch_shapes=[pltpu.VMEM((nsel, tm), F32), pltpu.VMEM((nsel, tm), F32)],
        compiler_params=_cp(("parallel",)),
        name="peer_route",
    )(q, keys, cid, cbias)


PEER_TB = 64
PEER_SLOTS = 8
PEER_AHEAD = PEER_SLOTS - 1
HALF_SUB = 8
HI_MASK = -65536


def _pack_expert_table(u_all, v_all, layer):
    _, n_exp, d = u_all.shape
    tb = 256
    assert d == 2 * HALF_SUB * LANE and n_exp % tb == 0

    def pack(x):
        bits = lax.bitcast_convert_type(x.astype(BF16).astype(F32), jnp.int32)
        word = (bits[:, :d // 2] & HI_MASK) | lax.shift_right_logical(bits[:, d // 2:], 16)
        return word.reshape(x.shape[0], HALF_SUB, LANE)

    def body(u_ref, v_ref, o_ref):
        o_ref[:, 0:HALF_SUB, :] = pack(u_ref[...])
        o_ref[:, HALF_SUB:2 * HALF_SUB, :] = pack(v_ref[...])

    return pl.pallas_call(
        body,
        out_shape=jax.ShapeDtypeStruct((n_exp, 2 * HALF_SUB, LANE), jnp.int32),
        grid=(n_exp // tb,),
        in_specs=[pl.BlockSpec((None, tb, d), lambda i: (layer, i, 0)),
                  pl.BlockSpec((None, tb, d), lambda i: (layer, i, 0))],
        out_specs=pl.BlockSpec((tb, 2 * HALF_SUB, LANE), lambda i: (i, 0, 0)),
        compiler_params=_cp(("parallel",)),
        name="pack_experts",
    )(u_all, v_all)


def _unpack_pair(word):
    return lax.bitcast_convert_type(word & HI_MASK, F32), lax.bitcast_convert_type(word << 16, F32)


def _experts_kernel(idx_ref, g_ref, h_ref, x_ref, mod_ref, fin_ref, tab_ref, o_ref, *scratch, final_norm):
    bufs, sem = scratch[:PEER_SLOTS], scratch[PEER_SLOTS]
    nsel = PEER_HEADS * PEER_TOPK
    per = nsel // (2 * HALF_SUB)
    step = pl.program_id(0)

    def issue(t, slot, j0=0, j1=nsel):
        for j in range(j0, j1):
            e = idx_ref[0, t * nsel + j]
            pltpu.make_async_copy(tab_ref.at[e], bufs[slot].at[:, j, :], sem.at[slot]).start(priority=j % 2)

    def wait(slot):
        pltpu.make_async_copy(bufs[slot], bufs[slot], sem.at[slot]).wait()

    @pl.when(step == 0)
    def _():
        for s in range(PEER_AHEAD):
            issue(s, s)

    eye = lax.broadcasted_iota(jnp.int32, (nsel, nsel), 0) == lax.broadcasted_iota(jnp.int32, (nsel, nsel), 1)

    def group(t4, c):
        for s in range(PEER_SLOTS):
            t = t4 * PEER_SLOTS + s
            wait(s)
            nxt = (s + PEER_AHEAD) % PEER_SLOTS
            row = pl.ds(t, 1)
            hrow = h_ref[row, :]
            buf = bufs[s]
            acc = None
            for cc in range(HALF_SUB):
                u_hi, u_lo = _unpack_pair(buf[cc])
                term = (u_hi * hrow[:, cc * LANE:(cc + 1) * LANE]
                        + u_lo * hrow[:, (HALF_SUB + cc) * LANE:(HALF_SUB + cc + 1) * LANE])
                acc = term if acc is None else acc + term
                issue(t + PEER_AHEAD, nxt, cc * per, (cc + 1) * per)
            act = jnp.sum(acc, axis=-1, keepdims=True)
            gcol = jnp.sum(jnp.where(eye, g_ref[row, :], 0.0), axis=-1, keepdims=True)
            w = _gelu(act) * gcol
            outs = [None] * (2 * HALF_SUB)
            for cc in range(HALF_SUB):
                v_hi, v_lo = _unpack_pair(buf[HALF_SUB + cc])
                outs[cc] = jnp.sum(w * v_hi, axis=0, keepdims=True)
                outs[HALF_SUB + cc] = jnp.sum(w * v_lo, axis=0, keepdims=True)
                issue(t + PEER_AHEAD, nxt, (HALF_SUB + cc) * per, (HALF_SUB + cc + 1) * per)
            o_ref[row, :] = x_ref[row, :] + mod_ref[5:6, :] * jnp.concatenate(outs, axis=-1)
        return c

    lax.fori_loop(0, PEER_TB // PEER_SLOTS, group, 0)

    if final_norm:
        y = o_ref[...]
        o_ref[...] = y * lax.rsqrt(jnp.mean(y * y, axis=-1, keepdims=True) + EPS) * fin_ref[...]

    @pl.when(step == pl.num_programs(0) - 1)
    def _():
        for s in range(PEER_AHEAD):
            wait(s)


def _experts(idx, g, h2, x1, mod, fin_g, tab, off, nctx, final_norm):
    t_all, d = h2.shape
    bsz = mod.shape[0]
    nsel = idx.shape[-1]
    tb = PEER_TB
    nb = t_all // tb
    per_b = nb // bsz
    per = TM // tb
    assert d == 2 * HALF_SUB * LANE and tb % PEER_SLOTS == 0
    main = idx.reshape(nb, tb * nsel)
    idx_ext = jnp.concatenate([main, jnp.roll(main, -1, axis=0)[:, :PEER_AHEAD * nsel]], axis=1)
    idx_ext = idx_ext.reshape(nb, 1, (tb + PEER_AHEAD) * nsel)
    tokb = lambda i: (i, 0)

    def mod_map(i):
        return (i // per_b, ((i % per_b) // per + off >= nctx).astype(jnp.int32), 0, 0)

    return pl.pallas_call(
        functools.partial(_experts_kernel, final_norm=final_norm),
        out_shape=jax.ShapeDtypeStruct((t_all, d), F32),
        grid=(nb,),
        in_specs=[
            pl.BlockSpec((None, 1, (tb + PEER_AHEAD) * nsel), lambda i: (i, 0, 0), memory_space=pltpu.SMEM),
            pl.BlockSpec((tb, nsel), tokb),
            pl.BlockSpec((tb, d), tokb),
            pl.BlockSpec((tb, d), tokb),
            pl.BlockSpec((None, None, N_MOD, d), mod_map),
            _const_spec((1, d)),
            pl.BlockSpec(memory_space=pl.ANY),
        ],
        out_specs=pl.BlockSpec((tb, d), tokb),
        scratch_shapes=[pltpu.VMEM((2 * HALF_SUB, nsel, LANE), jnp.int32) for _ in range(PEER_SLOTS)]
        + [pltpu.SemaphoreType.DMA((PEER_SLOTS,))],
        compiler_params=_cp(("arbitrary",)),
        name="peer_experts",
    )(idx_ext, g, h2, x1, mod, fin_g, tab)


def _pos_embed_2d(n, d):
    n_rows = n // GRID_W
    quarter = d // 4
    omega = 1.0 / (POS_THETA ** (jnp.arange(quarter, dtype=F32) / quarter))

    def axis_code(length):
        ang = jnp.arange(length, dtype=F32)[:, None] * omega[None]
        return jnp.concatenate([jnp.sin(ang), jnp.cos(ang)], axis=-1)

    er = jnp.broadcast_to(axis_code(n_rows)[:, None], (n_rows, GRID_W, d // 2))
    ec = jnp.broadcast_to(axis_code(GRID_W)[None], (n_rows, GRID_W, d // 2))
    return jnp.concatenate([er, ec], axis=-1).reshape(n, d)


def kernel(x, c, ctx, c_ctx, w_mod, b_mod, w_in, gmlp_ln_g, gmlp_ln_b, gmlp_w_s, gmlp_b_s, hgrn_lb_logits, hgrn_norm_g, s5_lam_re, s5_lam_im, s5_log_step, s5_b_re, s5_b_im, s5_c_re, s5_c_im, s5_d, s5_w_glu, w_branch_a, w_branch_b, w_branch_c, w_out, peer_w_q, peer_sub_keys, peer_u, peer_v, final_norm_g):
    bsz, n, d = x.shape
    ctx_len = ctx.shape[1]
    depth = w_mod.shape[0]
    nctx = ctx_len // TM
    da = gmlp_ln_g.shape[-1]
    db = hgrn_norm_g.shape[-1]
    dc = s5_d.shape[-1]
    assert n % TM == 0 and ctx_len % TM == 0 and bsz % SUB == 0
    assert da == db and d == 2 * da and dc % LANE == 0 and dc <= da
    assert w_in.shape[-1] == 2 * da + 5 * db + dc + 3 * d

    xall = _embed(x, ctx, _pos_embed_2d(n, d).astype(x.dtype))

    rows = -(-(bsz + 1) // 8) * 8
    cc = jnp.zeros((rows, d), F32).at[0].set(c_ctx).at[1:bsz + 1].set(c)
    mod_all = _modulation(cc, w_mod, b_mod)

    gamma = jax.nn.softmax(hgrn_lb_logits.astype(F32), axis=0)
    lower_bounds = jnp.cumsum(gamma, axis=0) - gamma[0:1]

    for l in range(depth):
        last = l == depth - 1
        off = nctx if last else 0
        mod_c = jnp.broadcast_to(mod_all[l, 0][None], (bsz, N_MOD * d))
        mod = jnp.stack([mod_c, mod_all[l, 1:bsz + 1]], axis=1).reshape(bsz, 2, N_MOD, d)

        cu0 = 2 * da + 5 * db
        w_pad = jnp.concatenate([w_in[l][:, :cu0 + dc], jnp.zeros((d, da - dc), F32), w_in[l][:, cu0 + dc:]],
                                axis=1).astype(BF16)
        p = _in_proj(xall, mod, w_pad, ctx_len)

        bias = jnp.repeat(gmlp_b_s[l].T, da // G_A, axis=1)
        ya = _gmlp(p, gmlp_ln_g[l].reshape(1, da), gmlp_ln_b[l].reshape(1, da), gmlp_w_s[l].astype(BF16), bias)

        o_f = _hgrn(p, lower_bounds[l, 0].reshape(1, db), False, nctx)
        o_b = _hgrn(p, lower_bounds[l, 1].reshape(1, db), True, nctx)

        ys = []
        for r in range(2):
            prm = _s5_params(s5_lam_re[l, r], s5_lam_im[l, r], s5_log_step[l, r], s5_b_re[l, r], s5_b_im[l, r],
                             s5_c_re[l, r], s5_c_im[l, r])
            ys.append(_s5(p, (cu0 // da, da), *prm, rev=bool(r), ctx_len=ctx_len))

        merged = _merge_a(ya, o_f, o_b, p, ys[0], ys[1], hgrn_norm_g[l].reshape(1, db), s5_d[l].reshape(1, dc),
                          s5_w_glu[l].astype(BF16), w_branch_a[l].astype(BF16), w_branch_b[l].astype(BF16),
                          w_branch_c[l].astype(BF16), off)
        x1, h2, q = _merge_b(merged, xall, mod, w_out[l].astype(BF16), peer_w_q[l].astype(BF16), off, nctx)

        lo = x1.shape[1]
        idx, g = _route(q.reshape(bsz * lo, -1), peer_sub_keys[l].astype(BF16))
        tab = _pack_expert_table(peer_u, peer_v, l)
        xall = _experts(idx, g, h2.reshape(bsz * lo, d), x1.reshape(bsz * lo, d), mod,
                        final_norm_g.astype(F32).reshape(1, d), tab, off, nctx, final_norm=last)
        xall = xall.reshape(bsz, lo, d)

    return xall
```
